```python
import jax
import jax.numpy as jnp
from jax import lax
import numpy as np

D_MODEL = 4096
BATCH = 4
SEQ = 2048
DEPTH = 1

HEAD_DIM = 128
NSA_HEADS = 16
NSA_KV_GROUPS = 4
NSA_Q_PER_KV = NSA_HEADS // NSA_KV_GROUPS
FOX_HEADS = 16
NSA_WIDTH = NSA_HEADS * HEAD_DIM
FOX_WIDTH = FOX_HEADS * HEAD_DIM
N_NSA_BRANCHES = 3
N_MERGE_BRANCHES = 2
CMP_BLOCK = 32
CMP_STRIDE = 16
SEL_BLOCK = 64
SEL_TOPN = 16
WINDOW = 512
Q_BLOCK = 128
SEL_Q_BLOCK = 16
ROPE_THETA = 500000.0
ROT_DIM = HEAD_DIM // 4
D_FF = 4 * D_MODEL
RMS_EPS = 1e-6
ATTN_SCALE = HEAD_DIM ** -0.5

COLS_NSA_Q = NSA_WIDTH
COLS_NSA_KV = N_NSA_BRANCHES * 2 * NSA_KV_GROUPS * HEAD_DIM
COLS_NSA_GATE = N_NSA_BRANCHES * NSA_HEADS
COLS_FOX_QKV = 3 * FOX_WIDTH
COLS_FOX_F = FOX_HEADS
COLS_MERGE = N_MERGE_BRANCHES * D_MODEL
D_IN = COLS_NSA_Q + COLS_NSA_KV + COLS_NSA_GATE + COLS_FOX_QKV + COLS_FOX_F + COLS_MERGE
SPLIT_POINTS = [COLS_NSA_Q,
                COLS_NSA_Q + COLS_NSA_KV,
                COLS_NSA_Q + COLS_NSA_KV + COLS_NSA_GATE,
                COLS_NSA_Q + COLS_NSA_KV + COLS_NSA_GATE + COLS_FOX_QKV,
                COLS_NSA_Q + COLS_NSA_KV + COLS_NSA_GATE + COLS_FOX_QKV + COLS_FOX_F]

kernel_name = "hybrid_nsa_fox_gated_block"


def _rmsnorm(x, g):
    xf = x.astype(jnp.float32)
    y = xf * lax.rsqrt(jnp.mean(xf * xf, axis=-1, keepdims=True) + RMS_EPS)
    return (y * g.astype(jnp.float32)).astype(x.dtype)


def _partial_rope(x, pos):
    half = ROT_DIM // 2
    inv = ROPE_THETA ** (-jnp.arange(half, dtype=jnp.float32) / half)
    ang = jnp.asarray(pos, dtype=jnp.float32)[:, None] * inv[None, :]
    cos = jnp.cos(ang)[:, None, :].astype(x.dtype)
    sin = jnp.sin(ang)[:, None, :].astype(x.dtype)
    x1 = x[..., :half]
    x2 = x[..., half:ROT_DIM]
    return jnp.concatenate([x1 * cos - x2 * sin, x1 * sin + x2 * cos, x[..., ROT_DIM:]], axis=-1)


def _masked_softmax(s, mask):
    s = jnp.where(mask, s.astype(jnp.float32), -jnp.inf)
    m = jnp.max(s, axis=-1, keepdims=True)
    m = jnp.where(jnp.isfinite(m), m, 0.0)
    p = jnp.exp(s - m)
    d = jnp.sum(p, axis=-1, keepdims=True)
    return p / jnp.where(d > 0, d, 1.0)


def _compress(x, pos_emb, w1, w2):
    B, T, G, dh = x.shape
    n_cmp = (T - CMP_BLOCK) // CMP_STRIDE + 1
    idx = np.arange(n_cmp)[:, None] * CMP_STRIDE + np.arange(CMP_BLOCK)[None, :]
    blocks = x[:, idx] + pos_emb[None, None, :, None, :]
    blocks = blocks.transpose(0, 1, 3, 2, 4).reshape(B, n_cmp, G, CMP_BLOCK * dh)
    return jax.nn.gelu(blocks @ w1) @ w2


def _nsa(q, kv, gate_logits, nsa_k_norm, cmp_pos_k, cmp_pos_v, w_cmp_k1, w_cmp_k2, w_cmp_v1, w_cmp_v2):
    B, T, H, dh = q.shape
    G, R = NSA_KV_GROUPS, NSA_Q_PER_KV
    pos = np.arange(T)
    q_g = q.reshape(B, T, G, R, dh)
    kc, vc = kv[:, :, 0, 0], kv[:, :, 0, 1]
    ks = _partial_rope(_rmsnorm(kv[:, :, 1, 0], nsa_k_norm), pos)
    vs = kv[:, :, 1, 1]
    kw = _partial_rope(_rmsnorm(kv[:, :, 2, 0], nsa_k_norm), pos)
    vw = kv[:, :, 2, 1]

    n_cmp = (T - CMP_BLOCK) // CMP_STRIDE + 1
    end_pos = np.arange(n_cmp) * CMP_STRIDE + CMP_BLOCK - 1
    k_cmp = _compress(kc, cmp_pos_k, w_cmp_k1, w_cmp_k2)
    k_cmp = _partial_rope(_rmsnorm(k_cmp, nsa_k_norm), end_pos)
    v_cmp = _compress(vc, cmp_pos_v, w_cmp_v1, w_cmp_v2)
    s_c = jnp.einsum('btgrd,bngd->bgrtn', q_g, k_cmp) * ATTN_SCALE
    p_c = _masked_softmax(s_c, end_pos[None, :] <= pos[:, None])
    o_c = jnp.einsum('bgrtn,bngd->btgrd', p_c.astype(v_cmp.dtype), v_cmp)

    n_slc = T // SEL_BLOCK
    ci = np.arange(n_cmp)[:, None] * CMP_STRIDE
    sj = np.arange(n_slc)[None, :] * SEL_BLOCK
    overlap = ((ci < sj + SEL_BLOCK) & (ci + CMP_BLOCK > sj)).astype(np.float32)
    imp = jnp.einsum('bgtn,nj->bgtj', jnp.sum(p_c, axis=2), overlap)
    cur = pos // SEL_BLOCK
    jj = np.arange(n_slc)
    forced = (jj[None, :] == 0) | (jj[None, :] == cur[:, None]) | (jj[None, :] == cur[:, None] - 1)
    causal = jj[None, :] <= cur[:, None]
    score = jnp.where(forced, jnp.inf, jnp.where(causal, imp, -jnp.inf))
    top_n = min(SEL_TOPN, n_slc)
    sel_val, sel_idx = lax.top_k(score, top_n)
    sel_ok = sel_val > -jnp.inf

    kb = ks.reshape(B, n_slc, SEL_BLOCK, G, dh).transpose(0, 3, 1, 2, 4)
    vb = vs.reshape(B, n_slc, SEL_BLOCK, G, dh).transpose(0, 3, 1, 2, 4)
    nb = T // SEL_Q_BLOCK
    bi = jnp.arange(B)[:, None, None, None]
    gi = jnp.arange(G)[None, :, None, None]

    def sel_block(args):
        cblk, qb, ib, okb = args
        kg = kb[bi, gi, ib]
        vg = vb[bi, gi, ib]
        tq = cblk * SEL_Q_BLOCK + jnp.arange(SEL_Q_BLOCK)
        kpos = ib[..., None] * SEL_BLOCK + jnp.arange(SEL_BLOCK)
        mask = okb[..., None] & (kpos <= tq[None, None, :, None, None])
        s = jnp.einsum('bqgrd,bgqnld->bgrqnl', qb, kg) * ATTN_SCALE
        s = s.reshape(B, G, R, SEL_Q_BLOCK, top_n * SEL_BLOCK)
        p = _masked_softmax(s, mask.reshape(B, G, 1, SEL_Q_BLOCK, top_n * SEL_BLOCK))
        p = p.reshape(B, G, R, SEL_Q_BLOCK, top_n, SEL_BLOCK).astype(vg.dtype)
        return jnp.einsum('bgrqnl,bgqnld->bqgrd', p, vg)

    xs_sel = (jnp.arange(nb),
              q_g.reshape(B, nb, SEL_Q_BLOCK, G, R, dh).swapaxes(0, 1),
              sel_idx.reshape(B, G, nb, SEL_Q_BLOCK, top_n).transpose(2, 0, 1, 3, 4),
              sel_ok.reshape(B, G, nb, SEL_Q_BLOCK, top_n).transpose(2, 0, 1, 3, 4))
    o_s = lax.map(sel_block, xs_sel).swapaxes(0, 1).reshape(B, T, G, R, dh)

    kpad = jnp.pad(kw, ((0, 0), (WINDOW, 0), (0, 0), (0, 0)))
    vpad = jnp.pad(vw, ((0, 0), (WINDOW, 0), (0, 0), (0, 0)))
    nq = T // Q_BLOCK
    span = WINDOW + Q_BLOCK

    def win_block(args):
        cblk, qb = args
        start = cblk * Q_BLOCK
        kband = lax.dynamic_slice_in_dim(kpad, start, span, axis=1)
        vband = lax.dynamic_slice_in_dim(vpad, start, span, axis=1)
        tq = start + jnp.arange(Q_BLOCK)
        kpos = start - WINDOW + jnp.arange(span)
        mask = (kpos[None, :] <= tq[:, None]) & (kpos[None, :] > tq[:, None] - WINDOW) & (kpos[None, :] >= 0)
        s = jnp.einsum('bqgrd,bkgd->bgrqk', qb, kband) * ATTN_SCALE
        p = _masked_softmax(s, mask).astype(vband.dtype)
        return jnp.einsum('bgrqk,bkgd->bqgrd', p, vband)

    xs_win = (jnp.arange(nq), q_g.reshape(B, nq, Q_BLOCK, G, R, dh).swapaxes(0, 1))
    o_w = lax.map(win_block, xs_win).swapaxes(0, 1).reshape(B, T, G, R, dh)

    g = jax.nn.sigmoid(gate_logits).reshape(B, T, G, R, N_NSA_BRANCHES)
    o = g[..., 0:1] * o_c + g[..., 1:2] * o_s + g[..., 2:3] * o_w
    return o.reshape(B, T, NSA_WIDTH)


def _fox(q, k, v, f_logit):
    B, T, H, dh = q.shape
    cum = jnp.cumsum(jax.nn.log_sigmoid(f_logit.astype(jnp.float32)), axis=1).transpose(0, 2, 1)
    outs = []
    for i in range(T // Q_BLOCK):
        qs, qe = i * Q_BLOCK, (i + 1) * Q_BLOCK
        s = jnp.einsum('bqhd,bkhd->bhqk', q[:, qs:qe], k[:, :qe]).astype(jnp.float32) * ATTN_SCALE
        bias = cum[:, :, qs:qe, None] - cum[:, :, None, :qe]
        mask = np.arange(qs, qe)[:, None] >= np.arange(qe)[None, :]
        p = _masked_softmax(s + bias, mask).astype(v.dtype)
        outs.append(jnp.einsum('bhqk,bkhd->bqhd', p, v[:, :qe]))
    return jnp.concatenate(outs, axis=1).reshape(B, T, FOX_WIDTH)


def setup_inputs(seed: int = 0) -> dict:
    key = jax.random.key(seed)
    ks = jax.random.split(key, 24)
    f32 = jnp.float32
    L = DEPTH

    def nrm(k, shape, fan_in, mult=1.0):
        return jax.random.normal(k, shape, f32) * (mult * fan_in ** -0.5)

    def gain(k, shape):
        return 1.0 + 0.02 * jax.random.normal(k, shape, f32)

    return {
        "x": jax.random.normal(ks[0], (BATCH, SEQ, D_MODEL), f32),
        "c": jax.random.normal(ks[1], (BATCH, D_MODEL), f32),
        "w_ada": nrm(ks[2], (L, D_MODEL, 6 * D_MODEL), D_MODEL, 0.5),
        "b_ada": 0.01 * jax.random.normal(ks[3], (L, 6 * D_MODEL), f32),
        "norm1_g": gain(ks[4], (L, D_MODEL)),
        "norm2_g": gain(ks[5], (L, D_MODEL)),
        "w_in": nrm(ks[6], (L, D_MODEL, D_IN), D_MODEL),
        "b_forget": jax.random.uniform(ks[7], (L, FOX_HEADS), f32, 1.0, 4.0),
        "nsa_q_norm": gain(ks[8], (L, HEAD_DIM)),
        "nsa_k_norm": gain(ks[9], (L, HEAD_DIM)),
        "fox_q_norm": gain(ks[10], (L, HEAD_DIM)),
        "fox_k_norm": gain(ks[11], (L, HEAD_DIM)),
        "cmp_pos_k": 0.1 * jax.random.normal(ks[12], (L, CMP_BLOCK, HEAD_DIM), f32),
        "cmp_pos_v": 0.1 * jax.random.normal(ks[13], (L, CMP_BLOCK, HEAD_DIM), f32),
        "w_cmp_k1": nrm(ks[14], (L, CMP_BLOCK * HEAD_DIM, HEAD_DIM), CMP_BLOCK * HEAD_DIM),
        "w_cmp_k2": nrm(ks[15], (L, HEAD_DIM, HEAD_DIM), HEAD_DIM),
        "w_cmp_v1": nrm(ks[16], (L, CMP_BLOCK * HEAD_DIM, HEAD_DIM), CMP_BLOCK * HEAD_DIM),
        "w_cmp_v2": nrm(ks[17], (L, HEAD_DIM, HEAD_DIM), HEAD_DIM),
        "w_up_nsa": nrm(ks[18], (L, NSA_WIDTH, D_MODEL), NSA_WIDTH),
        "w_up_fox": nrm(ks[19], (L, FOX_WIDTH, D_MODEL), FOX_WIDTH),
        "w_out": nrm(ks[20], (L, D_MODEL, D_MODEL), D_MODEL),
        "w_ff1": nrm(ks[21], (L, D_MODEL, D_FF), D_MODEL),
        "w_ff2": nrm(ks[22], (L, D_FF, D_MODEL), D_FF),
    }


def reference(x, c, w_ada, b_ada, norm1_g, norm2_g, w_in, b_forget, nsa_q_norm, nsa_k_norm,
              fox_q_norm, fox_k_norm, cmp_pos_k, cmp_pos_v, w_cmp_k1, w_cmp_k2, w_cmp_v1, w_cmp_v2,
              w_up_nsa, w_up_fox, w_out, w_ff1, w_ff2):
    B, T, D = x.shape
    pos = np.arange(T)
    for l in range(DEPTH):
        mod = jax.nn.silu(c) @ w_ada[l] + b_ada[l]
        shift1, scale1, gate1, shift2, scale2, gate2 = [m[:, None, :] for m in jnp.split(mod, 6, axis=-1)]

        h = _rmsnorm(x, norm1_g[l]) * (1.0 + scale1) + shift1
        proj = h @ w_in[l]
        nsa_q, nsa_kv, nsa_gate, fox_qkv, fox_f, merge_g = jnp.split(proj, SPLIT_POINTS, axis=-1)

        q_n = _partial_rope(_rmsnorm(nsa_q.reshape(B, T, NSA_HEADS, HEAD_DIM), nsa_q_norm[l]), pos)
        kv_n = nsa_kv.reshape(B, T, N_NSA_BRANCHES, 2, NSA_KV_GROUPS, HEAD_DIM)
        o_nsa = _nsa(q_n, kv_n, nsa_gate, nsa_k_norm[l], cmp_pos_k[l], cmp_pos_v[l],
                     w_cmp_k1[l], w_cmp_k2[l], w_cmp_v1[l], w_cmp_v2[l])

        qkv_f = fox_qkv.reshape(B, T, 3, FOX_HEADS, HEAD_DIM)
        q_f = _rmsnorm(qkv_f[:, :, 0], fox_q_norm[l])
        k_f = _rmsnorm(qkv_f[:, :, 1], fox_k_norm[l])
        o_fox = _fox(q_f, k_f, qkv_f[:, :, 2], fox_f + b_forget[l])

        g_merge = jax.nn.sigmoid(merge_g)
        y = g_merge[..., :D] * (o_nsa @ w_up_nsa[l]) + g_merge[..., D:] * (o_fox @ w_up_fox[l])
        x = x + gate1 * (y @ w_out[l])

        h2 = _rmsnorm(x, norm2_g[l]) * (1.0 + scale2) + shift2
        x = x + gate2 * (jnp.square(jax.nn.relu(h2 @ w_ff1[l])) @ w_ff2[l])
    return x
```

```python
import functools

import numpy as np
import jax
import jax.numpy as jnp
from jax import lax
from jax.experimental import pallas as pl
from jax.experimental.pallas import tpu as pltpu

F32 = jnp.float32
BF16 = jnp.bfloat16

HEAD_DIM = 128
NSA_HEADS = 16
NSA_GROUPS = 4
NSA_REP = NSA_HEADS // NSA_GROUPS
FOX_HEADS = 16
CMP_BLOCK = 32
CMP_STRIDE = 16
SEL_BLOCK = 64
SEL_SHIFT = SEL_BLOCK.bit_length() - 1
SEL_TOPN = 16
WINDOW = 512
ROPE_THETA = 500000.0
ROT_DIM = HEAD_DIM // 4
RMS_EPS = 1e-6
ATTN_SCALE = HEAD_DIM ** -0.5

NEG = -1e30
BIG = 1e30
LANES = 128
VMEM_LIMIT = 56 * 1024 * 1024


def _cparams(*sem):
    return pltpu.CompilerParams(dimension_semantics=sem, vmem_limit_bytes=VMEM_LIMIT)


def _dot(a, b):
    return jnp.dot(a, b, preferred_element_type=F32)


def _dot_nt(a, b):
    return lax.dot_general(a, b, (((1,), (1,)), ((), ())), preferred_element_type=F32)


def _ada_kernel(c_ref, w_ref, b_ref, o_ref):
    c = c_ref[...]
    s = (c * jax.nn.sigmoid(c)).astype(BF16)
    o_ref[...] = _dot(s, w_ref[...].astype(BF16)) + b_ref[...]


def _ada(c_pad, w, b):
    rows, d = c_pad.shape
    n = w.shape[1]
    tn = 1024
    return pl.pallas_call(
        _ada_kernel,
        grid=(n // tn,),
        in_specs=[pl.BlockSpec((rows, d), lambda j: (0, 0)),
                  pl.BlockSpec((d, tn), lambda j: (0, j)),
                  pl.BlockSpec((1, tn), lambda j: (0, j))],
        out_specs=pl.BlockSpec((rows, tn), lambda j: (0, j)),
        out_shape=jax.ShapeDtypeStruct((rows, n), F32),
        compiler_params=_cparams("arbitrary"),
        name="ada",
    )(c_pad, w, b)


def _normmod_kernel(x_ref, g_ref, sc_ref, sh_ref, o_ref):
    x = x_ref[...]
    ms = jnp.mean(x * x, axis=-1, keepdims=True)
    y = x * lax.rsqrt(ms + RMS_EPS) * g_ref[...]
    o_ref[...] = (y * (1.0 + sc_ref[...]) + sh_ref[...]).astype(o_ref.dtype)


def _normmod(x2, g, scale, shift, seq):
    m, d = x2.shape
    tr = 256
    per = seq // tr
    return pl.pallas_call(
        _normmod_kernel,
        grid=(m // tr,),
        in_specs=[pl.BlockSpec((tr, d), lambda i: (i, 0)),
                  pl.BlockSpec((1, d), lambda i: (0, 0)),
                  pl.BlockSpec((None, 1, d), lambda i: (i // per, 0, 0)),
                  pl.BlockSpec((None, 1, d), lambda i: (i // per, 0, 0))],
        out_specs=pl.BlockSpec((tr, d), lambda i: (i, 0)),
        out_shape=jax.ShapeDtypeStruct((m, d), BF16),
        compiler_params=_cparams("arbitrary"),
        name="normmod",
    )(x2, g, scale, shift)


def _ep_raw(acc, o_ref):
    o_ref[...] = acc.astype(o_ref.dtype)


def _ep_sigmoid(acc, o_ref):
    o_ref[...] = jax.nn.sigmoid(acc).astype(o_ref.dtype)


def _ep_relu2(acc, o_ref):
    r = jnp.maximum(acc, 0.0)
    o_ref[...] = (r * r).astype(o_ref.dtype)


def _ep_residual(acc, o_ref, x_ref, g_ref):
    o_ref[...] = x_ref[...] + g_ref[...] * acc


def _head_norm(a, g):
    ms = jnp.mean(a * a, axis=-1, keepdims=True)
    return a * lax.rsqrt(ms + RMS_EPS) * g


def _rope(a, c, sa, sb):
    return a * c + pltpu.roll(a, LANES - ROT_DIM // 2, 1) * sa + pltpu.roll(a, ROT_DIM // 2, 1) * sb


def _ep_headnorm(acc, o_ref, g_ref):
    g = g_ref[...]
    for h in range(acc.shape[1] // HEAD_DIM):
        sl = slice(h * HEAD_DIM, (h + 1) * HEAD_DIM)
        o_ref[:, sl] = _head_norm(acc[:, sl], g).astype(o_ref.dtype)


def _ep_headnorm_rope(acc, o_ref, g_ref, c_ref, sa_ref, sb_ref):
    g = g_ref[...]
    c, sa, sb = c_ref[...], sa_ref[...], sb_ref[...]
    for h in range(acc.shape[1] // HEAD_DIM):
        sl = slice(h * HEAD_DIM, (h + 1) * HEAD_DIM)
        o_ref[:, sl] = _rope(_head_norm(acc[:, sl], g), c, sa, sb).astype(o_ref.dtype)


def _mm_kernel(a_ref, b_ref, *rest, epilogue, n_extra):
    extras = rest[:n_extra]
    o_ref = rest[n_extra]
    epilogue(_dot(a_ref[...], b_ref[...]), o_ref, *extras)


def _matmul(a, b, epilogue, out_dtype, extras=(), extra_specs=(), tm=1024, tn=1024, name="mm"):
    m, k = a.shape
    n = b.shape[1]
    tn = min(tn, n)
    return pl.pallas_call(
        functools.partial(_mm_kernel, epilogue=epilogue, n_extra=len(extras)),
        grid=(n // tn, m // tm),
        in_specs=[pl.BlockSpec((tm, k), lambda j, i: (i, 0)),
                  pl.BlockSpec((k, tn), lambda j, i: (0, j))] + list(extra_specs),
        out_specs=pl.BlockSpec((tm, tn), lambda j, i: (i, j)),
        out_shape=jax.ShapeDtypeStruct((m, n), out_dtype),
        compiler_params=_cparams("arbitrary", "arbitrary"),
        name=name,
    )(a, b, *extras)


def _mmk_kernel(a_ref, b_ref, x_ref, g_ref, o_ref, acc_ref, *, nk):
    kk = pl.program_id(2)

    @pl.when(kk == 0)
    def _():
        acc_ref[...] = jnp.zeros_like(acc_ref)

    acc_ref[...] += _dot(a_ref[...], b_ref[...])

    @pl.when(kk == nk - 1)
    def _():
        o_ref[...] = x_ref[...] + g_ref[...] * acc_ref[...]


def _matmul_k_residual(a, b, x, gate, seq, tm=1024, tn=1024, tk=2048, name="mmk"):
    m, k = a.shape
    n = b.shape[1]
    per = seq // tm
    nk = k // tk
    return pl.pallas_call(
        functools.partial(_mmk_kernel, nk=nk),
        grid=(m // tm, n // tn, nk),
        in_specs=[pl.BlockSpec((tm, tk), lambda i, j, kk: (i, kk)),
                  pl.BlockSpec((tk, tn), lambda i, j, kk: (kk, j)),
                  pl.BlockSpec((tm, tn), lambda i, j, kk: (i, j)),
                  pl.BlockSpec((None, 1, tn), lambda i, j, kk: (i // per, 0, j))],
        out_specs=pl.BlockSpec((tm, tn), lambda i, j, kk: (i, j)),
        out_shape=jax.ShapeDtypeStruct((m, n), F32),
        scratch_shapes=[pltpu.VMEM((tm, tn), F32)],
        compiler_params=_cparams("arbitrary", "arbitrary", "arbitrary"),
        name=name,
    )(a, b, x, gate)


def _compress_kernel(xk_ref, xv_ref, pek_ref, pev_ref, w1k_ref, w2k_ref, w1v_ref, w2v_ref,
                     g_ref, c_ref, sa_ref, sb_ref, ko_ref, vo_ref):
    half = CMP_STRIDE * HEAD_DIM

    def comp(x_ref, pe_ref, w1_ref, w2_ref):
        x = x_ref[...]
        lo = (x + pe_ref[0:1, :]).astype(BF16)
        hi = (x + pe_ref[1:2, :]).astype(BF16)
        p = _dot(lo, w1_ref[0:half, :])
        q = _dot(hi, w1_ref[half:2 * half, :])
        h = p + pltpu.roll(q, q.shape[0] - 1, 0)
        h = jax.nn.gelu(h, approximate=True)
        return _dot(h.astype(BF16), w2_ref[...])

    kc = comp(xk_ref, pek_ref, w1k_ref, w2k_ref)
    kc = _rope(_head_norm(kc, g_ref[...]), c_ref[...], sa_ref[...], sb_ref[...])
    ko_ref[...] = kc.astype(ko_ref.dtype)
    vo_ref[...] = comp(xv_ref, pev_ref, w1v_ref, w2v_ref).astype(vo_ref.dtype)


def _compress(xk, xv, pek, pev, w1k, w2k, w1v, w2v, g, c, sa, sb):
    b, gr, nr, wd = xk.shape
    xspec = pl.BlockSpec((None, None, nr, wd), lambda i, j: (i, j, 0, 0))
    full = lambda arr: pl.BlockSpec(arr.shape, lambda i, j: (0,) * arr.ndim)
    ospec = pl.BlockSpec((None, None, nr, HEAD_DIM), lambda i, j: (i, j, 0, 0))
    oshape = jax.ShapeDtypeStruct((b, gr, nr, HEAD_DIM), BF16)
    return pl.pallas_call(
        _compress_kernel,
        grid=(b, gr),
        in_specs=[xspec, xspec, full(pek), full(pev), full(w1k), full(w2k), full(w1v), full(w2v),
                  full(g), full(c), full(sa), full(sb)],
        out_specs=[ospec, ospec],
        out_shape=[oshape, oshape],
        compiler_params=_cparams("arbitrary", "arbitrary"),
        name="compress",
    )(xk, xv, pek, pev, w1k, w2k, w1v, w2v, g, c, sa, sb)


def _cmp_kernel(q_ref, k_ref, v_ref, ov_ref, oc_ref, sel_ref, *, tt, n_cmp, n_slc):
    t = pl.program_id(2) * tt + lax.broadcasted_iota(jnp.int32, (tt, LANES), 0)
    n = lax.broadcasted_iota(jnp.int32, (tt, LANES), 1)
    valid = (n < n_cmp) & (n * CMP_STRIDE + (CMP_BLOCK - 1) <= t)
    k = k_ref[...]
    v = v_ref[...]
    psum = jnp.zeros((tt, LANES), F32)
    for r in range(NSA_REP):
        sl = slice(r * HEAD_DIM, (r + 1) * HEAD_DIM)
        s = jnp.where(valid, _dot_nt(q_ref[:, sl], k) * ATTN_SCALE, NEG)
        m = jnp.max(s, axis=-1, keepdims=True)
        p = jnp.where(valid, jnp.exp(s - m), 0.0)
        d = jnp.sum(p, axis=-1, keepdims=True)
        p = p / jnp.where(d > 0, d, 1.0)
        psum = psum + p
        oc_ref[:, sl] = _dot(p.astype(BF16), v)
    imp = jnp.dot(psum, ov_ref[...], preferred_element_type=F32, precision=lax.Precision.HIGHEST)
    cur = t >> SEL_SHIFT
    forced = (n == 0) | (n == cur) | (n == cur - 1)
    score = jnp.where(forced, BIG, jnp.where(n <= cur, imp, -BIG))
    cnt = jnp.zeros((tt, LANES), F32)
    for jp in range(n_slc):
        col = score[:, jp:jp + 1]
        beats = (col > score) | ((col == score) & (jp < n))
        cnt = cnt + jnp.where(beats, 1.0, 0.0)
    sel = (cnt < SEL_TOPN) & (score > -0.5 * BIG)
    sel_ref[...] = jnp.where(sel, 1.0, 0.0).astype(sel_ref.dtype)


def _cmp_attention(q, kc, vc, overlap, seq):
    m, _ = q.shape
    b = m // seq
    tt = 512
    per = seq // tt
    n_cmp = (seq - CMP_BLOCK) // CMP_STRIDE + 1
    kspec = pl.BlockSpec((None, None, LANES, HEAD_DIM), lambda i, g, j: (i, g, 0, 0))
    return pl.pallas_call(
        functools.partial(_cmp_kernel, tt=tt, n_cmp=n_cmp, n_slc=seq // SEL_BLOCK),
        grid=(b, NSA_GROUPS, per),
        in_specs=[pl.BlockSpec((tt, NSA_REP * HEAD_DIM), lambda i, g, j: (i * per + j, g)),
                  kspec, kspec,
                  pl.BlockSpec((LANES, LANES), lambda i, g, j: (0, 0))],
        out_specs=[pl.BlockSpec((tt, NSA_REP * HEAD_DIM), lambda i, g, j: (i * per + j, g)),
                   pl.BlockSpec((None, None, tt, LANES), lambda i, g, j: (i, g, j, 0))],
        out_shape=[jax.ShapeDtypeStruct((m, NSA_HEADS * HEAD_DIM), F32),
                   jax.ShapeDtypeStruct((b, NSA_GROUPS, seq, LANES), BF16)],
        compiler_params=_cparams("arbitrary", "arbitrary", "arbitrary"),
        name="cmp_attn",
    )(q, kc, vc, overlap)


def _flash_first(s, v, m_sc, l_sc, acc_sc):
    m = jnp.max(s, axis=-1, keepdims=True)
    p = jnp.exp(s - m)
    m_sc[...] = m
    l_sc[...] = jnp.sum(p, axis=-1, keepdims=True)
    acc_sc[...] = _dot(p.astype(BF16), v)


def _flash_next(s, v, m_sc, l_sc, acc_sc):
    m_old = m_sc[...]
    m = jnp.maximum(m_old, jnp.max(s, axis=-1, keepdims=True))
    a = jnp.exp(m_old - m)
    p = jnp.exp(s - m)
    l_sc[...] = a * l_sc[...] + jnp.sum(p, axis=-1, keepdims=True)
    acc_sc[...] = a * acc_sc[...] + _dot(p.astype(BF16), v)
    m_sc[...] = m


def _nsa_kernel(q_ref, ks_ref, vs_ref, kw_ref, vw_ref, sel_ref, oc_ref, gl_ref, o_ref,
                m_sc, l_sc, acc_sc, os_sc, *, tq):
    qt = pl.program_id(2)
    st = (m_sc, l_sc, acc_sc)
    q4 = jnp.concatenate([q_ref[:, r * HEAD_DIM:(r + 1) * HEAD_DIM] for r in range(NSA_REP)], axis=0)
    row = lax.broadcasted_iota(jnp.int32, (tq, tq), 0)
    col = lax.broadcasted_iota(jnp.int32, (tq, tq), 1)
    causal_b = jnp.where(col <= row, 0.0, NEG)
    edge_b = jnp.where(col > row, 0.0, NEG)

    def rep(bias):
        return jnp.concatenate([bias] * NSA_REP, axis=0)

    def tile(ref, kt):
        return ref[pl.ds(pl.multiple_of(kt * tq, tq), tq), :]

    def scores(k_ref, kt):
        return _dot_nt(q4, tile(k_ref, kt)) * ATTN_SCALE

    selm = sel_ref[...]
    per_tile = tq // SEL_BLOCK

    def sel_bias(kt):
        expand = jnp.where(row == kt * per_tile + (col >> SEL_SHIFT), 1.0, 0.0).astype(BF16)
        return jnp.where(_dot(selm, expand) > 0.5, 0.0, NEG)

    _flash_first(scores(ks_ref, qt) + rep(sel_bias(qt) + causal_b), tile(vs_ref, qt), *st)

    def sel_body(kt, carry):
        _flash_next(scores(ks_ref, kt) + rep(sel_bias(kt)), tile(vs_ref, kt), *st)
        return carry

    lax.fori_loop(0, qt, sel_body, 0)
    os_sc[...] = acc_sc[...] / l_sc[...]

    n_back = WINDOW // tq
    _flash_first(scores(kw_ref, qt) + rep(causal_b), tile(vw_ref, qt), *st)

    def win_body(kt, carry):
        _flash_next(scores(kw_ref, kt), tile(vw_ref, kt), *st)
        return carry

    lax.fori_loop(jnp.maximum(qt - (n_back - 1), 0), qt, win_body, 0)

    @pl.when(qt >= n_back)
    def _():
        _flash_next(scores(kw_ref, qt - n_back) + rep(edge_b), tile(vw_ref, qt - n_back), *st)

    gate = jax.nn.sigmoid(gl_ref[...])
    for r in range(NSA_REP):
        sl = slice(r * HEAD_DIM, (r + 1) * HEAD_DIM)
        rows = slice(r * tq, (r + 1) * tq)
        ow = acc_sc[rows, :] / l_sc[rows, :]
        o = (gate[:, 3 * r:3 * r + 1] * oc_ref[:, sl] + gate[:, 3 * r + 1:3 * r + 2] * os_sc[rows, :]
             + gate[:, 3 * r + 2:3 * r + 3] * ow)
        o_ref[:, sl] = o.astype(o_ref.dtype)


def _nsa_attention(q, kk, vv, sel, oc, gl, seq):
    m = q.shape[0]
    b = m // seq
    tq = 128
    per = seq // tq
    width = NSA_REP * HEAD_DIM
    qspec = pl.BlockSpec((tq, width), lambda i, g, j: (i * per + j, g))
    ks = pl.BlockSpec((seq, HEAD_DIM), lambda i, g, j: (i, g))
    kw = pl.BlockSpec((seq, HEAD_DIM), lambda i, g, j: (i, NSA_GROUPS + g))
    rows = NSA_REP * tq
    return pl.pallas_call(
        functools.partial(_nsa_kernel, tq=tq),
        grid=(b, NSA_GROUPS, per),
        in_specs=[qspec, ks, ks, kw, kw,
                  pl.BlockSpec((None, None, tq, LANES), lambda i, g, j: (i, g, j, 0)),
                  qspec,
                  pl.BlockSpec((tq, LANES), lambda i, g, j: (i * per + j, g))],
        out_specs=qspec,
        out_shape=jax.ShapeDtypeStruct((m, NSA_HEADS * HEAD_DIM), BF16),
        scratch_shapes=[pltpu.VMEM((rows, 1), F32), pltpu.VMEM((rows, 1), F32),
                        pltpu.VMEM((rows, HEAD_DIM), F32), pltpu.VMEM((rows, HEAD_DIM), F32)],
        compiler_params=_cparams("arbitrary", "arbitrary", "arbitrary"),
        name="nsa_attn",
    )(q, kk, vv, kk, vv, sel, oc, gl)


def _cum_kernel(f_ref, b_ref, o_ref):
    x = f_ref[...] + b_ref[...]
    ls = jnp.minimum(x, 0.0) - jnp.log1p(jnp.exp(-jnp.abs(x)))
    row = lax.broadcasted_iota(jnp.int32, ls.shape, 0)
    d = 1
    while d < ls.shape[0]:
        ls = ls + jnp.where(row >= d, pltpu.roll(ls, d, 0), 0.0)
        d *= 2
    o_ref[...] = ls


def _cum_forget(fl, bias, seq, col_block):
    m = fl.shape[0]
    return pl.pallas_call(
        _cum_kernel,
        grid=(m // seq,),
        in_specs=[pl.BlockSpec((seq, LANES), lambda i: (i, col_block)),
                  pl.BlockSpec((1, LANES), lambda i: (0, 0))],
        out_specs=pl.BlockSpec((seq, LANES), lambda i: (i, 0)),
        out_shape=jax.ShapeDtypeStruct((m, LANES), F32),
        compiler_params=_cparams("arbitrary"),
        name="cum_forget",
    )(fl, bias)


def _fox_kernel(q_ref, k_ref, v_ref, ck_ref, o_ref, m_sc, l_sc, acc_sc, *, tq):
    qt = pl.program_id(2)
    st = (m_sc, l_sc, acc_sc)
    q = q_ref[...]
    row = lax.broadcasted_iota(jnp.int32, (tq, tq), 0)
    col = lax.broadcasted_iota(jnp.int32, (tq, tq), 1)
    causal_b = jnp.where(col <= row, 0.0, NEG)

    def tile(ref, kt):
        return ref[pl.ds(pl.multiple_of(kt * tq, tq), tq), :]

    def scores(kt):
        return _dot_nt(q, tile(k_ref, kt)) * ATTN_SCALE - ck_ref[kt]

    _flash_first(scores(qt) + causal_b, tile(v_ref, qt), *st)

    def body(kt, carry):
        _flash_next(scores(kt), tile(v_ref, kt), *st)
        return carry

    lax.fori_loop(0, qt, body, 0)
    o_ref[...] = (acc_sc[...] / l_sc[...]).astype(o_ref.dtype)


def _fox_attention(q, k, v, ck, seq):
    m = q.shape[0]
    b = m // seq
    tq = ck.shape[-1]
    per = seq // tq
    kv = pl.BlockSpec((seq, HEAD_DIM), lambda i, h, j: (i, h))
    qo = pl.BlockSpec((tq, HEAD_DIM), lambda i, h, j: (i * per + j, h))
    return pl.pallas_call(
        functools.partial(_fox_kernel, tq=tq),
        grid=(b, FOX_HEADS, per),
        in_specs=[qo, kv, kv,
                  pl.BlockSpec((None, per, 1, tq), lambda i, h, j: (i * FOX_HEADS + h, 0, 0, 0))],
        out_specs=qo,
        out_shape=jax.ShapeDtypeStruct((m, FOX_HEADS * HEAD_DIM), BF16),
        scratch_shapes=[pltpu.VMEM((tq, 1), F32), pltpu.VMEM((tq, 1), F32),
                        pltpu.VMEM((tq, HEAD_DIM), F32)],
        compiler_params=_cparams("arbitrary", "arbitrary", "arbitrary"),
        name="fox_attn",
    )(q, k, v, ck)


def _merge_kernel(a1_ref, w1_ref, a2_ref, w2_ref, g1_ref, g2_ref, o_ref):
    u1 = _dot(a1_ref[...], w1_ref[...])
    u2 = _dot(a2_ref[...], w2_ref[...])
    o_ref[...] = (g1_ref[...].astype(F32) * u1 + g2_ref[...].astype(F32) * u2).astype(o_ref.dtype)


def _merge(a1, w1, a2, w2, gates):
    m, k = a1.shape
    n = w1.shape[1]
    tm, tn = 1024, 512
    nb = n // tn
    aspec = pl.BlockSpec((tm, k), lambda j, i: (i, 0))
    wspec = pl.BlockSpec((k, tn), lambda j, i: (0, j))
    return pl.pallas_call(
        _merge_kernel,
        grid=(nb, m // tm),
        in_specs=[aspec, wspec, aspec, wspec,
                  pl.BlockSpec((tm, tn), lambda j, i: (i, j)),
                  pl.BlockSpec((tm, tn), lambda j, i: (i, nb + j))],
        out_specs=pl.BlockSpec((tm, tn), lambda j, i: (i, j)),
        out_shape=jax.ShapeDtypeStruct((m, n), BF16),
        compiler_params=_cparams("arbitrary", "arbitrary"),
        name="merge",
    )(a1, w1, a2, w2, gates, gates)


def _rope_tables(pos):
    half = ROT_DIM // 2
    inv = ROPE_THETA ** (-jnp.arange(half, dtype=F32) / half)
    ang = jnp.asarray(pos, dtype=F32)[:, None] * inv[None, :]
    cos, sin = jnp.cos(ang), jnp.sin(ang)
    n = ang.shape[0]
    one = jnp.ones((n, HEAD_DIM - ROT_DIM), F32)
    zero = jnp.zeros((n, HEAD_DIM - ROT_DIM), F32)
    zh = jnp.zeros((n, half), F32)
    c = jnp.concatenate([cos, cos, one], axis=1)
    sa = jnp.concatenate([-sin, zh, zero], axis=1)
    sb = jnp.concatenate([zh, sin, zero], axis=1)
    return c, sa, sb


def _layer(x2, c_pad, bsz, seq, w_ada, b_ada, norm1_g, norm2_g, w_in, b_forget, nsa_q_norm, nsa_k_norm,
           fox_q_norm, fox_k_norm, cmp_pos_k, cmp_pos_v, w_cmp_k1, w_cmp_k2, w_cmp_v1, w_cmp_v2,
           w_up_nsa, w_up_fox, w_out, w_ff1, w_ff2):
    d = x2.shape[1]
    hd = HEAD_DIM
    tm = 1024
    per_m = seq // tm

    mod = _ada(c_pad, w_ada, b_ada.reshape(1, -1))[:bsz]
    shift1, scale1, gate1, shift2, scale2, gate2 = [t.reshape(bsz, 1, d) for t in jnp.split(mod, 6, axis=-1)]

    h = _normmod(x2, norm1_g.reshape(1, d), scale1, shift1, seq)

    o_q = 0
    o_kv = o_q + NSA_HEADS * hd
    o_gate = o_kv + 3 * 2 * NSA_GROUPS * hd
    o_fox = o_gate + 3 * NSA_HEADS
    o_f = o_fox + 3 * FOX_HEADS * hd
    o_merge = o_f + FOX_HEADS
    gw = NSA_GROUPS * hd

    def seg(*ranges):
        parts = [w_in[:, a:b] for a, b in ranges]
        return (parts[0] if len(parts) == 1 else jnp.concatenate(parts, axis=1)).astype(BF16)

    def kv_cols(branch, which):
        a = o_kv + (branch * 2 + which) * gw
        return (a, a + gw)

    pos = np.arange(seq)
    c_t, sa_t, sb_t = _rope_tables(pos)
    tab_specs = [pl.BlockSpec((tm, hd), lambda j, i: (i % per_m, 0))] * 3
    gspec = pl.BlockSpec((1, hd), lambda j, i: (0, 0))

    q_n = _matmul(h, seg((o_q, o_kv)), _ep_headnorm_rope, BF16,
                  extras=(nsa_q_norm.reshape(1, hd), c_t, sa_t, sb_t), extra_specs=[gspec] + tab_specs,
                  name="proj_nsa_q")
    k_sw = _matmul(h, seg(kv_cols(1, 0), kv_cols(2, 0)), _ep_headnorm_rope, BF16,
                   extras=(nsa_k_norm.reshape(1, hd), c_t, sa_t, sb_t), extra_specs=[gspec] + tab_specs,
                   name="proj_nsa_k")
    v_sw = _matmul(h, seg(kv_cols(1, 1), kv_cols(2, 1)), _ep_raw, BF16, name="proj_nsa_v")
    kv_c = _matmul(h, seg(kv_cols(0, 0), kv_cols(0, 1)), _ep_raw, F32, name="proj_nsa_cmp")
    fq = _matmul(h, seg((o_fox, o_fox + FOX_HEADS * hd)), _ep_headnorm, BF16,
                 extras=(fox_q_norm.reshape(1, hd),), extra_specs=[gspec], name="proj_fox_q")
    fk = _matmul(h, seg((o_fox + FOX_HEADS * hd, o_fox + 2 * FOX_HEADS * hd)), _ep_headnorm, BF16,
                 extras=(fox_k_norm.reshape(1, hd),), extra_specs=[gspec], name="proj_fox_k")
    fv = _matmul(h, seg((o_fox + 2 * FOX_HEADS * hd, o_f)), _ep_raw, BF16, name="proj_fox_v")
    g_merge = _matmul(h, seg((o_merge, o_merge + 2 * d)), _ep_sigmoid, BF16, name="proj_merge")

    per_group = NSA_REP * 3
    gate_blocks = [jnp.pad(w_in[:, o_gate + g * per_group:o_gate + (g + 1) * per_group],
                           ((0, 0), (0, LANES - per_group))) for g in range(NSA_GROUPS)]
    f_block = jnp.pad(w_in[:, o_f:o_merge], ((0, 0), (0, LANES - FOX_HEADS)))
    w_small = jnp.concatenate(gate_blocks + [f_block], axis=1).astype(BF16)
    small = _matmul(h, w_small, _ep_raw, F32, tn=w_small.shape[1], name="proj_small")

    n_rows = seq // CMP_STRIDE
    def blocks_view(cols):
        t = kv_c[:, cols].reshape(bsz, seq, NSA_GROUPS, hd).transpose(0, 2, 1, 3)
        return t.reshape(bsz, NSA_GROUPS, n_rows, CMP_STRIDE * hd)
    end_pos = np.arange(n_rows) * CMP_STRIDE + CMP_BLOCK - 1
    ce, sae, sbe = _rope_tables(end_pos)
    k_cmp, v_cmp = _compress(
        blocks_view(slice(0, gw)), blocks_view(slice(gw, 2 * gw)),
        cmp_pos_k.reshape(2, CMP_STRIDE * hd), cmp_pos_v.reshape(2, CMP_STRIDE * hd),
        w_cmp_k1.astype(BF16), w_cmp_k2.astype(BF16), w_cmp_v1.astype(BF16), w_cmp_v2.astype(BF16),
        nsa_k_norm.reshape(1, hd), ce, sae, sbe)

    n_slc = seq // SEL_BLOCK
    ci = np.arange(LANES)[:, None] * CMP_STRIDE
    sj = np.arange(LANES)[None, :] * SEL_BLOCK
    overlap = ((ci < sj + SEL_BLOCK) & (ci + CMP_BLOCK > sj) & (np.arange(LANES)[None, :] < n_slc)
               & (np.arange(LANES)[:, None] < n_rows - 1)).astype(np.float32)
    o_c, sel = _cmp_attention(q_n, k_cmp, v_cmp, jnp.asarray(overlap), seq)

    o_nsa = _nsa_attention(q_n, k_sw, v_sw, sel, o_c, small, seq)

    f_bias = jnp.pad(b_forget, (0, LANES - FOX_HEADS)).reshape(1, LANES)
    cum = _cum_forget(small, f_bias, seq, NSA_GROUPS)
    tq_f = 256
    ck = cum[:, :FOX_HEADS].reshape(bsz, seq, FOX_HEADS).transpose(0, 2, 1)
    ck = ck.reshape(bsz * FOX_HEADS, seq // tq_f, 1, tq_f)
    o_fox = _fox_attention(fq, fk, fv, ck, seq)

    y = _merge(o_nsa, w_up_nsa.astype(BF16), o_fox, w_up_fox.astype(BF16), g_merge)
    tn_out = 512
    res_specs = [pl.BlockSpec((tm, tn_out), lambda j, i: (i, j)),
                 pl.BlockSpec((None, 1, tn_out), lambda j, i: (i // per_m, 0, j))]
    x_mid = _matmul(y, w_out.astype(BF16), _ep_residual, F32, extras=(x2, gate1), extra_specs=res_specs,
                    tn=tn_out, name="out_proj")

    h2 = _normmod(x_mid, norm2_g.reshape(1, d), scale2, shift2, seq)
    hid = _matmul(h2, w_ff1.astype(BF16), _ep_relu2, BF16, name="ff1")
    return _matmul_k_residual(hid, w_ff2.astype(BF16), x_mid, gate2, seq, name="ff2")


def kernel(x, c, w_ada, b_ada, norm1_g, norm2_g, w_in, b_forget, nsa_q_norm, nsa_k_norm, fox_q_norm, fox_k_norm, cmp_pos_k, cmp_pos_v, w_cmp_k1, w_cmp_k2, w_cmp_v1, w_cmp_v2, w_up_nsa, w_up_fox, w_out, w_ff1, w_ff2):
    bsz, seq, d = x.shape
    x2 = x.reshape(bsz * seq, d)
    c_pad = jnp.pad(c, ((0, 8 - bsz), (0, 0)))
    params = (w_ada, b_ada, norm1_g, norm2_g, w_in, b_forget, nsa_q_norm, nsa_k_norm, fox_q_norm, fox_k_norm,
              cmp_pos_k, cmp_pos_v, w_cmp_k1, w_cmp_k2, w_cmp_v1, w_cmp_v2, w_up_nsa, w_up_fox, w_out,
              w_ff1, w_ff2)
    for layer in range(w_ada.shape[0]):
        x2 = _layer(x2, c_pad, bsz, seq, *[p[layer] for p in params])
    return x2.reshape(bsz, seq, d)
```

```python
import functools

import numpy as np
import jax
import jax.numpy as jnp
from jax import lax
from jax.experimental import pallas as pl
from jax.experimental.pallas import tpu as pltpu

F32 = jnp.float32
BF16 = jnp.bfloat16

HEAD_DIM = 128
NSA_HEADS = 16
NSA_GROUPS = 4
NSA_REP = NSA_HEADS // NSA_GROUPS
FOX_HEADS = 16
CMP_BLOCK = 32
CMP_STRIDE = 16
SEL_BLOCK = 64
SEL_SHIFT = SEL_BLOCK.bit_length() - 1
SEL_TOPN = 16
WINDOW = 512
ROPE_THETA = 500000.0
ROT_DIM = HEAD_DIM // 4
RMS_EPS = 1e-6
ATTN_SCALE = HEAD_DIM ** -0.5

LOG2E = float(np.log2(np.e))
NEG = -1e30
BIG = 1e30
LANES = 128
VMEM_LIMIT = 56 * 1024 * 1024


def _cparams(*sem):
    return pltpu.CompilerParams(dimension_semantics=sem, vmem_limit_bytes=VMEM_LIMIT)


def _dot(a, b):
    return jnp.dot(a, b, preferred_element_type=F32)


def _dot_nt(a, b):
    return lax.dot_general(a, b, (((1,), (1,)), ((), ())), preferred_element_type=F32)


def _ada_kernel(c_ref, w_ref, b_ref, o_ref):
    c = c_ref[...]
    s = (c * jax.nn.sigmoid(c)).astype(BF16)
    o_ref[...] = _dot(s, w_ref[...].astype(BF16)) + b_ref[...]


def _ada(c_pad, w, b):
    rows, d = c_pad.shape
    n = w.shape[1]
    tn = 1024
    return pl.pallas_call(
        _ada_kernel,
        grid=(n // tn,),
        in_specs=[pl.BlockSpec((rows, d), lambda j: (0, 0)),
                  pl.BlockSpec((d, tn), lambda j: (0, j)),
                  pl.BlockSpec((1, tn), lambda j: (0, j))],
        out_specs=pl.BlockSpec((rows, tn), lambda j: (0, j)),
        out_shape=jax.ShapeDtypeStruct((rows, n), F32),
        compiler_params=_cparams("arbitrary"),
        name="ada",
    )(c_pad, w, b)


def _normmod_kernel(x_ref, g_ref, sc_ref, sh_ref, o_ref):
    x = x_ref[...]
    ms = jnp.mean(x * x, axis=-1, keepdims=True)
    y = x * lax.rsqrt(ms + RMS_EPS) * g_ref[...]
    o_ref[...] = (y * (1.0 + sc_ref[...]) + sh_ref[...]).astype(o_ref.dtype)


def _normmod(x2, g, scale, shift, seq):
    m, d = x2.shape
    tr = 256
    per = seq // tr
    return pl.pallas_call(
        _normmod_kernel,
        grid=(m // tr,),
        in_specs=[pl.BlockSpec((tr, d), lambda i: (i, 0)),
                  pl.BlockSpec((1, d), lambda i: (0, 0)),
                  pl.BlockSpec((None, 1, d), lambda i: (i // per, 0, 0)),
                  pl.BlockSpec((None, 1, d), lambda i: (i // per, 0, 0))],
        out_specs=pl.BlockSpec((tr, d), lambda i: (i, 0)),
        out_shape=jax.ShapeDtypeStruct((m, d), BF16),
        compiler_params=_cparams("arbitrary"),
        name="normmod",
    )(x2, g, scale, shift)


def _ep_raw(acc, o_ref):
    o_ref[...] = acc.astype(o_ref.dtype)


def _ep_sigmoid(acc, o_ref):
    o_ref[...] = jax.nn.sigmoid(acc).astype(o_ref.dtype)


def _ep_relu2(acc, o_ref):
    r = jnp.maximum(acc, 0.0)
    o_ref[...] = (r * r).astype(o_ref.dtype)


def _ep_residual(acc, o_ref, x_ref, g_ref):
    o_ref[...] = x_ref[...] + g_ref[...] * acc


def _head_norm(a, g):
    ms = jnp.mean(a * a, axis=-1, keepdims=True)
    return a * lax.rsqrt(ms + RMS_EPS) * g


def _rope(a, c, sa, sb):
    return a * c + pltpu.roll(a, LANES - ROT_DIM // 2, 1) * sa + pltpu.roll(a, ROT_DIM // 2, 1) * sb


def _ep_headnorm(acc, o_ref, g_ref):
    g = g_ref[...]
    for h in range(acc.shape[1] // HEAD_DIM):
        sl = slice(h * HEAD_DIM, (h + 1) * HEAD_DIM)
        o_ref[:, sl] = _head_norm(acc[:, sl], g).astype(o_ref.dtype)


def _ep_headnorm_rope(acc, o_ref, g_ref, c_ref, sa_ref, sb_ref):
    g = g_ref[...]
    c, sa, sb = c_ref[...], sa_ref[...], sb_ref[...]
    for h in range(acc.shape[1] // HEAD_DIM):
        sl = slice(h * HEAD_DIM, (h + 1) * HEAD_DIM)
        o_ref[:, sl] = _rope(_head_norm(acc[:, sl], g), c, sa, sb).astype(o_ref.dtype)


def _mm_kernel(a_ref, b_ref, *rest, epilogue, n_extra):
    extras = rest[:n_extra]
    o_ref = rest[n_extra]
    epilogue(_dot(a_ref[...], b_ref[...]), o_ref, *extras)


def _matmul(a, b, epilogue, out_dtype, extras=(), extra_specs=(), tm=1024, tn=1024, name="mm"):
    m, k = a.shape
    n = b.shape[1]
    tn = min(tn, n)
    return pl.pallas_call(
        functools.partial(_mm_kernel, epilogue=epilogue, n_extra=len(extras)),
        grid=(n // tn, m // tm),
        in_specs=[pl.BlockSpec((tm, k), lambda j, i: (i, 0)),
                  pl.BlockSpec((k, tn), lambda j, i: (0, j))] + list(extra_specs),
        out_specs=pl.BlockSpec((tm, tn), lambda j, i: (i, j)),
        out_shape=jax.ShapeDtypeStruct((m, n), out_dtype),
        compiler_params=_cparams("arbitrary", "arbitrary"),
        name=name,
    )(a, b, *extras)


def _mmk_kernel(a_ref, b_ref, x_ref, g_ref, o_ref, acc_ref, *, nk):
    kk = pl.program_id(2)

    @pl.when(kk == 0)
    def _():
        acc_ref[...] = jnp.zeros_like(acc_ref)

    acc_ref[...] += _dot(a_ref[...], b_ref[...])

    @pl.when(kk == nk - 1)
    def _():
        o_ref[...] = x_ref[...] + g_ref[...] * acc_ref[...]


def _matmul_k_residual(a, b, x, gate, seq, tm=1024, tn=1024, tk=2048, name="mmk"):
    m, k = a.shape
    n = b.shape[1]
    per = seq // tm
    nk = k // tk
    return pl.pallas_call(
        functools.partial(_mmk_kernel, nk=nk),
        grid=(m // tm, n // tn, nk),
        in_specs=[pl.BlockSpec((tm, tk), lambda i, j, kk: (i, kk)),
                  pl.BlockSpec((tk, tn), lambda i, j, kk: (kk, j)),
                  pl.BlockSpec((tm, tn), lambda i, j, kk: (i, j)),
                  pl.BlockSpec((None, 1, tn), lambda i, j, kk: (i // per, 0, j))],
        out_specs=pl.BlockSpec((tm, tn), lambda i, j, kk: (i, j)),
        out_shape=jax.ShapeDtypeStruct((m, n), F32),
        scratch_shapes=[pltpu.VMEM((tm, tn), F32)],
        compiler_params=_cparams("arbitrary", "arbitrary", "arbitrary"),
        name=name,
    )(a, b, x, gate)


def _compress_kernel(xk_ref, xv_ref, pek_ref, pev_ref, w1k_ref, w2k_ref, w1v_ref, w2v_ref,
                     g_ref, c_ref, sa_ref, sb_ref, ko_ref, vo_ref):
    half = CMP_STRIDE * HEAD_DIM

    def comp(x_ref, pe_ref, w1_ref, w2_ref):
        x = x_ref[...]
        lo = (x + pe_ref[0:1, :]).astype(BF16)
        hi = (x + pe_ref[1:2, :]).astype(BF16)
        p = _dot(lo, w1_ref[0:half, :])
        q = _dot(hi, w1_ref[half:2 * half, :])
        h = p + pltpu.roll(q, q.shape[0] - 1, 0)
        h = jax.nn.gelu(h, approximate=True)
        return _dot(h.astype(BF16), w2_ref[...])

    kc = comp(xk_ref, pek_ref, w1k_ref, w2k_ref)
    kc = _rope(_head_norm(kc, g_ref[...]), c_ref[...], sa_ref[...], sb_ref[...])
    ko_ref[...] = kc.astype(ko_ref.dtype)
    vo_ref[...] = comp(xv_ref, pev_ref, w1v_ref, w2v_ref).astype(vo_ref.dtype)


def _compress(xk, xv, pek, pev, w1k, w2k, w1v, w2v, g, c, sa, sb):
    b, gr, nr, wd = xk.shape
    xspec = pl.BlockSpec((None, None, nr, wd), lambda i, j: (i, j, 0, 0))
    full = lambda arr: pl.BlockSpec(arr.shape, lambda i, j: (0,) * arr.ndim)
    ospec = pl.BlockSpec((None, None, nr, HEAD_DIM), lambda i, j: (i, j, 0, 0))
    oshape = jax.ShapeDtypeStruct((b, gr, nr, HEAD_DIM), BF16)
    return pl.pallas_call(
        _compress_kernel,
        grid=(b, gr),
        in_specs=[xspec, xspec, full(pek), full(pev), full(w1k), full(w2k), full(w1v), full(w2v),
                  full(g), full(c), full(sa), full(sb)],
        out_specs=[ospec, ospec],
        out_shape=[oshape, oshape],
        compiler_params=_cparams("arbitrary", "arbitrary"),
        name="compress",
    )(xk, xv, pek, pev, w1k, w2k, w1v, w2v, g, c, sa, sb)


def _cmp_kernel(q_ref, k_ref, v_ref, ov_ref, oc_ref, sel_ref, *, tt, n_cmp, n_slc):
    t = pl.program_id(2) * tt + lax.broadcasted_iota(jnp.int32, (tt, LANES), 0)
    n = lax.broadcasted_iota(jnp.int32, (tt, LANES), 1)
    valid = (n < n_cmp) & (n * CMP_STRIDE + (CMP_BLOCK - 1) <= t)
    k = k_ref[...]
    v = v_ref[...]
    psum = jnp.zeros((tt, LANES), F32)
    for r in range(NSA_REP):
        sl = slice(r * HEAD_DIM, (r + 1) * HEAD_DIM)
        s = jnp.where(valid, _dot_nt(q_ref[:, sl], k) * ATTN_SCALE, NEG)
        m = jnp.max(s, axis=-1, keepdims=True)
        p = jnp.where(valid, jnp.exp(s - m), 0.0)
        d = jnp.sum(p, axis=-1, keepdims=True)
        p = p / jnp.where(d > 0, d, 1.0)
        psum = psum + p
        oc_ref[:, sl] = _dot(p.astype(BF16), v)
    imp = jnp.dot(psum, ov_ref[...], preferred_element_type=F32, precision=lax.Precision.HIGHEST)
    cur = t >> SEL_SHIFT
    forced = (n == 0) | (n == cur) | (n == cur - 1)
    score = jnp.where(forced, BIG, jnp.where(n <= cur, imp, -BIG))
    cnt = jnp.zeros((tt, LANES), F32)
    for jp in range(n_slc):
        col = score[:, jp:jp + 1]
        beats = (col > score) | ((col == score) & (jp < n))
        cnt = cnt + jnp.where(beats, 1.0, 0.0)
    sel = (cnt < SEL_TOPN) & (score > -0.5 * BIG)
    sel_ref[...] = jnp.where(sel, 1.0, 0.0).astype(sel_ref.dtype)


def _cmp_attention(q, kc, vc, overlap, seq):
    m, _ = q.shape
    b = m // seq
    tt = 512
    per = seq // tt
    n_cmp = (seq - CMP_BLOCK) // CMP_STRIDE + 1
    kspec = pl.BlockSpec((None, None, LANES, HEAD_DIM), lambda i, g, j: (i, g, 0, 0))
    return pl.pallas_call(
        functools.partial(_cmp_kernel, tt=tt, n_cmp=n_cmp, n_slc=seq // SEL_BLOCK),
        grid=(b, NSA_GROUPS, per),
        in_specs=[pl.BlockSpec((tt, NSA_REP * HEAD_DIM), lambda i, g, j: (i * per + j, g)),
                  kspec, kspec,
                  pl.BlockSpec((LANES, LANES), lambda i, g, j: (0, 0))],
        out_specs=[pl.BlockSpec((tt, NSA_REP * HEAD_DIM), lambda i, g, j: (i * per + j, g)),
                   pl.BlockSpec((None, None, tt, LANES), lambda i, g, j: (i, g, j, 0))],
        out_shape=[jax.ShapeDtypeStruct((m, NSA_HEADS * HEAD_DIM), F32),
                   jax.ShapeDtypeStruct((b, NSA_GROUPS, seq, LANES), BF16)],
        compiler_params=_cparams("arbitrary", "arbitrary", "arbitrary"),
        name="cmp_attn",
    )(q, kc, vc, overlap)


def _softmax_pv(t, v):
    m = jnp.max(t, axis=-1, keepdims=True)
    p = jnp.exp2(t - m)
    l = jnp.sum(p, axis=-1, keepdims=True)
    return _dot(p.astype(BF16), v) / l


def _tri_bias(n, lower):
    row = lax.broadcasted_iota(jnp.int32, (n, n), 0)
    col = lax.broadcasted_iota(jnp.int32, (n, n), 1)
    return jnp.where(col <= row, 0.0, NEG) if lower else jnp.where(col > row, 0.0, NEG)


def _nsa_kernel(q_ref, ks_ref, vs_ref, kw_ref, vw_ref, sel_ref, ex_ref, oc_ref, gl_ref, o_ref,
                t_sc, p_sc, l_sc, bs_sc, bw_sc, os_sc, *, tq):
    seq = q_ref.shape[0]
    live = pl.program_id(0) >= 0
    n_back = WINDOW // tq
    w_cols = (n_back + 1) * tq
    bw_sc[:, 0:tq] = _tri_bias(tq, False)
    bw_sc[:, tq:n_back * tq] = jnp.zeros((tq, (n_back - 1) * tq), F32)
    bw_sc[:, n_back * tq:w_cols] = _tri_bias(tq, True)

    jobs = []
    for qt in range(seq // tq):
        lo, hi = qt * tq, (qt + 1) * tq
        jobs.append(("sel", lo, hi, 0))
        jobs.append(("win", lo, hi, max(lo - WINDOW, 0)))

    def head_rows(r):
        return slice(r * tq, (r + 1) * tq)

    def stage_scores(j):
        kind, lo, hi, klo = jobs[j]
        slot = j % 2
        n = hi - klo
        q4 = jnp.concatenate([q_ref[lo:hi, r * HEAD_DIM:(r + 1) * HEAD_DIM] for r in range(NSA_REP)], axis=0)
        k_ref = ks_ref if kind == "sel" else kw_ref
        t_sc[slot, :, 0:n] = _dot_nt(q4, k_ref[klo:hi, :]) * (ATTN_SCALE * LOG2E)
        if kind == "sel":
            picked = _dot(sel_ref[lo:hi, :], ex_ref[:, 0:hi])
            bias = jnp.where(picked > 0.5, 0.0, NEG)
            if lo > 0:
                bs_sc[slot, :, 0:lo] = bias[:, 0:lo]
            bs_sc[slot, :, lo:hi] = bias[:, lo:hi] + _tri_bias(tq, True)

    def stage_softmax(j):
        kind, lo, hi, klo = jobs[j]
        slot = j % 2
        n = hi - klo
        for r in range(NSA_REP):
            bias = bs_sc[slot, :, 0:n] if kind == "sel" else bw_sc[:, w_cols - n:w_cols]
            t = t_sc[slot, head_rows(r), 0:n] + bias
            p = jnp.exp2(t - jnp.max(t, axis=-1, keepdims=True))
            l_sc[slot, head_rows(r), :] = jnp.sum(p, axis=-1, keepdims=True)
            p_sc[slot, head_rows(r), 0:n] = p.astype(BF16)

    def stage_output(j):
        kind, lo, hi, klo = jobs[j]
        slot = j % 2
        n = hi - klo
        v_ref = vs_ref if kind == "sel" else vw_ref
        o4 = _dot(p_sc[slot, :, 0:n], v_ref[klo:hi, :]) / l_sc[slot]
        if kind == "sel":
            os_sc[...] = o4
            return
        gate = jax.nn.sigmoid(gl_ref[lo:hi, :])
        for r in range(NSA_REP):
            sl = slice(r * HEAD_DIM, (r + 1) * HEAD_DIM)
            o = (gate[:, 3 * r:3 * r + 1] * oc_ref[lo:hi, sl]
                 + gate[:, 3 * r + 1:3 * r + 2] * os_sc[head_rows(r), :]
                 + gate[:, 3 * r + 2:3 * r + 3] * o4[head_rows(r), :])
            o_ref[lo:hi, sl] = o.astype(o_ref.dtype)

    for i in range(len(jobs) + 2):
        @pl.when(live)
        def _region():
            if i < len(jobs):
                stage_scores(i)
            if 0 <= i - 1 < len(jobs):
                stage_softmax(i - 1)
            if 0 <= i - 2 < len(jobs):
                stage_output(i - 2)


def _nsa_attention(q, kk, vv, sel, expand, oc, gl, seq):
    m = q.shape[0]
    b = m // seq
    width = NSA_REP * HEAD_DIM
    qspec = pl.BlockSpec((seq, width), lambda i, g: (i, g))
    ks = pl.BlockSpec((seq, HEAD_DIM), lambda i, g: (i, g))
    kw = pl.BlockSpec((seq, HEAD_DIM), lambda i, g: (i, NSA_GROUPS + g))
    tq = 128
    rows = NSA_REP * tq
    return pl.pallas_call(
        functools.partial(_nsa_kernel, tq=tq),
        grid=(b, NSA_GROUPS),
        scratch_shapes=[pltpu.VMEM((2, rows, seq), F32), pltpu.VMEM((2, rows, seq), BF16),
                        pltpu.VMEM((2, rows, 1), F32), pltpu.VMEM((2, tq, seq), F32),
                        pltpu.VMEM((tq, WINDOW + tq), F32), pltpu.VMEM((rows, HEAD_DIM), F32)],
        in_specs=[qspec, ks, ks, kw, kw,
                  pl.BlockSpec((None, None, seq, LANES), lambda i, g: (i, g, 0, 0)),
                  pl.BlockSpec((LANES, seq), lambda i, g: (0, 0)),
                  qspec,
                  pl.BlockSpec((seq, LANES), lambda i, g: (i, g))],
        out_specs=qspec,
        out_shape=jax.ShapeDtypeStruct((m, NSA_HEADS * HEAD_DIM), BF16),
        compiler_params=_cparams("arbitrary", "arbitrary"),
        name="nsa_attn",
    )(q, kk, vv, kk, vv, sel, expand, oc, gl)


def _cum_kernel(f_ref, b_ref, o_ref):
    x = f_ref[...] + b_ref[...]
    ls = jnp.minimum(x, 0.0) - jnp.log1p(jnp.exp(-jnp.abs(x)))
    row = lax.broadcasted_iota(jnp.int32, ls.shape, 0)
    d = 1
    while d < ls.shape[0]:
        ls = ls + jnp.where(row >= d, pltpu.roll(ls, d, 0), 0.0)
        d *= 2
    o_ref[...] = ls


def _cum_forget(fl, bias, seq, col_block):
    m = fl.shape[0]
    return pl.pallas_call(
        _cum_kernel,
        grid=(m // seq,),
        in_specs=[pl.BlockSpec((seq, LANES), lambda i: (i, col_block)),
                  pl.BlockSpec((1, LANES), lambda i: (0, 0))],
        out_specs=pl.BlockSpec((seq, LANES), lambda i: (i, 0)),
        out_shape=jax.ShapeDtypeStruct((m, LANES), F32),
        compiler_params=_cparams("arbitrary"),
        name="cum_forget",
    )(fl, bias)


def _fox_kernel(q_ref, k_ref, v_ref, ck_ref, o_ref, *, tq):
    seq = q_ref.shape[0]
    causal_b = _tri_bias(tq, True)
    ck = ck_ref[...] * LOG2E
    for qt in range(seq // tq):
        lo, hi = qt * tq, (qt + 1) * tq
        t = _dot_nt(q_ref[lo:hi, :], k_ref[0:hi, :]) * (ATTN_SCALE * LOG2E) - ck[:, 0:hi]
        diag = t[:, lo:hi] + causal_b
        t = diag if qt == 0 else jnp.concatenate([t[:, 0:lo], diag], axis=1)
        o_ref[lo:hi, :] = _softmax_pv(t, v_ref[0:hi, :]).astype(o_ref.dtype)


def _fox_attention(q, k, v, ck, seq):
    m = q.shape[0]
    b = m // seq
    blk = pl.BlockSpec((seq, HEAD_DIM), lambda i, h: (i, h))
    return pl.pallas_call(
        functools.partial(_fox_kernel, tq=256),
        grid=(b, FOX_HEADS),
        in_specs=[blk, blk, blk,
                  pl.BlockSpec((None, 1, seq), lambda i, h: (i * FOX_HEADS + h, 0, 0))],
        out_specs=blk,
        out_shape=jax.ShapeDtypeStruct((m, FOX_HEADS * HEAD_DIM), BF16),
        compiler_params=_cparams("arbitrary", "arbitrary"),
        name="fox_attn",
    )(q, k, v, ck)


def _merge_kernel(a1_ref, w1_ref, a2_ref, w2_ref, g1_ref, g2_ref, o_ref):
    u1 = _dot(a1_ref[...], w1_ref[...])
    u2 = _dot(a2_ref[...], w2_ref[...])
    o_ref[...] = (g1_ref[...].astype(F32) * u1 + g2_ref[...].astype(F32) * u2).astype(o_ref.dtype)


def _merge(a1, w1, a2, w2, gates):
    m, k = a1.shape
    n = w1.shape[1]
    tm, tn = 1024, 512
    nb = n // tn
    aspec = pl.BlockSpec((tm, k), lambda j, i: (i, 0))
    wspec = pl.BlockSpec((k, tn), lambda j, i: (0, j))
    return pl.pallas_call(
        _merge_kernel,
        grid=(nb, m // tm),
        in_specs=[aspec, wspec, aspec, wspec,
                  pl.BlockSpec((tm, tn), lambda j, i: (i, j)),
                  pl.BlockSpec((tm, tn), lambda j, i: (i, nb + j))],
        out_specs=pl.BlockSpec((tm, tn), lambda j, i: (i, j)),
        out_shape=jax.ShapeDtypeStruct((m, n), BF16),
        compiler_params=_cparams("arbitrary", "arbitrary"),
        name="merge",
    )(a1, w1, a2, w2, gates, gates)


def _rope_tables(pos):
    half = ROT_DIM // 2
    inv = ROPE_THETA ** (-jnp.arange(half, dtype=F32) / half)
    ang = jnp.asarray(pos, dtype=F32)[:, None] * inv[None, :]
    cos, sin = jnp.cos(ang), jnp.sin(ang)
    n = ang.shape[0]
    one = jnp.ones((n, HEAD_DIM - ROT_DIM), F32)
    zero = jnp.zeros((n, HEAD_DIM - ROT_DIM), F32)
    zh = jnp.zeros((n, half), F32)
    c = jnp.concatenate([cos, cos, one], axis=1)
    sa = jnp.concatenate([-sin, zh, zero], axis=1)
    sb = jnp.concatenate([zh, sin, zero], axis=1)
    return c, sa, sb


def _layer(x2, c_pad, bsz, seq, w_ada, b_ada, norm1_g, norm2_g, w_in, b_forget, nsa_q_norm, nsa_k_norm,
           fox_q_norm, fox_k_norm, cmp_pos_k, cmp_pos_v, w_cmp_k1, w_cmp_k2, w_cmp_v1, w_cmp_v2,
           w_up_nsa, w_up_fox, w_out, w_ff1, w_ff2):
    d = x2.shape[1]
    hd = HEAD_DIM
    tm = 1024
    per_m = seq // tm

    mod = _ada(c_pad, w_ada, b_ada.reshape(1, -1))[:bsz]
    shift1, scale1, gate1, shift2, scale2, gate2 = [t.reshape(bsz, 1, d) for t in jnp.split(mod, 6, axis=-1)]

    h = _normmod(x2, norm1_g.reshape(1, d), scale1, shift1, seq)

    o_q = 0
    o_kv = o_q + NSA_HEADS * hd
    o_gate = o_kv + 3 * 2 * NSA_GROUPS * hd
    o_fox = o_gate + 3 * NSA_HEADS
    o_f = o_fox + 3 * FOX_HEADS * hd
    o_merge = o_f + FOX_HEADS
    gw = NSA_GROUPS * hd

    def seg(*ranges):
        parts = [w_in[:, a:b] for a, b in ranges]
        return (parts[0] if len(parts) == 1 else jnp.concatenate(parts, axis=1)).astype(BF16)

    def kv_cols(branch, which):
        a = o_kv + (branch * 2 + which) * gw
        return (a, a + gw)

    pos = np.arange(seq)
    c_t, sa_t, sb_t = _rope_tables(pos)
    tab_specs = [pl.BlockSpec((tm, hd), lambda j, i: (i % per_m, 0))] * 3
    gspec = pl.BlockSpec((1, hd), lambda j, i: (0, 0))

    q_n = _matmul(h, seg((o_q, o_kv)), _ep_headnorm_rope, BF16,
                  extras=(nsa_q_norm.reshape(1, hd), c_t, sa_t, sb_t), extra_specs=[gspec] + tab_specs,
                  name="proj_nsa_q")
    k_sw = _matmul(h, seg(kv_cols(1, 0), kv_cols(2, 0)), _ep_headnorm_rope, BF16,
                   extras=(nsa_k_norm.reshape(1, hd), c_t, sa_t, sb_t), extra_specs=[gspec] + tab_specs,
                   name="proj_nsa_k")
    v_sw = _matmul(h, seg(kv_cols(1, 1), kv_cols(2, 1)), _ep_raw, BF16, name="proj_nsa_v")
    kv_c = _matmul(h, seg(kv_cols(0, 0), kv_cols(0, 1)), _ep_raw, F32, name="proj_nsa_cmp")
    fq = _matmul(h, seg((o_fox, o_fox + FOX_HEADS * hd)), _ep_headnorm, BF16,
                 extras=(fox_q_norm.reshape(1, hd),), extra_specs=[gspec], name="proj_fox_q")
    fk = _matmul(h, seg((o_fox + FOX_HEADS * hd, o_fox + 2 * FOX_HEADS * hd)), _ep_headnorm, BF16,
                 extras=(fox_k_norm.reshape(1, hd),), extra_specs=[gspec], name="proj_fox_k")
    fv = _matmul(h, seg((o_fox + 2 * FOX_HEADS * hd, o_f)), _ep_raw, BF16, name="proj_fox_v")
    g_merge = _matmul(h, seg((o_merge, o_merge + 2 * d)), _ep_sigmoid, BF16, name="proj_merge")

    per_group = NSA_REP * 3
    gate_blocks = [jnp.pad(w_in[:, o_gate + g * per_group:o_gate + (g + 1) * per_group],
                           ((0, 0), (0, LANES - per_group))) for g in range(NSA_GROUPS)]
    f_block = jnp.pad(w_in[:, o_f:o_merge], ((0, 0), (0, LANES - FOX_HEADS)))
    w_small = jnp.concatenate(gate_blocks + [f_block], axis=1).astype(BF16)
    small = _matmul(h, w_small, _ep_raw, F32, tn=w_small.shape[1], name="proj_small")

    n_rows = seq // CMP_STRIDE
    def blocks_view(cols):
        t = kv_c[:, cols].reshape(bsz, seq, NSA_GROUPS, hd).transpose(0, 2, 1, 3)
        return t.reshape(bsz, NSA_GROUPS, n_rows, CMP_STRIDE * hd)
    end_pos = np.arange(n_rows) * CMP_STRIDE + CMP_BLOCK - 1
    ce, sae, sbe = _rope_tables(end_pos)
    k_cmp, v_cmp = _compress(
        blocks_view(slice(0, gw)), blocks_view(slice(gw, 2 * gw)),
        cmp_pos_k.reshape(2, CMP_STRIDE * hd), cmp_pos_v.reshape(2, CMP_STRIDE * hd),
        w_cmp_k1.astype(BF16), w_cmp_k2.astype(BF16), w_cmp_v1.astype(BF16), w_cmp_v2.astype(BF16),
        nsa_k_norm.reshape(1, hd), ce, sae, sbe)

    n_slc = seq // SEL_BLOCK
    ci = np.arange(LANES)[:, None] * CMP_STRIDE
    sj = np.arange(LANES)[None, :] * SEL_BLOCK
    overlap = ((ci < sj + SEL_BLOCK) & (ci + CMP_BLOCK > sj) & (np.arange(LANES)[None, :] < n_slc)
               & (np.arange(LANES)[:, None] < n_rows - 1)).astype(np.float32)
    o_c, sel = _cmp_attention(q_n, k_cmp, v_cmp, jnp.asarray(overlap), seq)

    expand = (np.arange(LANES)[:, None] == (np.arange(seq)[None, :] >> SEL_SHIFT)).astype(np.float32)
    o_nsa = _nsa_attention(q_n, k_sw, v_sw, sel, jnp.asarray(expand, dtype=BF16), o_c, small, seq)

    f_bias = jnp.pad(b_forget, (0, LANES - FOX_HEADS)).reshape(1, LANES)
    cum = _cum_forget(small, f_bias, seq, NSA_GROUPS)
    ck = cum[:, :FOX_HEADS].reshape(bsz, seq, FOX_HEADS).transpose(0, 2, 1).reshape(bsz * FOX_HEADS, 1, seq)
    o_fox = _fox_attention(fq, fk, fv, ck, seq)

    y = _merge(o_nsa, w_up_nsa.astype(BF16), o_fox, w_up_fox.astype(BF16), g_merge)
    tn_out = 512
    res_specs = [pl.BlockSpec((tm, tn_out), lambda j, i: (i, j)),
                 pl.BlockSpec((None, 1, tn_out), lambda j, i: (i // per_m, 0, j))]
    x_mid = _matmul(y, w_out.astype(BF16), _ep_residual, F32, extras=(x2, gate1), extra_specs=res_specs,
                    tn=tn_out, name="out_proj")

    h2 = _normmod(x_mid, norm2_g.reshape(1, d), scale2, shift2, seq)
    hid = _matmul(h2, w_ff1.astype(BF16), _ep_relu2, BF16, name="ff1")
    return _matmul_k_residual(hid, w_ff2.astype(BF16), x_mid, gate2, seq, name="ff2")


def kernel(x, c, w_ada, b_ada, norm1_g, norm2_g, w_in, b_forget, nsa_q_norm, nsa_k_norm, fox_q_norm, fox_k_norm, cmp_pos_k, cmp_pos_v, w_cmp_k1, w_cmp_k2, w_cmp_v1, w_cmp_v2, w_up_nsa, w_up_fox, w_out, w_ff1, w_ff2):
    bsz, seq, d = x.shape
    x2 = x.reshape(bsz * seq, d)
    c_pad = jnp.pad(c, ((0, 8 - bsz), (0, 0)))
    params = (w_ada, b_ada, norm1_g, norm2_g, w_in, b_forget, nsa_q_norm, nsa_k_norm, fox_q_norm, fox_k_norm,
              cmp_pos_k, cmp_pos_v, w_cmp_k1, w_cmp_k2, w_cmp_v1, w_cmp_v2, w_up_nsa, w_up_fox, w_out,
              w_ff1, w_ff2)
    for layer in range(w_ada.shape[0]):
        x2 = _layer(x2, c_pad, bsz, seq, *[p[layer] for p in params])
    return x2.reshape(bsz, seq, d)
```

```python
import functools

import numpy as np
import jax
import jax.numpy as jnp
from jax import lax
from jax.experimental import pallas as pl
from jax.experimental.pallas import tpu as pltpu

F32 = jnp.float32
BF16 = jnp.bfloat16

HEAD_DIM = 128
NSA_HEADS = 16
NSA_GROUPS = 4
NSA_REP = NSA_HEADS // NSA_GROUPS
FOX_HEADS = 16
CMP_BLOCK = 32
CMP_STRIDE = 16
SEL_BLOCK = 64
SEL_SHIFT = SEL_BLOCK.bit_length() - 1
SEL_TOPN = 16
WINDOW = 512
ROPE_THETA = 500000.0
ROT_DIM = HEAD_DIM // 4
RMS_EPS = 1e-6
ATTN_SCALE = HEAD_DIM ** -0.5

LOG2E = float(np.log2(np.e))
NEG = -1e30
BIG = 1e30
LANES = 128
VMEM_LIMIT = 56 * 1024 * 1024


def _cparams(*sem):
    return pltpu.CompilerParams(dimension_semantics=sem, vmem_limit_bytes=VMEM_LIMIT)


def _dot(a, b):
    return jnp.dot(a, b, preferred_element_type=F32)


def _dot_nt(a, b):
    return lax.dot_general(a, b, (((1,), (1,)), ((), ())), preferred_element_type=F32)


def _ada_kernel(c_ref, w_ref, b_ref, o_ref):
    c = c_ref[...]
    s = (c * jax.nn.sigmoid(c)).astype(BF16)
    o_ref[...] = _dot(s, w_ref[...].astype(BF16)) + b_ref[...]


def _ada(c_pad, w, b):
    rows, d = c_pad.shape
    n = w.shape[1]
    tn = 1024
    return pl.pallas_call(
        _ada_kernel,
        grid=(n // tn,),
        in_specs=[pl.BlockSpec((rows, d), lambda j: (0, 0)),
                  pl.BlockSpec((d, tn), lambda j: (0, j)),
                  pl.BlockSpec((1, tn), lambda j: (0, j))],
        out_specs=pl.BlockSpec((rows, tn), lambda j: (0, j)),
        out_shape=jax.ShapeDtypeStruct((rows, n), F32),
        compiler_params=_cparams("arbitrary"),
        name="ada",
    )(c_pad, w, b)


def _normmod_kernel(x_ref, g_ref, sc_ref, sh_ref, o_ref):
    x = x_ref[...]
    ms = jnp.mean(x * x, axis=-1, keepdims=True)
    y = x * lax.rsqrt(ms + RMS_EPS) * g_ref[...]
    o_ref[...] = (y * (1.0 + sc_ref[...]) + sh_ref[...]).astype(o_ref.dtype)


def _normmod(x2, g, scale, shift, seq):
    m, d = x2.shape
    tr = 256
    per = seq // tr
    return pl.pallas_call(
        _normmod_kernel,
        grid=(m // tr,),
        in_specs=[pl.BlockSpec((tr, d), lambda i: (i, 0)),
                  pl.BlockSpec((1, d), lambda i: (0, 0)),
                  pl.BlockSpec((None, 1, d), lambda i: (i // per, 0, 0)),
                  pl.BlockSpec((None, 1, d), lambda i: (i // per, 0, 0))],
        out_specs=pl.BlockSpec((tr, d), lambda i: (i, 0)),
        out_shape=jax.ShapeDtypeStruct((m, d), BF16),
        compiler_params=_cparams("arbitrary"),
        name="normmod",
    )(x2, g, scale, shift)


def _ep_raw(acc, o_ref):
    o_ref[...] = acc.astype(o_ref.dtype)


def _ep_sigmoid(acc, o_ref):
    o_ref[...] = jax.nn.sigmoid(acc).astype(o_ref.dtype)


def _ep_relu2(acc, o_ref):
    r = jnp.maximum(acc, 0.0)
    o_ref[...] = (r * r).astype(o_ref.dtype)


def _ep_residual(acc, o_ref, x_ref, g_ref):
    o_ref[...] = x_ref[...] + g_ref[...] * acc


def _head_norm(a, g):
    ms = jnp.mean(a * a, axis=-1, keepdims=True)
    return a * lax.rsqrt(ms + RMS_EPS) * g


def _rope(a, c, sa, sb):
    return a * c + pltpu.roll(a, LANES - ROT_DIM // 2, 1) * sa + pltpu.roll(a, ROT_DIM // 2, 1) * sb


def _ep_headnorm(acc, o_ref, g_ref):
    g = g_ref[...]
    for h in range(acc.shape[1] // HEAD_DIM):
        sl = slice(h * HEAD_DIM, (h + 1) * HEAD_DIM)
        o_ref[:, sl] = _head_norm(acc[:, sl], g).astype(o_ref.dtype)


def _ep_headnorm_rope(acc, o_ref, g_ref, c_ref, sa_ref, sb_ref):
    g = g_ref[...]
    c, sa, sb = c_ref[...], sa_ref[...], sb_ref[...]
    for h in range(acc.shape[1] // HEAD_DIM):
        sl = slice(h * HEAD_DIM, (h + 1) * HEAD_DIM)
        o_ref[:, sl] = _rope(_head_norm(acc[:, sl], g), c, sa, sb).astype(o_ref.dtype)


def _cast_blocks(refs_in, refs_out):
    for src, dst in zip(refs_in, refs_out):
        dst[...] = src[...].astype(dst.dtype)


def _cast_specs(casts, steps, index_of_step):
    specs, shapes = [], []
    for w in casts:
        rows, cols = w.shape
        assert rows % (steps * 16) == 0
        specs.append(pl.BlockSpec((rows // steps, cols), lambda *g: (index_of_step(*g), 0)))
        shapes.append(jax.ShapeDtypeStruct((rows, cols), BF16))
    return specs, shapes


def _mm_kernel(a_ref, b_ref, *rest, epilogue, n_extra, n_cast, cast_weight):
    extras = rest[:n_extra]
    cast_in = rest[n_extra:n_extra + n_cast]
    o_ref = rest[n_extra + n_cast]
    cast_out = rest[n_extra + n_cast + 1:n_extra + 2 * n_cast + 1]
    if cast_weight:
        wb_sc = rest[-1]

        @pl.when(pl.program_id(1) == 0)
        def _():
            wb_sc[...] = b_ref[...].astype(BF16)

        acc = _dot(a_ref[...], wb_sc[...])
    else:
        acc = _dot(a_ref[...], b_ref[...])
    epilogue(acc, o_ref, *extras)
    _cast_blocks(cast_in, cast_out)


def _matmul(a, b, epilogue, out_dtype, extras=(), extra_specs=(), tm=1024, tn=1024, name="mm",
            casts=(), col_blocks=None):
    m, k = a.shape
    cast_weight = col_blocks is not None
    if cast_weight:
        first, stride, count = col_blocks
        n = count * tn
        b_map = lambda j, i: (0, first + stride * j)
    else:
        n = b.shape[1]
        tn = min(tn, n)
        b_map = lambda j, i: (0, j)
    grid = (n // tn, m // tm)
    cast_specs, cast_shapes = _cast_specs(casts, grid[0] * grid[1], lambda j, i: j * grid[1] + i)
    out = pl.pallas_call(
        functools.partial(_mm_kernel, epilogue=epilogue, n_extra=len(extras), n_cast=len(casts),
                          cast_weight=cast_weight),
        grid=grid,
        in_specs=[pl.BlockSpec((tm, k), lambda j, i: (i, 0)),
                  pl.BlockSpec((k, tn), b_map)] + list(extra_specs) + cast_specs,
        out_specs=[pl.BlockSpec((tm, tn), lambda j, i: (i, j))] + cast_specs,
        out_shape=[jax.ShapeDtypeStruct((m, n), out_dtype)] + cast_shapes,
        scratch_shapes=[pltpu.VMEM((k, tn), BF16)] if cast_weight else [],
        compiler_params=_cparams("arbitrary", "arbitrary"),
        name=name,
    )(a, b, *extras, *casts)
    return out if casts else out[0]


def _mmk_kernel(a_ref, b_ref, x_ref, g_ref, o_ref, acc_ref, *, nk):
    kk = pl.program_id(2)

    @pl.when(kk == 0)
    def _():
        acc_ref[...] = jnp.zeros_like(acc_ref)

    acc_ref[...] += _dot(a_ref[...], b_ref[...])

    @pl.when(kk == nk - 1)
    def _():
        o_ref[...] = x_ref[...] + g_ref[...] * acc_ref[...]


def _matmul_k_residual(a, b, x, gate, seq, tm=1024, tn=1024, tk=2048, name="mmk"):
    m, k = a.shape
    n = b.shape[1]
    per = seq // tm
    nk = k // tk
    return pl.pallas_call(
        functools.partial(_mmk_kernel, nk=nk),
        grid=(m // tm, n // tn, nk),
        in_specs=[pl.BlockSpec((tm, tk), lambda i, j, kk: (i, kk)),
                  pl.BlockSpec((tk, tn), lambda i, j, kk: (kk, j)),
                  pl.BlockSpec((tm, tn), lambda i, j, kk: (i, j)),
                  pl.BlockSpec((None, 1, tn), lambda i, j, kk: (i // per, 0, j))],
        out_specs=pl.BlockSpec((tm, tn), lambda i, j, kk: (i, j)),
        out_shape=jax.ShapeDtypeStruct((m, n), F32),
        scratch_shapes=[pltpu.VMEM((tm, tn), F32)],
        compiler_params=_cparams("arbitrary", "arbitrary", "arbitrary"),
        name=name,
    )(a, b, x, gate)


def _compress_kernel(xk_ref, xv_ref, pek_ref, pev_ref, w1k_ref, w2k_ref, w1v_ref, w2v_ref,
                     g_ref, c_ref, sa_ref, sb_ref, ko_ref, vo_ref):
    half = CMP_STRIDE * HEAD_DIM

    def comp(x_ref, pe_ref, w1_ref, w2_ref):
        x = x_ref[...]
        lo = (x + pe_ref[0:1, :]).astype(BF16)
        hi = (x + pe_ref[1:2, :]).astype(BF16)
        p = _dot(lo, w1_ref[0:half, :])
        q = _dot(hi, w1_ref[half:2 * half, :])
        h = p + pltpu.roll(q, q.shape[0] - 1, 0)
        h = jax.nn.gelu(h, approximate=True)
        return _dot(h.astype(BF16), w2_ref[...])

    kc = comp(xk_ref, pek_ref, w1k_ref, w2k_ref)
    kc = _rope(_head_norm(kc, g_ref[...]), c_ref[...], sa_ref[...], sb_ref[...])
    ko_ref[...] = kc.astype(ko_ref.dtype)
    vo_ref[...] = comp(xv_ref, pev_ref, w1v_ref, w2v_ref).astype(vo_ref.dtype)


def _compress(xk, xv, pek, pev, w1k, w2k, w1v, w2v, g, c, sa, sb):
    b, gr, nr, wd = xk.shape
    xspec = pl.BlockSpec((None, None, nr, wd), lambda i, j: (i, j, 0, 0))
    full = lambda arr: pl.BlockSpec(arr.shape, lambda i, j: (0,) * arr.ndim)
    ospec = pl.BlockSpec((None, None, nr, HEAD_DIM), lambda i, j: (i, j, 0, 0))
    oshape = jax.ShapeDtypeStruct((b, gr, nr, HEAD_DIM), BF16)
    return pl.pallas_call(
        _compress_kernel,
        grid=(b, gr),
        in_specs=[xspec, xspec, full(pek), full(pev), full(w1k), full(w2k), full(w1v), full(w2v),
                  full(g), full(c), full(sa), full(sb)],
        out_specs=[ospec, ospec],
        out_shape=[oshape, oshape],
        compiler_params=_cparams("arbitrary", "arbitrary"),
        name="compress",
    )(xk, xv, pek, pev, w1k, w2k, w1v, w2v, g, c, sa, sb)


def _cmp_kernel(q_ref, k_ref, v_ref, ovt_ref, oc_ref, sel_ref, *, tt, n_slc):
    t0 = pl.program_id(2) * tt
    t = t0 + lax.broadcasted_iota(jnp.int32, (tt, LANES), 0)
    n = lax.broadcasted_iota(jnp.int32, (tt, LANES), 1)
    valid = n * CMP_STRIDE + (CMP_BLOCK - 1) <= t
    k = k_ref[...]
    v = v_ref[...]
    psum = jnp.zeros((tt, LANES), F32)
    for r in range(NSA_REP):
        sl = slice(r * HEAD_DIM, (r + 1) * HEAD_DIM)
        s = jnp.where(valid, _dot_nt(q_ref[:, sl], k), NEG)
        m = jnp.max(s, axis=-1, keepdims=True)
        p = jnp.where(valid, jnp.exp2(s - m), 0.0)
        d = jnp.sum(p, axis=-1, keepdims=True)
        p = p / jnp.where(d > 0, d, 1.0)
        psum = psum + p
        oc_ref[:, sl] = _dot(p.astype(BF16), v)
    imp = lax.dot_general(ovt_ref[...], psum, (((1,), (1,)), ((), ())), preferred_element_type=F32,
                          precision=lax.Precision.HIGHEST)[0:n_slc, :]
    j = lax.broadcasted_iota(jnp.int32, (n_slc, tt), 0)
    cur = (t0 + lax.broadcasted_iota(jnp.int32, (n_slc, tt), 1)) >> SEL_SHIFT
    score = jnp.where(j <= cur, imp, -BIG)
    score = jnp.where(j == 0, BIG, jnp.where(j == cur, BIG, jnp.where(j == cur - 1, BIG, score)))
    cnt = jnp.zeros((n_slc, tt), F32)
    for jp in range(n_slc):
        row = score[jp:jp + 1, :]
        ge = jnp.where(row >= score, 1.0, 0.0)
        gt = jnp.where(row > score, 1.0, 0.0)
        cnt = cnt + jnp.where(j > jp, ge, gt)
    sel = jnp.where(cnt < SEL_TOPN, jnp.where(score > -0.5 * BIG, 1.0, 0.0), 0.0)
    sel = jnp.concatenate([sel, jnp.zeros((LANES - n_slc, tt), F32)], axis=0)
    sel_ref[...] = sel.T.astype(sel_ref.dtype)


def _cmp_attention(q, kc, vc, overlap, seq):
    m, _ = q.shape
    b = m // seq
    tt = 512
    per = seq // tt
    n_cmp = (seq - CMP_BLOCK) // CMP_STRIDE + 1
    assert n_cmp <= LANES and n_cmp * CMP_STRIDE + CMP_BLOCK - 1 >= seq
    kspec = pl.BlockSpec((None, None, LANES, HEAD_DIM), lambda i, g, j: (i, g, 0, 0))
    return pl.pallas_call(
        functools.partial(_cmp_kernel, tt=tt, n_slc=seq // SEL_BLOCK),
        grid=(b, NSA_GROUPS, per),
        in_specs=[pl.BlockSpec((tt, NSA_REP * HEAD_DIM), lambda i, g, j: (i * per + j, g)),
                  kspec, kspec,
                  pl.BlockSpec((LANES, LANES), lambda i, g, j: (0, 0))],
        out_specs=[pl.BlockSpec((tt, NSA_REP * HEAD_DIM), lambda i, g, j: (i * per + j, g)),
                   pl.BlockSpec((None, None, tt, LANES), lambda i, g, j: (i, g, j, 0))],
        out_shape=[jax.ShapeDtypeStruct((m, NSA_HEADS * HEAD_DIM), F32),
                   jax.ShapeDtypeStruct((b, NSA_GROUPS, seq, LANES), BF16)],
        compiler_params=_cparams("arbitrary", "arbitrary", "arbitrary"),
        name="cmp_attn",
    )(q, kc, vc, overlap)


def _softmax_pv(t, v):
    m = jnp.max(t, axis=-1, keepdims=True)
    p = jnp.exp2(t - m)
    l = jnp.sum(p, axis=-1, keepdims=True)
    return _dot(p.astype(BF16), v) / l


def _tri_bias(n, lower):
    row = lax.broadcasted_iota(jnp.int32, (n, n), 0)
    col = lax.broadcasted_iota(jnp.int32, (n, n), 1)
    return jnp.where(col <= row, 0.0, NEG) if lower else jnp.where(col > row, 0.0, NEG)


def _nsa_kernel(q_ref, ks_ref, vs_ref, kw_ref, vw_ref, sel_ref, ex_ref, oc_ref, gl_ref, o_ref,
                t_sc, p_sc, l_sc, bs_sc, bw_sc, os_sc, *, tq):
    seq = q_ref.shape[0]
    live = pl.program_id(0) >= 0
    n_back = WINDOW // tq
    w_cols = (n_back + 1) * tq
    bw_sc[:, 0:tq] = _tri_bias(tq, False)
    bw_sc[:, tq:n_back * tq] = jnp.zeros((tq, (n_back - 1) * tq), F32)
    bw_sc[:, n_back * tq:w_cols] = _tri_bias(tq, True)

    jobs = []
    for qt in range(seq // tq):
        lo, hi = qt * tq, (qt + 1) * tq
        jobs.append(("sel", lo, hi, 0))
        jobs.append(("win", lo, hi, max(lo - WINDOW, 0)))

    def head_rows(r):
        return slice(r * tq, (r + 1) * tq)

    def stage_scores(j):
        kind, lo, hi, klo = jobs[j]
        slot = j % 2
        n = hi - klo
        q4 = jnp.concatenate([q_ref[lo:hi, r * HEAD_DIM:(r + 1) * HEAD_DIM] for r in range(NSA_REP)], axis=0)
        k_ref = ks_ref if kind == "sel" else kw_ref
        t_sc[slot, :, 0:n] = _dot_nt(q4, k_ref[klo:hi, :])
        if kind == "sel":
            picked = _dot(sel_ref[lo:hi, :], ex_ref[:, 0:hi])
            bias = jnp.where(picked > 0.5, 0.0, NEG)
            if lo > 0:
                bs_sc[slot, :, 0:lo] = bias[:, 0:lo]
            bs_sc[slot, :, lo:hi] = bias[:, lo:hi] + _tri_bias(tq, True)

    def stage_softmax(j):
        kind, lo, hi, klo = jobs[j]
        slot = j % 2
        n = hi - klo
        for r in range(NSA_REP):
            bias = bs_sc[slot, :, 0:n] if kind == "sel" else bw_sc[:, w_cols - n:w_cols]
            t = t_sc[slot, head_rows(r), 0:n] + bias
            p = jnp.exp2(t - jnp.max(t, axis=-1, keepdims=True))
            l_sc[slot, head_rows(r), :] = jnp.sum(p, axis=-1, keepdims=True)
            p_sc[slot, head_rows(r), 0:n] = p.astype(BF16)

    def stage_output(j):
        kind, lo, hi, klo = jobs[j]
        slot = j % 2
        n = hi - klo
        v_ref = vs_ref if kind == "sel" else vw_ref
        o4 = _dot(p_sc[slot, :, 0:n], v_ref[klo:hi, :]) / l_sc[slot]
        if kind == "sel":
            os_sc[...] = o4
            return
        gate = jax.nn.sigmoid(gl_ref[lo:hi, :])
        for r in range(NSA_REP):
            sl = slice(r * HEAD_DIM, (r + 1) * HEAD_DIM)
            o = (gate[:, 3 * r:3 * r + 1] * oc_ref[lo:hi, sl]
                 + gate[:, 3 * r + 1:3 * r + 2] * os_sc[head_rows(r), :]
                 + gate[:, 3 * r + 2:3 * r + 3] * o4[head_rows(r), :])
            o_ref[lo:hi, sl] = o.astype(o_ref.dtype)

    for i in range(len(jobs) + 2):
        @pl.when(live)
        def _region():
            if i < len(jobs):
                stage_scores(i)
            if 0 <= i - 1 < len(jobs):
                stage_softmax(i - 1)
            if 0 <= i - 2 < len(jobs):
                stage_output(i - 2)


def _nsa_attention(q, kk, vv, sel, expand, oc, gl, seq):
    m = q.shape[0]
    b = m // seq
    width = NSA_REP * HEAD_DIM
    qspec = pl.BlockSpec((seq, width), lambda i, g: (i, g))
    ks = pl.BlockSpec((seq, HEAD_DIM), lambda i, g: (i, g))
    kw = pl.BlockSpec((seq, HEAD_DIM), lambda i, g: (i, NSA_GROUPS + g))
    tq = 128
    rows = NSA_REP * tq
    return pl.pallas_call(
        functools.partial(_nsa_kernel, tq=tq),
        grid=(b, NSA_GROUPS),
        scratch_shapes=[pltpu.VMEM((2, rows, seq), F32), pltpu.VMEM((2, rows, seq), BF16),
                        pltpu.VMEM((2, rows, 1), F32), pltpu.VMEM((2, tq, seq), F32),
                        pltpu.VMEM((tq, WINDOW + tq), F32), pltpu.VMEM((rows, HEAD_DIM), F32)],
        in_specs=[qspec, ks, ks, kw, kw,
                  pl.BlockSpec((None, None, seq, LANES), lambda i, g: (i, g, 0, 0)),
                  pl.BlockSpec((LANES, seq), lambda i, g: (0, 0)),
                  qspec,
                  pl.BlockSpec((seq, LANES), lambda i, g: (i, g))],
        out_specs=qspec,
        out_shape=jax.ShapeDtypeStruct((m, NSA_HEADS * HEAD_DIM), BF16),
        compiler_params=_cparams("arbitrary", "arbitrary"),
        name="nsa_attn",
    )(q, kk, vv, kk, vv, sel, expand, oc, gl)


def _cum_kernel(f_ref, b_ref, o_ref):
    x = f_ref[...] + b_ref[...]
    ls = jnp.minimum(x, 0.0) - jnp.log1p(jnp.exp(-jnp.abs(x)))
    row = lax.broadcasted_iota(jnp.int32, ls.shape, 0)
    d = 1
    while d < ls.shape[0]:
        ls = ls + jnp.where(row >= d, pltpu.roll(ls, d, 0), 0.0)
        d *= 2
    o_ref[...] = ls


def _cum_forget(fl, bias, seq, col_block):
    m = fl.shape[0]
    return pl.pallas_call(
        _cum_kernel,
        grid=(m // seq,),
        in_specs=[pl.BlockSpec((seq, LANES), lambda i: (i, col_block)),
                  pl.BlockSpec((1, LANES), lambda i: (0, 0))],
        out_specs=pl.BlockSpec((seq, LANES), lambda i: (i, 0)),
        out_shape=jax.ShapeDtypeStruct((m, LANES), F32),
        compiler_params=_cparams("arbitrary"),
        name="cum_forget",
    )(fl, bias)


def _fox_kernel(q_ref, k_ref, v_ref, ck_ref, *rest, tq):
    n_cast = (len(rest) - 1) // 2
    o_ref = rest[n_cast]
    _cast_blocks(rest[:n_cast], rest[n_cast + 1:])
    seq = q_ref.shape[0]
    causal_b = _tri_bias(tq, True)
    ck = ck_ref[...] * LOG2E
    for qt in range(seq // tq):
        lo, hi = qt * tq, (qt + 1) * tq
        t = _dot_nt(q_ref[lo:hi, :], k_ref[0:hi, :]) - ck[:, 0:hi]
        diag = t[:, lo:hi] + causal_b
        t = diag if qt == 0 else jnp.concatenate([t[:, 0:lo], diag], axis=1)
        o_ref[lo:hi, :] = _softmax_pv(t, v_ref[0:hi, :]).astype(o_ref.dtype)


def _fox_attention(q, k, v, ck, seq, casts=()):
    m = q.shape[0]
    b = m // seq
    blk = pl.BlockSpec((seq, HEAD_DIM), lambda i, h: (i, h))
    cast_specs, cast_shapes = _cast_specs(casts, b * FOX_HEADS, lambda i, h: i * FOX_HEADS + h)
    return pl.pallas_call(
        functools.partial(_fox_kernel, tq=256),
        grid=(b, FOX_HEADS),
        in_specs=[blk, blk, blk,
                  pl.BlockSpec((None, 1, seq), lambda i, h: (i * FOX_HEADS + h, 0, 0))] + cast_specs,
        out_specs=[blk] + cast_specs,
        out_shape=[jax.ShapeDtypeStruct((m, FOX_HEADS * HEAD_DIM), BF16)] + cast_shapes,
        compiler_params=_cparams("arbitrary", "arbitrary"),
        name="fox_attn",
    )(q, k, v, ck, *casts)


def _merge_kernel(a1_ref, w1_ref, a2_ref, w2_ref, g1_ref, g2_ref, *rest):
    n_cast = (len(rest) - 1) // 2
    o_ref = rest[n_cast]
    u1 = _dot(a1_ref[...], w1_ref[...])
    u2 = _dot(a2_ref[...], w2_ref[...])
    o_ref[...] = (g1_ref[...].astype(F32) * u1 + g2_ref[...].astype(F32) * u2).astype(o_ref.dtype)
    _cast_blocks(rest[:n_cast], rest[n_cast + 1:])


def _merge(a1, w1, a2, w2, gates, casts=()):
    m, k = a1.shape
    n = w1.shape[1]
    tm, tn = 1024, 512
    nb = n // tn
    mb = m // tm
    aspec = pl.BlockSpec((tm, k), lambda j, i: (i, 0))
    wspec = pl.BlockSpec((k, tn), lambda j, i: (0, j))
    cast_specs, cast_shapes = _cast_specs(casts, nb * mb, lambda j, i: j * mb + i)
    return pl.pallas_call(
        _merge_kernel,
        grid=(nb, mb),
        in_specs=[aspec, wspec, aspec, wspec,
                  pl.BlockSpec((tm, tn), lambda j, i: (i, j)),
                  pl.BlockSpec((tm, tn), lambda j, i: (i, nb + j))] + cast_specs,
        out_specs=[pl.BlockSpec((tm, tn), lambda j, i: (i, j))] + cast_specs,
        out_shape=[jax.ShapeDtypeStruct((m, n), BF16)] + cast_shapes,
        compiler_params=_cparams("arbitrary", "arbitrary"),
        name="merge",
    )(a1, w1, a2, w2, gates, gates, *casts)


def _rope_tables(pos):
    half = ROT_DIM // 2
    inv = ROPE_THETA ** (-jnp.arange(half, dtype=F32) / half)
    ang = jnp.asarray(pos, dtype=F32)[:, None] * inv[None, :]
    cos, sin = jnp.cos(ang), jnp.sin(ang)
    n = ang.shape[0]
    one = jnp.ones((n, HEAD_DIM - ROT_DIM), F32)
    zero = jnp.zeros((n, HEAD_DIM - ROT_DIM), F32)
    zh = jnp.zeros((n, half), F32)
    c = jnp.concatenate([cos, cos, one], axis=1)
    sa = jnp.concatenate([-sin, zh, zero], axis=1)
    sb = jnp.concatenate([zh, sin, zero], axis=1)
    return c, sa, sb


def _layer(x2, c_pad, bsz, seq, w_ada, b_ada, norm1_g, norm2_g, w_in, b_forget, nsa_q_norm, nsa_k_norm,
           fox_q_norm, fox_k_norm, cmp_pos_k, cmp_pos_v, w_cmp_k1, w_cmp_k2, w_cmp_v1, w_cmp_v2,
           w_up_nsa, w_up_fox, w_out, w_ff1, w_ff2):
    d = x2.shape[1]
    hd = HEAD_DIM
    tm = 1024
    per_m = seq // tm

    mod = _ada(c_pad, w_ada, b_ada.reshape(1, -1))[:bsz]
    shift1, scale1, gate1, shift2, scale2, gate2 = [t.reshape(bsz, 1, d) for t in jnp.split(mod, 6, axis=-1)]

    h = _normmod(x2, norm1_g.reshape(1, d), scale1, shift1, seq)

    o_q = 0
    o_kv = o_q + NSA_HEADS * hd
    o_gate = o_kv + 3 * 2 * NSA_GROUPS * hd
    o_fox = o_gate + 3 * NSA_HEADS
    o_f = o_fox + 3 * FOX_HEADS * hd
    o_merge = o_f + FOX_HEADS
    gw = NSA_GROUPS * hd

    def seg(*ranges):
        parts = [w_in[:, a:b] for a, b in ranges]
        return (parts[0] if len(parts) == 1 else jnp.concatenate(parts, axis=1)).astype(BF16)

    pos = np.arange(seq)
    c_t, sa_t, sb_t = _rope_tables(pos)
    tab_specs = [pl.BlockSpec((tm, hd), lambda j, i: (i % per_m, 0))] * 3
    gspec = pl.BlockSpec((1, hd), lambda j, i: (0, 0))

    q_scale = ATTN_SCALE * LOG2E
    assert o_kv % gw == 0
    kv0 = o_kv // gw
    q_n = _matmul(h, w_in, _ep_headnorm_rope, BF16, tn=gw, col_blocks=(0, 1, kv0),
                  extras=(q_scale * nsa_q_norm.reshape(1, hd), c_t, sa_t, sb_t), extra_specs=[gspec] + tab_specs,
                  name="proj_nsa_q")
    k_sw = _matmul(h, w_in, _ep_headnorm_rope, BF16, tn=gw, col_blocks=(kv0 + 2, 2, 2),
                   extras=(nsa_k_norm.reshape(1, hd), c_t, sa_t, sb_t), extra_specs=[gspec] + tab_specs,
                   name="proj_nsa_k")
    v_sw = _matmul(h, w_in, _ep_raw, BF16, tn=gw, col_blocks=(kv0 + 3, 2, 2), name="proj_nsa_v")
    kv_c = _matmul(h, w_in, _ep_raw, F32, tn=gw, col_blocks=(kv0, 1, 2), name="proj_nsa_cmp")
    fq = _matmul(h, seg((o_fox, o_fox + FOX_HEADS * hd)), _ep_headnorm, BF16,
                 extras=(q_scale * fox_q_norm.reshape(1, hd),), extra_specs=[gspec], name="proj_fox_q")
    fk = _matmul(h, seg((o_fox + FOX_HEADS * hd, o_fox + 2 * FOX_HEADS * hd)), _ep_headnorm, BF16,
                 extras=(fox_k_norm.reshape(1, hd),), extra_specs=[gspec], name="proj_fox_k")
    fv = _matmul(h, seg((o_fox + 2 * FOX_HEADS * hd, o_f)), _ep_raw, BF16, name="proj_fox_v")
    g_merge = _matmul(h, seg((o_merge, o_merge + 2 * d)), _ep_sigmoid, BF16, name="proj_merge")

    per_group = NSA_REP * 3
    gate_blocks = [jnp.pad(w_in[:, o_gate + g * per_group:o_gate + (g + 1) * per_group],
                           ((0, 0), (0, LANES - per_group))) for g in range(NSA_GROUPS)]
    f_block = jnp.pad(w_in[:, o_f:o_merge], ((0, 0), (0, LANES - FOX_HEADS)))
    w_small = jnp.concatenate(gate_blocks + [f_block], axis=1).astype(BF16)
    small = _matmul(h, w_small, _ep_raw, F32, tn=w_small.shape[1], name="proj_small")

    n_rows = seq // CMP_STRIDE
    def blocks_view(cols):
        t = kv_c[:, cols].reshape(bsz, seq, NSA_GROUPS, hd).transpose(0, 2, 1, 3)
        return t.reshape(bsz, NSA_GROUPS, n_rows, CMP_STRIDE * hd)
    end_pos = np.arange(n_rows) * CMP_STRIDE + CMP_BLOCK - 1
    ce, sae, sbe = _rope_tables(end_pos)
    k_cmp, v_cmp = _compress(
        blocks_view(slice(0, gw)), blocks_view(slice(gw, 2 * gw)),
        cmp_pos_k.reshape(2, CMP_STRIDE * hd), cmp_pos_v.reshape(2, CMP_STRIDE * hd),
        w_cmp_k1.astype(BF16), w_cmp_k2.astype(BF16), w_cmp_v1.astype(BF16), w_cmp_v2.astype(BF16),
        nsa_k_norm.reshape(1, hd), ce, sae, sbe)

    n_slc = seq // SEL_BLOCK
    ci = np.arange(LANES)[:, None] * CMP_STRIDE
    sj = np.arange(LANES)[None, :] * SEL_BLOCK
    overlap = ((ci < sj + SEL_BLOCK) & (ci + CMP_BLOCK > sj) & (np.arange(LANES)[None, :] < n_slc)
               & (np.arange(LANES)[:, None] < n_rows - 1)).astype(np.float32)
    o_c, sel = _cmp_attention(q_n, k_cmp, v_cmp, jnp.asarray(overlap.T), seq)

    expand = (np.arange(LANES)[:, None] == (np.arange(seq)[None, :] >> SEL_SHIFT)).astype(np.float32)
    o_nsa = _nsa_attention(q_n, k_sw, v_sw, sel, jnp.asarray(expand, dtype=BF16), o_c, small, seq)

    f_bias = jnp.pad(b_forget, (0, LANES - FOX_HEADS)).reshape(1, LANES)
    cum = _cum_forget(small, f_bias, seq, NSA_GROUPS)
    ck = cum[:, :FOX_HEADS].reshape(bsz, seq, FOX_HEADS).transpose(0, 2, 1).reshape(bsz * FOX_HEADS, 1, seq)
    o_fox, up_nsa_b, up_fox_b, w_out_b = _fox_attention(fq, fk, fv, ck, seq, casts=(w_up_nsa, w_up_fox, w_out))

    y, w_ff1_b = _merge(o_nsa, up_nsa_b, o_fox, up_fox_b, g_merge, casts=(w_ff1,))
    tn_out = 512
    res_specs = [pl.BlockSpec((tm, tn_out), lambda j, i: (i, j)),
                 pl.BlockSpec((None, 1, tn_out), lambda j, i: (i // per_m, 0, j))]
    x_mid = _matmul(y, w_out_b, _ep_residual, F32, extras=(x2, gate1), extra_specs=res_specs,
                    tn=tn_out, name="out_proj")

    h2 = _normmod(x_mid, norm2_g.reshape(1, d), scale2, shift2, seq)
    hid, w_ff2_b = _matmul(h2, w_ff1_b, _ep_relu2, BF16, casts=(w_ff2,), name="ff1")
    return _matmul_k_residual(hid, w_ff2_b, x_mid, gate2, seq, name="ff2")


def kernel(x, c, w_ada, b_ada, norm1_g, norm2_g, w_in, b_forget, nsa_q_norm, nsa_k_norm, fox_q_norm, fox_k_norm, cmp_pos_k, cmp_pos_v, w_cmp_k1, w_cmp_k2, w_cmp_v1, w_cmp_v2, w_up_nsa, w_up_fox, w_out, w_ff1, w_ff2):
    bsz, seq, d = x.shape
    x2 = x.reshape(bsz * seq, d)
    c_pad = jnp.pad(c, ((0, 8 - bsz), (0, 0)))
    params = (w_ada, b_ada, norm1_g, norm2_g, w_in, b_forget, nsa_q_norm, nsa_k_norm, fox_q_norm, fox_k_norm,
              cmp_pos_k, cmp_pos_v, w_cmp_k1, w_cmp_k2, w_cmp_v1, w_cmp_v2, w_up_nsa, w_up_fox, w_out,
              w_ff1, w_ff2)
    for layer in range(w_ada.shape[0]):
        x2 = _layer(x2, c_pad, bsz, seq, *[p[layer] for p in params])
    return x2.reshape(bsz, seq, d)
```

```python
import functools

import numpy as np
import jax
import jax.numpy as jnp
from jax import lax
from jax.experimental import pallas as pl
from jax.experimental.pallas import tpu as pltpu

F32 = jnp.float32
BF16 = jnp.bfloat16

HEAD_DIM = 128
NSA_HEADS = 16
NSA_GROUPS = 4
NSA_REP = NSA_HEADS // NSA_GROUPS
FOX_HEADS = 16
CMP_BLOCK = 32
CMP_STRIDE = 16
SEL_BLOCK = 64
SEL_SHIFT = SEL_BLOCK.bit_length() - 1
SEL_TOPN = 16
WINDOW = 512
ROPE_THETA = 500000.0
ROT_DIM = HEAD_DIM // 4
RMS_EPS = 1e-6
ATTN_SCALE = HEAD_DIM ** -0.5

LOG2E = float(np.log2(np.e))
NEG = -1e30
BIG = 1e30
LANES = 128
VMEM_LIMIT = 56 * 1024 * 1024


def _cparams(*sem):
    return pltpu.CompilerParams(dimension_semantics=sem, vmem_limit_bytes=VMEM_LIMIT)


def _dot(a, b):
    return jnp.dot(a, b, preferred_element_type=F32)


def _dot_nt(a, b):
    return lax.dot_general(a, b, (((1,), (1,)), ((), ())), preferred_element_type=F32)


def _ada_kernel(c_ref, w_ref, b_ref, o_ref):
    c = c_ref[...]
    s = (c * jax.nn.sigmoid(c)).astype(BF16)
    o_ref[...] = _dot(s, w_ref[...].astype(BF16)) + b_ref[...]


def _ada(c_pad, w, b):
    rows, d = c_pad.shape
    n = w.shape[1]
    tn = 1024
    return pl.pallas_call(
        _ada_kernel,
        grid=(n // tn,),
        in_specs=[pl.BlockSpec((rows, d), lambda j: (0, 0)),
                  pl.BlockSpec((d, tn), lambda j: (0, j)),
                  pl.BlockSpec((1, tn), lambda j: (0, j))],
        out_specs=pl.BlockSpec((rows, tn), lambda j: (0, j)),
        out_shape=jax.ShapeDtypeStruct((rows, n), F32),
        compiler_params=_cparams("arbitrary"),
        name="ada",
    )(c_pad, w, b)


def _normmod_kernel(x_ref, g_ref, sc_ref, sh_ref, o_ref):
    x = x_ref[...]
    ms = jnp.mean(x * x, axis=-1, keepdims=True)
    y = x * lax.rsqrt(ms + RMS_EPS) * g_ref[...]
    o_ref[...] = (y * (1.0 + sc_ref[...]) + sh_ref[...]).astype(o_ref.dtype)


def _normmod(x2, g, scale, shift, seq):
    m, d = x2.shape
    tr = 256
    per = seq // tr
    return pl.pallas_call(
        _normmod_kernel,
        grid=(m // tr,),
        in_specs=[pl.BlockSpec((tr, d), lambda i: (i, 0)),
                  pl.BlockSpec((1, d), lambda i: (0, 0)),
                  pl.BlockSpec((None, 1, d), lambda i: (i // per, 0, 0)),
                  pl.BlockSpec((None, 1, d), lambda i: (i // per, 0, 0))],
        out_specs=pl.BlockSpec((tr, d), lambda i: (i, 0)),
        out_shape=jax.ShapeDtypeStruct((m, d), BF16),
        compiler_params=_cparams("arbitrary"),
        name="normmod",
    )(x2, g, scale, shift)


def _ep_raw(acc, o_ref):
    o_ref[...] = acc.astype(o_ref.dtype)


def _ep_sigmoid(acc, o_ref):
    o_ref[...] = jax.nn.sigmoid(acc).astype(o_ref.dtype)


def _ep_relu2(acc, o_ref):
    r = jnp.maximum(acc, 0.0)
    o_ref[...] = (r * r).astype(o_ref.dtype)


def _ep_residual(acc, o_ref, x_ref, g_ref):
    o_ref[...] = x_ref[...] + g_ref[...] * acc


def _head_norm(a, g):
    ms = jnp.mean(a * a, axis=-1, keepdims=True)
    return a * lax.rsqrt(ms + RMS_EPS) * g


def _rope(a, c, sa, sb):
    return a * c + pltpu.roll(a, LANES - ROT_DIM // 2, 1) * sa + pltpu.roll(a, ROT_DIM // 2, 1) * sb


def _ep_headnorm(acc, o_ref, g_ref):
    g = g_ref[...]
    for h in range(acc.shape[1] // HEAD_DIM):
        sl = slice(h * HEAD_DIM, (h + 1) * HEAD_DIM)
        o_ref[:, sl] = _head_norm(acc[:, sl], g).astype(o_ref.dtype)


def _ep_headnorm_rope(acc, o_ref, g_ref, c_ref, sa_ref, sb_ref):
    g = g_ref[...]
    c, sa, sb = c_ref[...], sa_ref[...], sb_ref[...]
    for h in range(acc.shape[1] // HEAD_DIM):
        sl = slice(h * HEAD_DIM, (h + 1) * HEAD_DIM)
        o_ref[:, sl] = _rope(_head_norm(acc[:, sl], g), c, sa, sb).astype(o_ref.dtype)


def _cast_blocks(refs_in, refs_out):
    for src, dst in zip(refs_in, refs_out):
        dst[...] = src[...].astype(dst.dtype)


def _cast_specs(casts, steps, index_of_step):
    specs, shapes = [], []
    for w in casts:
        rows, cols = w.shape
        assert rows % (steps * 16) == 0
        specs.append(pl.BlockSpec((rows // steps, cols), lambda *g: (index_of_step(*g), 0)))
        shapes.append(jax.ShapeDtypeStruct((rows, cols), BF16))
    return specs, shapes


def _mm_kernel(a_ref, b_ref, *rest, epilogue, n_extra, n_cast, transposed):
    extras = rest[:n_extra]
    cast_in = rest[n_extra:n_extra + n_cast]
    o_ref = rest[n_extra + n_cast]
    cast_out = rest[n_extra + n_cast + 1:]
    acc = _dot_nt(a_ref[...], b_ref[...]) if transposed else _dot(a_ref[...], b_ref[...])
    epilogue(acc, o_ref, *extras)
    _cast_blocks(cast_in, cast_out)


def _matmul(a, b, epilogue, out_dtype, extras=(), extra_specs=(), tm=1024, tn=1024, name="mm",
            casts=(), rows=None):
    m, k = a.shape
    if rows is None:
        n = b.shape[1]
        tn = min(tn, n)
        b_spec = pl.BlockSpec((k, tn), lambda j, i: (0, j))
    else:
        first, stride, count = rows
        assert first % 16 == 0 and stride % 16 == 0
        n = count * tn
        b_spec = pl.BlockSpec((pl.Element(tn), pl.Element(k)),
                              lambda j, i: (pl.multiple_of(first + j * stride, 16), 0))
    grid = (n // tn, m // tm)
    cast_specs, cast_shapes = _cast_specs(casts, grid[0] * grid[1], lambda j, i: j * grid[1] + i)
    out = pl.pallas_call(
        functools.partial(_mm_kernel, epilogue=epilogue, n_extra=len(extras), n_cast=len(casts),
                          transposed=rows is not None),
        grid=grid,
        in_specs=[pl.BlockSpec((tm, k), lambda j, i: (i, 0)), b_spec] + list(extra_specs) + cast_specs,
        out_specs=[pl.BlockSpec((tm, tn), lambda j, i: (i, j))] + cast_specs,
        out_shape=[jax.ShapeDtypeStruct((m, n), out_dtype)] + cast_shapes,
        compiler_params=_cparams("arbitrary", "arbitrary"),
        name=name,
    )(a, b, *extras, *casts)
    return out if casts else out[0]


def _mmk_kernel(a_ref, b_ref, x_ref, g_ref, o_ref, acc_ref, *, nk):
    kk = pl.program_id(2)

    @pl.when(kk == 0)
    def _():
        acc_ref[...] = jnp.zeros_like(acc_ref)

    acc_ref[...] += _dot(a_ref[...], b_ref[...])

    @pl.when(kk == nk - 1)
    def _():
        o_ref[...] = x_ref[...] + g_ref[...] * acc_ref[...]


def _matmul_k_residual(a, b, x, gate, seq, tm=1024, tn=1024, tk=2048, name="mmk"):
    m, k = a.shape
    n = b.shape[1]
    per = seq // tm
    nk = k // tk
    return pl.pallas_call(
        functools.partial(_mmk_kernel, nk=nk),
        grid=(m // tm, n // tn, nk),
        in_specs=[pl.BlockSpec((tm, tk), lambda i, j, kk: (i, kk)),
                  pl.BlockSpec((tk, tn), lambda i, j, kk: (kk, j)),
                  pl.BlockSpec((tm, tn), lambda i, j, kk: (i, j)),
                  pl.BlockSpec((None, 1, tn), lambda i, j, kk: (i // per, 0, j))],
        out_specs=pl.BlockSpec((tm, tn), lambda i, j, kk: (i, j)),
        out_shape=jax.ShapeDtypeStruct((m, n), F32),
        scratch_shapes=[pltpu.VMEM((tm, tn), F32)],
        compiler_params=_cparams("arbitrary", "arbitrary", "arbitrary"),
        name=name,
    )(a, b, x, gate)


def _compress_kernel(xk_ref, xv_ref, pek_ref, pev_ref, w1k_ref, w2k_ref, w1v_ref, w2v_ref,
                     g_ref, c_ref, sa_ref, sb_ref, ko_ref, vo_ref):
    half = CMP_STRIDE * HEAD_DIM

    def comp(x_ref, pe_ref, w1_ref, w2_ref):
        x = x_ref[...]
        lo = (x + pe_ref[0:1, :]).astype(BF16)
        hi = (x + pe_ref[1:2, :]).astype(BF16)
        p = _dot(lo, w1_ref[0:half, :])
        q = _dot(hi, w1_ref[half:2 * half, :])
        h = p + pltpu.roll(q, q.shape[0] - 1, 0)
        h = jax.nn.gelu(h, approximate=True)
        return _dot(h.astype(BF16), w2_ref[...])

    kc = comp(xk_ref, pek_ref, w1k_ref, w2k_ref)
    kc = _rope(_head_norm(kc, g_ref[...]), c_ref[...], sa_ref[...], sb_ref[...])
    ko_ref[...] = kc.astype(ko_ref.dtype)
    vo_ref[...] = comp(xv_ref, pev_ref, w1v_ref, w2v_ref).astype(vo_ref.dtype)


def _compress(xk, xv, pek, pev, w1k, w2k, w1v, w2v, g, c, sa, sb):
    b, gr, nr, wd = xk.shape
    xspec = pl.BlockSpec((None, None, nr, wd), lambda i, j: (i, j, 0, 0))
    full = lambda arr: pl.BlockSpec(arr.shape, lambda i, j: (0,) * arr.ndim)
    ospec = pl.BlockSpec((None, None, nr, HEAD_DIM), lambda i, j: (i, j, 0, 0))
    oshape = jax.ShapeDtypeStruct((b, gr, nr, HEAD_DIM), BF16)
    return pl.pallas_call(
        _compress_kernel,
        grid=(b, gr),
        in_specs=[xspec, xspec, full(pek), full(pev), full(w1k), full(w2k), full(w1v), full(w2v),
                  full(g), full(c), full(sa), full(sb)],
        out_specs=[ospec, ospec],
        out_shape=[oshape, oshape],
        compiler_params=_cparams("arbitrary", "arbitrary"),
        name="compress",
    )(xk, xv, pek, pev, w1k, w2k, w1v, w2v, g, c, sa, sb)


def _cmp_kernel(q_ref, k_ref, v_ref, ovt_ref, oc_ref, sel_ref, *, tt, n_slc):
    t0 = pl.program_id(2) * tt
    t = t0 + lax.broadcasted_iota(jnp.int32, (tt, LANES), 0)
    n = lax.broadcasted_iota(jnp.int32, (tt, LANES), 1)
    valid = n * CMP_STRIDE + (CMP_BLOCK - 1) <= t
    k = k_ref[...]
    v = v_ref[...]
    psum = jnp.zeros((tt, LANES), F32)
    for r in range(NSA_REP):
        sl = slice(r * HEAD_DIM, (r + 1) * HEAD_DIM)
        s = jnp.where(valid, _dot_nt(q_ref[:, sl], k), NEG)
        m = jnp.max(s, axis=-1, keepdims=True)
        p = jnp.where(valid, jnp.exp2(s - m), 0.0)
        d = jnp.sum(p, axis=-1, keepdims=True)
        p = p / jnp.where(d > 0, d, 1.0)
        psum = psum + p
        oc_ref[:, sl] = _dot(p.astype(BF16), v)
    imp = lax.dot_general(ovt_ref[...], psum, (((1,), (1,)), ((), ())), preferred_element_type=F32,
                          precision=lax.Precision.HIGHEST)[0:n_slc, :]
    j = lax.broadcasted_iota(jnp.int32, (n_slc, tt), 0)
    cur = (t0 + lax.broadcasted_iota(jnp.int32, (n_slc, tt), 1)) >> SEL_SHIFT
    score = jnp.where(j <= cur, imp, -BIG)
    score = jnp.where(j == 0, BIG, jnp.where(j == cur, BIG, jnp.where(j == cur - 1, BIG, score)))
    cnt = jnp.zeros((n_slc, tt), F32)
    for jp in range(n_slc):
        row = score[jp:jp + 1, :]
        ge = jnp.where(row >= score, 1.0, 0.0)
        gt = jnp.where(row > score, 1.0, 0.0)
        cnt = cnt + jnp.where(j > jp, ge, gt)
    sel = jnp.where(cnt < SEL_TOPN, jnp.where(score > -0.5 * BIG, 1.0, 0.0), 0.0)
    sel = jnp.concatenate([sel, jnp.zeros((LANES - n_slc, tt), F32)], axis=0)
    sel_ref[...] = sel.T.astype(sel_ref.dtype)


def _cmp_attention(q, kc, vc, overlap, seq):
    m, _ = q.shape
    b = m // seq
    tt = 512
    per = seq // tt
    n_cmp = (seq - CMP_BLOCK) // CMP_STRIDE + 1
    assert n_cmp <= LANES and n_cmp * CMP_STRIDE + CMP_BLOCK - 1 >= seq
    kspec = pl.BlockSpec((None, None, LANES, HEAD_DIM), lambda i, g, j: (i, g, 0, 0))
    return pl.pallas_call(
        functools.partial(_cmp_kernel, tt=tt, n_slc=seq // SEL_BLOCK),
        grid=(b, NSA_GROUPS, per),
        in_specs=[pl.BlockSpec((tt, NSA_REP * HEAD_DIM), lambda i, g, j: (i * per + j, g)),
                  kspec, kspec,
                  pl.BlockSpec((LANES, LANES), lambda i, g, j: (0, 0))],
        out_specs=[pl.BlockSpec((tt, NSA_REP * HEAD_DIM), lambda i, g, j: (i * per + j, g)),
                   pl.BlockSpec((None, None, tt, LANES), lambda i, g, j: (i, g, j, 0))],
        out_shape=[jax.ShapeDtypeStruct((m, NSA_HEADS * HEAD_DIM), F32),
                   jax.ShapeDtypeStruct((b, NSA_GROUPS, seq, LANES), BF16)],
        compiler_params=_cparams("arbitrary", "arbitrary", "arbitrary"),
        name="cmp_attn",
    )(q, kc, vc, overlap)


def _softmax_pv(t, v):
    m = jnp.max(t, axis=-1, keepdims=True)
    p = jnp.exp2(t - m)
    l = jnp.sum(p, axis=-1, keepdims=True)
    return _dot(p.astype(BF16), v) / l


def _tri_bias(n, lower):
    row = lax.broadcasted_iota(jnp.int32, (n, n), 0)
    col = lax.broadcasted_iota(jnp.int32, (n, n), 1)
    return jnp.where(col <= row, 0.0, NEG) if lower else jnp.where(col > row, 0.0, NEG)


def _nsa_kernel(q_ref, ks_ref, vs_ref, kw_ref, vw_ref, sel_ref, ex_ref, oc_ref, gl_ref, o_ref,
                t_sc, p_sc, l_sc, bs_sc, bw_sc, os_sc, *, tq):
    seq = q_ref.shape[0]
    live = pl.program_id(0) >= 0
    n_back = WINDOW // tq
    w_cols = (n_back + 1) * tq
    bw_sc[:, 0:tq] = _tri_bias(tq, False)
    bw_sc[:, tq:n_back * tq] = jnp.zeros((tq, (n_back - 1) * tq), F32)
    bw_sc[:, n_back * tq:w_cols] = _tri_bias(tq, True)

    jobs = []
    for qt in range(seq // tq):
        lo, hi = qt * tq, (qt + 1) * tq
        jobs.append(("sel", lo, hi, 0))
        jobs.append(("win", lo, hi, max(lo - WINDOW, 0)))

    def head_rows(r):
        return slice(r * tq, (r + 1) * tq)

    def stage_scores(j):
        kind, lo, hi, klo = jobs[j]
        slot = j % 2
        n = hi - klo
        q4 = jnp.concatenate([q_ref[lo:hi, r * HEAD_DIM:(r + 1) * HEAD_DIM] for r in range(NSA_REP)], axis=0)
        k_ref = ks_ref if kind == "sel" else kw_ref
        t_sc[slot, :, 0:n] = _dot_nt(q4, k_ref[klo:hi, :])
        if kind == "sel":
            picked = _dot(sel_ref[lo:hi, :], ex_ref[:, 0:hi])
            bias = jnp.where(picked > 0.5, 0.0, NEG)
            if lo > 0:
                bs_sc[slot, :, 0:lo] = bias[:, 0:lo]
            bs_sc[slot, :, lo:hi] = bias[:, lo:hi] + _tri_bias(tq, True)

    def stage_softmax(j):
        kind, lo, hi, klo = jobs[j]
        slot = j % 2
        n = hi - klo
        for r in range(NSA_REP):
            bias = bs_sc[slot, :, 0:n] if kind == "sel" else bw_sc[:, w_cols - n:w_cols]
            t = t_sc[slot, head_rows(r), 0:n] + bias
            p = jnp.exp2(t - jnp.max(t, axis=-1, keepdims=True))
            l_sc[slot, head_rows(r), :] = jnp.sum(p, axis=-1, keepdims=True)
            p_sc[slot, head_rows(r), 0:n] = p.astype(BF16)

    def stage_output(j):
        kind, lo, hi, klo = jobs[j]
        slot = j % 2
        n = hi - klo
        v_ref = vs_ref if kind == "sel" else vw_ref
        o4 = _dot(p_sc[slot, :, 0:n], v_ref[klo:hi, :]) / l_sc[slot]
        if kind == "sel":
            os_sc[...] = o4
            return
        gate = jax.nn.sigmoid(gl_ref[lo:hi, :])
        for r in range(NSA_REP):
            sl = slice(r * HEAD_DIM, (r + 1) * HEAD_DIM)
            o = (gate[:, 3 * r:3 * r + 1] * oc_ref[lo:hi, sl]
                 + gate[:, 3 * r + 1:3 * r + 2] * os_sc[head_rows(r), :]
                 + gate[:, 3 * r + 2:3 * r + 3] * o4[head_rows(r), :])
            o_ref[lo:hi, sl] = o.astype(o_ref.dtype)

    for i in range(len(jobs) + 2):
        @pl.when(live)
        def _region():
            if i < len(jobs):
                stage_scores(i)
            if 0 <= i - 1 < len(jobs):
                stage_softmax(i - 1)
            if 0 <= i - 2 < len(jobs):
                stage_output(i - 2)


def _nsa_attention(q, kk, vv, sel, expand, oc, gl, seq):
    m = q.shape[0]
    b = m // seq
    width = NSA_REP * HEAD_DIM
    qspec = pl.BlockSpec((seq, width), lambda i, g: (i, g))
    ks = pl.BlockSpec((seq, HEAD_DIM), lambda i, g: (i, g))
    kw = pl.BlockSpec((seq, HEAD_DIM), lambda i, g: (i, NSA_GROUPS + g))
    tq = 128
    rows = NSA_REP * tq
    return pl.pallas_call(
        functools.partial(_nsa_kernel, tq=tq),
        grid=(b, NSA_GROUPS),
        scratch_shapes=[pltpu.VMEM((2, rows, seq), F32), pltpu.VMEM((2, rows, seq), BF16),
                        pltpu.VMEM((2, rows, 1), F32), pltpu.VMEM((2, tq, seq), F32),
                        pltpu.VMEM((tq, WINDOW + tq), F32), pltpu.VMEM((rows, HEAD_DIM), F32)],
        in_specs=[qspec, ks, ks, kw, kw,
                  pl.BlockSpec((None, None, seq, LANES), lambda i, g: (i, g, 0, 0)),
                  pl.BlockSpec((LANES, seq), lambda i, g: (0, 0)),
                  qspec,
                  pl.BlockSpec((seq, LANES), lambda i, g: (i, g))],
        out_specs=qspec,
        out_shape=jax.ShapeDtypeStruct((m, NSA_HEADS * HEAD_DIM), BF16),
        compiler_params=_cparams("arbitrary", "arbitrary"),
        name="nsa_attn",
    )(q, kk, vv, kk, vv, sel, expand, oc, gl)


def _cum_kernel(f_ref, b_ref, o_ref):
    x = f_ref[...] + b_ref[...]
    ls = jnp.minimum(x, 0.0) - jnp.log1p(jnp.exp(-jnp.abs(x)))
    row = lax.broadcasted_iota(jnp.int32, ls.shape, 0)
    d = 1
    while d < ls.shape[0]:
        ls = ls + jnp.where(row >= d, pltpu.roll(ls, d, 0), 0.0)
        d *= 2
    o_ref[...] = ls


def _cum_forget(fl, bias, seq, col_block):
    m = fl.shape[0]
    return pl.pallas_call(
        _cum_kernel,
        grid=(m // seq,),
        in_specs=[pl.BlockSpec((seq, LANES), lambda i: (i, col_block)),
                  pl.BlockSpec((1, LANES), lambda i: (0, 0))],
        out_specs=pl.BlockSpec((seq, LANES), lambda i: (i, 0)),
        out_shape=jax.ShapeDtypeStruct((m, LANES), F32),
        compiler_params=_cparams("arbitrary"),
        name="cum_forget",
    )(fl, bias)


def _fox_kernel(q_ref, k_ref, v_ref, ck_ref, *rest, tq):
    n_cast = (len(rest) - 1) // 2
    o_ref = rest[n_cast]
    _cast_blocks(rest[:n_cast], rest[n_cast + 1:])
    seq = q_ref.shape[0]
    causal_b = _tri_bias(tq, True)
    ck = ck_ref[...] * LOG2E
    for qt in range(seq // tq):
        lo, hi = qt * tq, (qt + 1) * tq
        t = _dot_nt(q_ref[lo:hi, :], k_ref[0:hi, :]) - ck[:, 0:hi]
        diag = t[:, lo:hi] + causal_b
        t = diag if qt == 0 else jnp.concatenate([t[:, 0:lo], diag], axis=1)
        o_ref[lo:hi, :] = _softmax_pv(t, v_ref[0:hi, :]).astype(o_ref.dtype)


def _fox_attention(q, k, v, ck, seq, casts=()):
    m = q.shape[0]
    b = m // seq
    blk = pl.BlockSpec((seq, HEAD_DIM), lambda i, h: (i, h))
    cast_specs, cast_shapes = _cast_specs(casts, b * FOX_HEADS, lambda i, h: i * FOX_HEADS + h)
    out = pl.pallas_call(
        functools.partial(_fox_kernel, tq=256),
        grid=(b, FOX_HEADS),
        in_specs=[blk, blk, blk,
                  pl.BlockSpec((None, 1, seq), lambda i, h: (i * FOX_HEADS + h, 0, 0))] + cast_specs,
        out_specs=[blk] + cast_specs,
        out_shape=[jax.ShapeDtypeStruct((m, FOX_HEADS * HEAD_DIM), BF16)] + cast_shapes,
        compiler_params=_cparams("arbitrary", "arbitrary"),
        name="fox_attn",
    )(q, k, v, ck, *casts)
    return out if casts else out[0]


def _merge_kernel(a1_ref, w1_ref, a2_ref, w2_ref, g1_ref, g2_ref, *rest):
    n_cast = (len(rest) - 1) // 2
    o_ref = rest[n_cast]
    u1 = _dot(a1_ref[...], w1_ref[...])
    u2 = _dot(a2_ref[...], w2_ref[...])
    o_ref[...] = (g1_ref[...].astype(F32) * u1 + g2_ref[...].astype(F32) * u2).astype(o_ref.dtype)
    _cast_blocks(rest[:n_cast], rest[n_cast + 1:])


def _merge(a1, w1, a2, w2, gates, casts=()):
    m, k = a1.shape
    n = w1.shape[1]
    tm, tn = 1024, 512
    nb = n // tn
    mb = m // tm
    aspec = pl.BlockSpec((tm, k), lambda j, i: (i, 0))
    wspec = pl.BlockSpec((k, tn), lambda j, i: (0, j))
    cast_specs, cast_shapes = _cast_specs(casts, nb * mb, lambda j, i: j * mb + i)
    out = pl.pallas_call(
        _merge_kernel,
        grid=(nb, mb),
        in_specs=[aspec, wspec, aspec, wspec,
                  pl.BlockSpec((tm, tn), lambda j, i: (i, j)),
                  pl.BlockSpec((tm, tn), lambda j, i: (i, nb + j))] + cast_specs,
        out_specs=[pl.BlockSpec((tm, tn), lambda j, i: (i, j))] + cast_specs,
        out_shape=[jax.ShapeDtypeStruct((m, n), BF16)] + cast_shapes,
        compiler_params=_cparams("arbitrary", "arbitrary"),
        name="merge",
    )(a1, w1, a2, w2, gates, gates, *casts)
    return out if casts else out[0]


def _rope_tables(pos):
    half = ROT_DIM // 2
    inv = ROPE_THETA ** (-jnp.arange(half, dtype=F32) / half)
    ang = jnp.asarray(pos, dtype=F32)[:, None] * inv[None, :]
    cos, sin = jnp.cos(ang), jnp.sin(ang)
    n = ang.shape[0]
    one = jnp.ones((n, HEAD_DIM - ROT_DIM), F32)
    zero = jnp.zeros((n, HEAD_DIM - ROT_DIM), F32)
    zh = jnp.zeros((n, half), F32)
    c = jnp.concatenate([cos, cos, one], axis=1)
    sa = jnp.concatenate([-sin, zh, zero], axis=1)
    sb = jnp.concatenate([zh, sin, zero], axis=1)
    return c, sa, sb


def _layer(x2, c_pad, bsz, seq, w_ada, b_ada, norm1_g, norm2_g, w_in, b_forget, nsa_q_norm, nsa_k_norm,
           fox_q_norm, fox_k_norm, cmp_pos_k, cmp_pos_v, w_cmp_k1, w_cmp_k2, w_cmp_v1, w_cmp_v2,
           w_up_nsa, w_up_fox, w_out, w_ff1, w_ff2):
    d = x2.shape[1]
    hd = HEAD_DIM
    tm = 1024
    per_m = seq // tm

    mod = _ada(c_pad, w_ada, b_ada.reshape(1, -1))[:bsz]
    shift1, scale1, gate1, shift2, scale2, gate2 = [t.reshape(bsz, 1, d) for t in jnp.split(mod, 6, axis=-1)]

    h = _normmod(x2, norm1_g.reshape(1, d), scale1, shift1, seq)

    o_q = 0
    o_kv = o_q + NSA_HEADS * hd
    o_gate = o_kv + 3 * 2 * NSA_GROUPS * hd
    o_fox = o_gate + 3 * NSA_HEADS
    o_f = o_fox + 3 * FOX_HEADS * hd
    o_merge = o_f + FOX_HEADS
    gw = NSA_GROUPS * hd

    w_t = w_in.T.astype(BF16)
    fw = FOX_HEADS * hd

    pos = np.arange(seq)
    c_t, sa_t, sb_t = _rope_tables(pos)
    tab_specs = [pl.BlockSpec((tm, hd), lambda j, i: (i % per_m, 0))] * 3
    gspec = pl.BlockSpec((1, hd), lambda j, i: (0, 0))

    q_scale = ATTN_SCALE * LOG2E
    q_n = _matmul(h, w_t, _ep_headnorm_rope, BF16, rows=(o_q, 1024, 2),
                  extras=(q_scale * nsa_q_norm.reshape(1, hd), c_t, sa_t, sb_t), extra_specs=[gspec] + tab_specs,
                  name="proj_nsa_q")
    k_sw = _matmul(h, w_t, _ep_headnorm_rope, BF16, tn=gw, rows=(o_kv + 2 * gw, 2 * gw, 2),
                   extras=(nsa_k_norm.reshape(1, hd), c_t, sa_t, sb_t), extra_specs=[gspec] + tab_specs,
                   name="proj_nsa_k")
    v_sw = _matmul(h, w_t, _ep_raw, BF16, tn=gw, rows=(o_kv + 3 * gw, 2 * gw, 2), name="proj_nsa_v")
    kv_c = _matmul(h, w_t, _ep_raw, F32, rows=(o_kv, 2 * gw, 1), name="proj_nsa_cmp")
    fq = _matmul(h, w_t, _ep_headnorm, BF16, rows=(o_fox, 1024, 2),
                 extras=(q_scale * fox_q_norm.reshape(1, hd),), extra_specs=[gspec], name="proj_fox_q")
    fk = _matmul(h, w_t, _ep_headnorm, BF16, rows=(o_fox + fw, 1024, 2),
                 extras=(fox_k_norm.reshape(1, hd),), extra_specs=[gspec], name="proj_fox_k")
    fv = _matmul(h, w_t, _ep_raw, BF16, rows=(o_fox + 2 * fw, 1024, 2), name="proj_fox_v")
    g_merge = _matmul(h, w_t, _ep_sigmoid, BF16, rows=(o_merge, 1024, 2 * d // 1024), name="proj_merge")

    per_group = NSA_REP * 3
    gate_blocks = [jnp.pad(w_t[o_gate + g * per_group:o_gate + (g + 1) * per_group],
                           ((0, LANES - per_group), (0, 0))) for g in range(NSA_GROUPS)]
    f_block = jnp.pad(w_t[o_f:o_merge], ((0, LANES - FOX_HEADS), (0, 0)))
    w_small = jnp.concatenate(gate_blocks + [f_block], axis=0)
    small = _matmul(h, w_small, _ep_raw, F32, tn=w_small.shape[0], rows=(0, 0, 1), name="proj_small")

    n_rows = seq // CMP_STRIDE
    def blocks_view(cols):
        t = kv_c[:, cols].reshape(bsz, seq, NSA_GROUPS, hd).transpose(0, 2, 1, 3)
        return t.reshape(bsz, NSA_GROUPS, n_rows, CMP_STRIDE * hd)
    end_pos = np.arange(n_rows) * CMP_STRIDE + CMP_BLOCK - 1
    ce, sae, sbe = _rope_tables(end_pos)
    k_cmp, v_cmp = _compress(
        blocks_view(slice(0, gw)), blocks_view(slice(gw, 2 * gw)),
        cmp_pos_k.reshape(2, CMP_STRIDE * hd), cmp_pos_v.reshape(2, CMP_STRIDE * hd),
        w_cmp_k1.astype(BF16), w_cmp_k2.astype(BF16), w_cmp_v1.astype(BF16), w_cmp_v2.astype(BF16),
        nsa_k_norm.reshape(1, hd), ce, sae, sbe)

    n_slc = seq // SEL_BLOCK
    ci = np.arange(LANES)[:, None] * CMP_STRIDE
    sj = np.arange(LANES)[None, :] * SEL_BLOCK
    overlap = ((ci < sj + SEL_BLOCK) & (ci + CMP_BLOCK > sj) & (np.arange(LANES)[None, :] < n_slc)
               & (np.arange(LANES)[:, None] < n_rows - 1)).astype(np.float32)
    o_c, sel = _cmp_attention(q_n, k_cmp, v_cmp, jnp.asarray(overlap.T), seq)

    expand = (np.arange(LANES)[:, None] == (np.arange(seq)[None, :] >> SEL_SHIFT)).astype(np.float32)
    o_nsa = _nsa_attention(q_n, k_sw, v_sw, sel, jnp.asarray(expand, dtype=BF16), o_c, small, seq)

    f_bias = jnp.pad(b_forget, (0, LANES - FOX_HEADS)).reshape(1, LANES)
    cum = _cum_forget(small, f_bias, seq, NSA_GROUPS)
    ck = cum[:, :FOX_HEADS].reshape(bsz, seq, FOX_HEADS).transpose(0, 2, 1).reshape(bsz * FOX_HEADS, 1, seq)
    o_fox, up_nsa_b, up_fox_b, w_out_b, w_ff1_b = _fox_attention(
        fq, fk, fv, ck, seq, casts=(w_up_nsa, w_up_fox, w_out, w_ff1))

    y = _merge(o_nsa, up_nsa_b, o_fox, up_fox_b, g_merge)
    tn_out = 512
    res_specs = [pl.BlockSpec((tm, tn_out), lambda j, i: (i, j)),
                 pl.BlockSpec((None, 1, tn_out), lambda j, i: (i // per_m, 0, j))]
    x_mid = _matmul(y, w_out_b, _ep_residual, F32, extras=(x2, gate1), extra_specs=res_specs,
                    tn=tn_out, name="out_proj")

    h2 = _normmod(x_mid, norm2_g.reshape(1, d), scale2, shift2, seq)
    hid, w_ff2_b = _matmul(h2, w_ff1_b, _ep_relu2, BF16, casts=(w_ff2,), name="ff1")
    return _matmul_k_residual(hid, w_ff2_b, x_mid, gate2, seq, name="ff2")


def kernel(x, c, w_ada, b_ada, norm1_g, norm2_g, w_in, b_forget, nsa_q_norm, nsa_k_norm, fox_q_norm, fox_k_norm, cmp_pos_k, cmp_pos_v, w_cmp_k1, w_cmp_k2, w_cmp_v1, w_cmp_v2, w_up_nsa, w_up_fox, w_out, w_ff1, w_ff2):
    bsz, seq, d = x.shape
    x2 = x.reshape(bsz * seq, d)
    c_pad = jnp.pad(c, ((0, 8 - bsz), (0, 0)))
    params = (w_ada, b_ada, norm1_g, norm2_g, w_in, b_forget, nsa_q_norm, nsa_k_norm, fox_q_norm, fox_k_norm,
              cmp_pos_k, cmp_pos_v, w_cmp_k1, w_cmp_k2, w_cmp_v1, w_cmp_v2, w_up_nsa, w_up_fox, w_out,
              w_ff1, w_ff2)
    for layer in range(w_ada.shape[0]):
        x2 = _layer(x2, c_pad, bsz, seq, *[p[layer] for p in params])
    return x2.reshape(bsz, seq, d)
```

```python
import functools

import numpy as np
import jax
import jax.numpy as jnp
from jax import lax
from jax.experimental import pallas as pl
from jax.experimental.pallas import tpu as pltpu

F32 = jnp.float32
BF16 = jnp.bfloat16

HEAD_DIM = 128
NSA_HEADS = 16
NSA_GROUPS = 4
NSA_REP = NSA_HEADS // NSA_GROUPS
FOX_HEADS = 16
CMP_BLOCK = 32
CMP_STRIDE = 16
SEL_BLOCK = 64
SEL_SHIFT = SEL_BLOCK.bit_length() - 1
SEL_TOPN = 16
WINDOW = 512
ROPE_THETA = 500000.0
ROT_DIM = HEAD_DIM // 4
RMS_EPS = 1e-6
ATTN_SCALE = HEAD_DIM ** -0.5

LOG2E = float(np.log2(np.e))
ROPE_HALF = ROT_DIM // 2
MXU_COLS = 256


def _head_perm(x, axis):
    mid = HEAD_DIM // 2
    cuts = [(0, ROPE_HALF), (mid, mid + ROPE_HALF), (ROT_DIM, mid), (ROPE_HALF, ROT_DIM), (mid + ROPE_HALF, HEAD_DIM)]
    return jnp.concatenate([lax.slice_in_dim(x, a, b, axis=axis) for a, b in cuts], axis=axis)
NEG = -1e30
BIG = 1e30
LANES = 128
VMEM_LIMIT = 56 * 1024 * 1024


def _cparams(*sem):
    return pltpu.CompilerParams(dimension_semantics=sem, vmem_limit_bytes=VMEM_LIMIT)


def _dot(a, b):
    return jnp.dot(a, b, preferred_element_type=F32)


def _dot_nt(a, b):
    return lax.dot_general(a, b, (((1,), (1,)), ((), ())), preferred_element_type=F32)


def _ada_kernel(c_ref, w_ref, b_ref, o_ref):
    c = c_ref[...]
    s = (c * jax.nn.sigmoid(c)).astype(BF16)
    o_ref[...] = _dot(s, w_ref[...].astype(BF16)) + b_ref[...]


def _ada(c_pad, w, b):
    rows, d = c_pad.shape
    n = w.shape[1]
    tn = 1024
    return pl.pallas_call(
        _ada_kernel,
        grid=(n // tn,),
        in_specs=[pl.BlockSpec((rows, d), lambda j: (0, 0)),
                  pl.BlockSpec((d, tn), lambda j: (0, j)),
                  pl.BlockSpec((1, tn), lambda j: (0, j))],
        out_specs=pl.BlockSpec((rows, tn), lambda j: (0, j)),
        out_shape=jax.ShapeDtypeStruct((rows, n), F32),
        compiler_params=_cparams("arbitrary"),
        name="ada",
    )(c_pad, w, b)


def _normmod_kernel(x_ref, g_ref, sc_ref, sh_ref, o_ref):
    x = x_ref[...]
    ms = jnp.mean(x * x, axis=-1, keepdims=True)
    y = x * lax.rsqrt(ms + RMS_EPS) * g_ref[...]
    o_ref[...] = (y * (1.0 + sc_ref[...]) + sh_ref[...]).astype(o_ref.dtype)


def _normmod(x2, g, scale, shift, seq):
    m, d = x2.shape
    tr = 512
    per = seq // tr
    return pl.pallas_call(
        _normmod_kernel,
        grid=(m // tr,),
        in_specs=[pl.BlockSpec((tr, d), lambda i: (i, 0)),
                  pl.BlockSpec((1, d), lambda i: (0, 0)),
                  pl.BlockSpec((None, 1, d), lambda i: (i // per, 0, 0)),
                  pl.BlockSpec((None, 1, d), lambda i: (i // per, 0, 0))],
        out_specs=pl.BlockSpec((tr, d), lambda i: (i, 0)),
        out_shape=jax.ShapeDtypeStruct((m, d), BF16),
        compiler_params=_cparams("arbitrary"),
        name="normmod",
    )(x2, g, scale, shift)


def _ep_raw(acc, o_ref):
    o_ref[...] = acc.astype(o_ref.dtype)


def _ep_sigmoid(acc, o_ref):
    o_ref[...] = jax.nn.sigmoid(acc).astype(o_ref.dtype)


def _ep_relu2(acc, o_ref):
    r = jnp.maximum(acc, 0.0)
    o_ref[...] = (r * r).astype(o_ref.dtype)


def _ep_residual(acc, o_ref, x_ref, g_ref):
    o_ref[...] = x_ref[...] + g_ref[...] * acc


def _head_norm(a, g):
    ms = jnp.mean(a * a, axis=-1, keepdims=True)
    return a * lax.rsqrt(ms + RMS_EPS) * g


def _rope(a, c, s):
    return a * c + pltpu.roll(a, HEAD_DIM // 2, 1) * s


def _chunk_head_norm(acc, g_ref, ones_ref):
    ss = _dot((acc * acc).astype(BF16), ones_ref[...])
    return acc * lax.rsqrt(ss * (1.0 / HEAD_DIM) + RMS_EPS) * g_ref[...]


def _ep_headnorm(acc, o_ref, g_ref, ones_ref):
    for c0 in range(0, acc.shape[1], MXU_COLS):
        y = _chunk_head_norm(acc[:, c0:c0 + MXU_COLS], g_ref, ones_ref)
        o_ref[:, c0:c0 + MXU_COLS] = y.astype(o_ref.dtype)


def _ep_headnorm_rope(acc, o_ref, g_ref, ones_ref, c_ref, s_ref):
    c, s = c_ref[...], s_ref[...]
    for c0 in range(0, acc.shape[1], MXU_COLS):
        y = _chunk_head_norm(acc[:, c0:c0 + MXU_COLS], g_ref, ones_ref)
        for h0 in range(0, MXU_COLS, HEAD_DIM):
            dst = slice(c0 + h0, c0 + h0 + HEAD_DIM)
            o_ref[:, dst] = _rope(y[:, h0:h0 + HEAD_DIM], c, s).astype(o_ref.dtype)


def _cast_blocks(refs_in, refs_out):
    for src, dst in zip(refs_in, refs_out):
        dst[...] = src[...].astype(dst.dtype)


def _cast_specs(casts, steps, index_of_step):
    specs, shapes = [], []
    for w in casts:
        rows, cols = w.shape
        assert rows % (steps * 16) == 0
        specs.append(pl.BlockSpec((rows // steps, cols), lambda *g: (index_of_step(*g), 0)))
        shapes.append(jax.ShapeDtypeStruct((rows, cols), BF16))
    return specs, shapes


def _mm_kernel(a_ref, b_ref, *rest, epilogue, n_extra, n_cast, transposed):
    extras = rest[:n_extra]
    cast_in = rest[n_extra:n_extra + n_cast]
    o_ref = rest[n_extra + n_cast]
    cast_out = rest[n_extra + n_cast + 1:]
    acc = _dot_nt(a_ref[...], b_ref[...]) if transposed else _dot(a_ref[...], b_ref[...])
    epilogue(acc, o_ref, *extras)
    _cast_blocks(cast_in, cast_out)


def _matmul(a, b, epilogue, out_dtype, extras=(), extra_specs=(), tm=1024, tn=1024, name="mm",
            casts=(), rows=None):
    m, k = a.shape
    if rows is None:
        n = b.shape[1]
        tn = min(tn, n)
        b_spec = pl.BlockSpec((k, tn), lambda j, i: (0, j))
    else:
        first, stride, count = rows
        assert first % 16 == 0 and stride % 16 == 0
        n = count * tn
        b_spec = pl.BlockSpec((pl.Element(tn), pl.Element(k)),
                              lambda j, i: (pl.multiple_of(first + j * stride, 16), 0))
    grid = (n // tn, m // tm)
    cast_specs, cast_shapes = _cast_specs(casts, grid[0] * grid[1], lambda j, i: j * grid[1] + i)
    out = pl.pallas_call(
        functools.partial(_mm_kernel, epilogue=epilogue, n_extra=len(extras), n_cast=len(casts),
                          transposed=rows is not None),
        grid=grid,
        in_specs=[pl.BlockSpec((tm, k), lambda j, i: (i, 0)), b_spec] + list(extra_specs) + cast_specs,
        out_specs=[pl.BlockSpec((tm, tn), lambda j, i: (i, j))] + cast_specs,
        out_shape=[jax.ShapeDtypeStruct((m, n), out_dtype)] + cast_shapes,
        compiler_params=_cparams("arbitrary", "arbitrary"),
        name=name,
    )(a, b, *extras, *casts)
    return out if casts else out[0]


def _mmk_kernel(a_ref, b_ref, x_ref, g_ref, o_ref, acc_ref, *, nk):
    kk = pl.program_id(2)

    @pl.when(kk == 0)
    def _():
        acc_ref[...] = jnp.zeros_like(acc_ref)

    acc_ref[...] += _dot(a_ref[...], b_ref[...])

    @pl.when(kk == nk - 1)
    def _():
        o_ref[...] = x_ref[...] + g_ref[...] * acc_ref[...]


def _matmul_k_residual(a, b, x, gate, seq, tm=1024, tn=1024, tk=2048, name="mmk"):
    m, k = a.shape
    n = b.shape[1]
    per = seq // tm
    nk = k // tk
    return pl.pallas_call(
        functools.partial(_mmk_kernel, nk=nk),
        grid=(m // tm, n // tn, nk),
        in_specs=[pl.BlockSpec((tm, tk), lambda i, j, kk: (i, kk)),
                  pl.BlockSpec((tk, tn), lambda i, j, kk: (kk, j)),
                  pl.BlockSpec((tm, tn), lambda i, j, kk: (i, j)),
                  pl.BlockSpec((None, 1, tn), lambda i, j, kk: (i // per, 0, j))],
        out_specs=pl.BlockSpec((tm, tn), lambda i, j, kk: (i, j)),
        out_shape=jax.ShapeDtypeStruct((m, n), F32),
        scratch_shapes=[pltpu.VMEM((tm, tn), F32)],
        compiler_params=_cparams("arbitrary", "arbitrary", "arbitrary"),
        name=name,
    )(a, b, x, gate)


def _compress_kernel(xk_ref, xv_ref, pek_ref, pev_ref, w1k_ref, w2k_ref, w1v_ref, w2v_ref,
                     g_ref, c_ref, s_ref, ko_ref, vo_ref):
    half = CMP_STRIDE * HEAD_DIM
    n_rows = xk_ref.shape[0] // CMP_STRIDE

    def comp(x_ref, pe_ref, w1_ref, w2_ref):
        x = jnp.concatenate([x_ref[pl.ds(l, n_rows, stride=CMP_STRIDE), :] for l in range(CMP_STRIDE)], axis=1)
        lo = (x + pe_ref[0:1, :]).astype(BF16)
        hi = (x + pe_ref[1:2, :]).astype(BF16)
        p = _dot(lo, w1_ref[0:half, :])
        q = _dot(hi, w1_ref[half:2 * half, :])
        h = p + pltpu.roll(q, q.shape[0] - 1, 0)
        h = jax.nn.gelu(h, approximate=True)
        return _dot(h.astype(BF16), w2_ref[...])

    kc = comp(xk_ref, pek_ref, w1k_ref, w2k_ref)
    kc = _rope(_head_norm(kc, g_ref[...]), c_ref[...], s_ref[...])
    ko_ref[...] = kc.astype(ko_ref.dtype)
    vo_ref[...] = comp(xv_ref, pev_ref, w1v_ref, w2v_ref).astype(vo_ref.dtype)


def _compress(kv_c, seq, pek, pev, w1k, w2k, w1v, w2v, g, c, s):
    b = kv_c.shape[0] // seq
    nr = seq // CMP_STRIDE
    kspec = pl.BlockSpec((seq, HEAD_DIM), lambda i, j: (i, j))
    vspec = pl.BlockSpec((seq, HEAD_DIM), lambda i, j: (i, NSA_GROUPS + j))
    full = lambda arr: pl.BlockSpec(arr.shape, lambda i, j: (0,) * arr.ndim)
    ospec = pl.BlockSpec((None, None, nr, HEAD_DIM), lambda i, j: (i, j, 0, 0))
    oshape = jax.ShapeDtypeStruct((b, NSA_GROUPS, nr, HEAD_DIM), BF16)
    return pl.pallas_call(
        _compress_kernel,
        grid=(b, NSA_GROUPS),
        in_specs=[kspec, vspec, full(pek), full(pev), full(w1k), full(w2k), full(w1v), full(w2v),
                  full(g), full(c), full(s)],
        out_specs=[ospec, ospec],
        out_shape=[oshape, oshape],
        compiler_params=_cparams("arbitrary", "arbitrary"),
        name="compress",
    )(kv_c, kv_c, pek, pev, w1k, w2k, w1v, w2v, g, c, s)


def _cmp_kernel(q_ref, k_ref, v_ref, ovt_ref, oc_ref, sel_ref, *, tt, n_slc):
    t0 = pl.program_id(2) * tt
    t = t0 + lax.broadcasted_iota(jnp.int32, (tt, LANES), 0)
    n = lax.broadcasted_iota(jnp.int32, (tt, LANES), 1)
    valid = n * CMP_STRIDE + (CMP_BLOCK - 1) <= t
    k = k_ref[...]
    v = v_ref[...]
    psum = jnp.zeros((tt, LANES), F32)
    for r in range(NSA_REP):
        sl = slice(r * HEAD_DIM, (r + 1) * HEAD_DIM)
        s = jnp.where(valid, _dot_nt(q_ref[:, sl], k), NEG)
        m = jnp.max(s, axis=-1, keepdims=True)
        p = jnp.where(valid, jnp.exp2(s - m), 0.0)
        d = jnp.sum(p, axis=-1, keepdims=True)
        p = p / jnp.where(d > 0, d, 1.0)
        psum = psum + p
        oc_ref[:, sl] = _dot(p.astype(BF16), v)
    imp = lax.dot_general(ovt_ref[...], psum, (((1,), (1,)), ((), ())), preferred_element_type=F32,
                          precision=lax.Precision.HIGHEST)[0:n_slc, :]
    j = lax.broadcasted_iota(jnp.int32, (n_slc, tt), 0)
    cur = (t0 + lax.broadcasted_iota(jnp.int32, (n_slc, tt), 1)) >> SEL_SHIFT
    score = jnp.where(j <= cur, imp, -BIG)
    score = jnp.where(j == 0, BIG, jnp.where(j == cur, BIG, jnp.where(j == cur - 1, BIG, score)))
    cnt = jnp.zeros((n_slc, tt), F32)
    for jp in range(n_slc):
        row = score[jp:jp + 1, :]
        ge = jnp.where(row >= score, 1.0, 0.0)
        gt = jnp.where(row > score, 1.0, 0.0)
        cnt = cnt + jnp.where(j > jp, ge, gt)
    sel = jnp.where(cnt < SEL_TOPN, jnp.where(score > -0.5 * BIG, 1.0, 0.0), 0.0)
    sel = jnp.concatenate([sel, jnp.zeros((LANES - n_slc, tt), F32)], axis=0)
    sel_ref[...] = sel.T.astype(sel_ref.dtype)


def _cmp_attention(q, kc, vc, overlap, seq):
    m, _ = q.shape
    b = m // seq
    tt = 512
    per = seq // tt
    n_cmp = (seq - CMP_BLOCK) // CMP_STRIDE + 1
    assert n_cmp <= LANES and n_cmp * CMP_STRIDE + CMP_BLOCK - 1 >= seq
    kspec = pl.BlockSpec((None, None, LANES, HEAD_DIM), lambda i, g, j: (i, g, 0, 0))
    return pl.pallas_call(
        functools.partial(_cmp_kernel, tt=tt, n_slc=seq // SEL_BLOCK),
        grid=(b, NSA_GROUPS, per),
        in_specs=[pl.BlockSpec((tt, NSA_REP * HEAD_DIM), lambda i, g, j: (i * per + j, g)),
                  kspec, kspec,
                  pl.BlockSpec((LANES, LANES), lambda i, g, j: (0, 0))],
        out_specs=[pl.BlockSpec((tt, NSA_REP * HEAD_DIM), lambda i, g, j: (i * per + j, g)),
                   pl.BlockSpec((None, None, tt, LANES), lambda i, g, j: (i, g, j, 0))],
        out_shape=[jax.ShapeDtypeStruct((m, NSA_HEADS * HEAD_DIM), F32),
                   jax.ShapeDtypeStruct((b, NSA_GROUPS, seq, LANES), BF16)],
        compiler_params=_cparams("arbitrary", "arbitrary", "arbitrary"),
        name="cmp_attn",
    )(q, kc, vc, overlap)


def _softmax_pv(t, v):
    m = jnp.max(t, axis=-1, keepdims=True)
    p = jnp.exp2(t - m)
    l = jnp.sum(p, axis=-1, keepdims=True)
    return _dot(p.astype(BF16), v) / l


def _tri_bias(n, lower):
    row = lax.broadcasted_iota(jnp.int32, (n, n), 0)
    col = lax.broadcasted_iota(jnp.int32, (n, n), 1)
    return jnp.where(col <= row, 0.0, NEG) if lower else jnp.where(col > row, 0.0, NEG)


def _nsa_kernel(q_ref, ks_ref, vs_ref, kw_ref, vw_ref, sel_ref, ex_ref, oc_ref, gl_ref, o_ref,
                t_sc, p_sc, l_sc, bs_sc, bw_sc, os_sc, *, tq):
    seq = q_ref.shape[0]
    live = pl.program_id(0) >= 0
    n_back = WINDOW // tq
    w_cols = (n_back + 1) * tq
    bw_sc[:, 0:tq] = _tri_bias(tq, False)
    bw_sc[:, tq:n_back * tq] = jnp.zeros((tq, (n_back - 1) * tq), F32)
    bw_sc[:, n_back * tq:w_cols] = _tri_bias(tq, True)

    jobs = []
    for qt in range(seq // tq):
        lo, hi = qt * tq, (qt + 1) * tq
        jobs.append(("sel", lo, hi, 0))
        jobs.append(("win", lo, hi, max(lo - WINDOW, 0)))

    def head_rows(r):
        return slice(r * tq, (r + 1) * tq)

    def stage_scores(j):
        kind, lo, hi, klo = jobs[j]
        slot = j % 2
        n = hi - klo
        q4 = jnp.concatenate([q_ref[lo:hi, r * HEAD_DIM:(r + 1) * HEAD_DIM] for r in range(NSA_REP)], axis=0)
        k_ref = ks_ref if kind == "sel" else kw_ref
        t_sc[slot, :, 0:n] = _dot_nt(q4, k_ref[klo:hi, :])
        if kind == "sel":
            picked = _dot(sel_ref[lo:hi, :], ex_ref[:, 0:hi])
            bias = jnp.where(picked > 0.5, 0.0, NEG)
            if lo > 0:
                bs_sc[slot, :, 0:lo] = bias[:, 0:lo]
            bs_sc[slot, :, lo:hi] = bias[:, lo:hi] + _tri_bias(tq, True)

    def stage_softmax(j):
        kind, lo, hi, klo = jobs[j]
        slot = j % 2
        n = hi - klo
        for r in range(NSA_REP):
            bias = bs_sc[slot, :, 0:n] if kind == "sel" else bw_sc[:, w_cols - n:w_cols]
            t = t_sc[slot, head_rows(r), 0:n] + bias
            p = jnp.exp2(t - jnp.max(t, axis=-1, keepdims=True))
            l_sc[slot, head_rows(r), :] = jnp.sum(p, axis=-1, keepdims=True)
            p_sc[slot, head_rows(r), 0:n] = p.astype(BF16)

    def stage_output(j):
        kind, lo, hi, klo = jobs[j]
        slot = j % 2
        n = hi - klo
        v_ref = vs_ref if kind == "sel" else vw_ref
        o4 = _dot(p_sc[slot, :, 0:n], v_ref[klo:hi, :]) / l_sc[slot]
        if kind == "sel":
            os_sc[...] = o4
            return
        gate = jax.nn.sigmoid(gl_ref[lo:hi, :])
        for r in range(NSA_REP):
            sl = slice(r * HEAD_DIM, (r + 1) * HEAD_DIM)
            o = (gate[:, 3 * r:3 * r + 1] * oc_ref[lo:hi, sl]
                 + gate[:, 3 * r + 1:3 * r + 2] * os_sc[head_rows(r), :]
                 + gate[:, 3 * r + 2:3 * r + 3] * o4[head_rows(r), :])
            o_ref[lo:hi, sl] = o.astype(o_ref.dtype)

    for i in range(len(jobs) + 2):
        @pl.when(live)
        def _region():
            if i < len(jobs):
                stage_scores(i)
            if 0 <= i - 1 < len(jobs):
                stage_softmax(i - 1)
            if 0 <= i - 2 < len(jobs):
                stage_output(i - 2)


def _nsa_attention(q, kk, vv, sel, expand, oc, gl, seq):
    m = q.shape[0]
    b = m // seq
    width = NSA_REP * HEAD_DIM
    qspec = pl.BlockSpec((seq, width), lambda i, g: (i, g))
    ks = pl.BlockSpec((seq, HEAD_DIM), lambda i, g: (i, g))
    kw = pl.BlockSpec((seq, HEAD_DIM), lambda i, g: (i, NSA_GROUPS + g))
    tq = 128
    rows = NSA_REP * tq
    return pl.pallas_call(
        functools.partial(_nsa_kernel, tq=tq),
        grid=(b, NSA_GROUPS),
        scratch_shapes=[pltpu.VMEM((2, rows, seq), F32), pltpu.VMEM((2, rows, seq), BF16),
                        pltpu.VMEM((2, rows, 1), F32), pltpu.VMEM((2, tq, seq), F32),
                        pltpu.VMEM((tq, WINDOW + tq), F32), pltpu.VMEM((rows, HEAD_DIM), F32)],
        in_specs=[qspec, ks, ks, kw, kw,
                  pl.BlockSpec((None, None, seq, LANES), lambda i, g: (i, g, 0, 0)),
                  pl.BlockSpec((LANES, seq), lambda i, g: (0, 0)),
                  qspec,
                  pl.BlockSpec((seq, LANES), lambda i, g: (i, g))],
        out_specs=qspec,
        out_shape=jax.ShapeDtypeStruct((m, NSA_HEADS * HEAD_DIM), BF16),
        compiler_params=_cparams("arbitrary", "arbitrary"),
        name="nsa_attn",
    )(q, kk, vv, kk, vv, sel, expand, oc, gl)


def _cum_kernel(f_ref, b_ref, o_ref):
    x = f_ref[...] + b_ref[...]
    ls = jnp.minimum(x, 0.0) - jnp.log1p(jnp.exp(-jnp.abs(x)))
    row = lax.broadcasted_iota(jnp.int32, ls.shape, 0)
    d = 1
    while d < ls.shape[0]:
        ls = ls + jnp.where(row >= d, pltpu.roll(ls, d, 0), 0.0)
        d *= 2
    o_ref[...] = ls


def _cum_forget(fl, bias, seq, col_block):
    m = fl.shape[0]
    return pl.pallas_call(
        _cum_kernel,
        grid=(m // seq,),
        in_specs=[pl.BlockSpec((seq, LANES), lambda i: (i, col_block)),
                  pl.BlockSpec((1, LANES), lambda i: (0, 0))],
        out_specs=pl.BlockSpec((seq, LANES), lambda i: (i, 0)),
        out_shape=jax.ShapeDtypeStruct((m, LANES), F32),
        compiler_params=_cparams("arbitrary"),
        name="cum_forget",
    )(fl, bias)


def _fox_kernel(q_ref, k_ref, v_ref, ck_ref, *rest, tq):
    n_cast = (len(rest) - 1) // 2
    o_ref = rest[n_cast]
    _cast_blocks(rest[:n_cast], rest[n_cast + 1:])
    seq = q_ref.shape[0]
    causal_b = _tri_bias(tq, True)
    ck = ck_ref[...] * LOG2E
    for qt in range(seq // tq):
        lo, hi = qt * tq, (qt + 1) * tq
        t = _dot_nt(q_ref[lo:hi, :], k_ref[0:hi, :]) - ck[:, 0:hi]
        diag = t[:, lo:hi] + causal_b
        t = diag if qt == 0 else jnp.concatenate([t[:, 0:lo], diag], axis=1)
        o_ref[lo:hi, :] = _softmax_pv(t, v_ref[0:hi, :]).astype(o_ref.dtype)


def _fox_attention(q, k, v, ck, seq, casts=()):
    m = q.shape[0]
    b = m // seq
    blk = pl.BlockSpec((seq, HEAD_DIM), lambda i, h: (i, h))
    cast_specs, cast_shapes = _cast_specs(casts, b * FOX_HEADS, lambda i, h: i * FOX_HEADS + h)
    out = pl.pallas_call(
        functools.partial(_fox_kernel, tq=256),
        grid=(b, FOX_HEADS),
        in_specs=[blk, blk, blk,
                  pl.BlockSpec((None, 1, seq), lambda i, h: (i * FOX_HEADS + h, 0, 0))] + cast_specs,
        out_specs=[blk] + cast_specs,
        out_shape=[jax.ShapeDtypeStruct((m, FOX_HEADS * HEAD_DIM), BF16)] + cast_shapes,
        compiler_params=_cparams("arbitrary", "arbitrary"),
        name="fox_attn",
    )(q, k, v, ck, *casts)
    return out if casts else out[0]


def _merge_kernel(a1_ref, w1_ref, a2_ref, w2_ref, g1_ref, g2_ref, *rest):
    n_cast = (len(rest) - 1) // 2
    o_ref = rest[n_cast]
    u1 = _dot(a1_ref[...], w1_ref[...])
    u2 = _dot(a2_ref[...], w2_ref[...])
    o_ref[...] = (g1_ref[...].astype(F32) * u1 + g2_ref[...].astype(F32) * u2).astype(o_ref.dtype)
    _cast_blocks(rest[:n_cast], rest[n_cast + 1:])


def _merge(a1, w1, a2, w2, gates, casts=()):
    m, k = a1.shape
    n = w1.shape[1]
    tm, tn = 1024, 512
    nb = n // tn
    mb = m // tm
    aspec = pl.BlockSpec((tm, k), lambda j, i: (i, 0))
    wspec = pl.BlockSpec((k, tn), lambda j, i: (0, j))
    cast_specs, cast_shapes = _cast_specs(casts, nb * mb, lambda j, i: j * mb + i)
    out = pl.pallas_call(
        _merge_kernel,
        grid=(nb, mb),
        in_specs=[aspec, wspec, aspec, wspec,
                  pl.BlockSpec((tm, tn), lambda j, i: (i, j)),
                  pl.BlockSpec((tm, tn), lambda j, i: (i, nb + j))] + cast_specs,
        out_specs=[pl.BlockSpec((tm, tn), lambda j, i: (i, j))] + cast_specs,
        out_shape=[jax.ShapeDtypeStruct((m, n), BF16)] + cast_shapes,
        compiler_params=_cparams("arbitrary", "arbitrary"),
        name="merge",
    )(a1, w1, a2, w2, gates, gates, *casts)
    return out if casts else out[0]


def _rope_tables(pos):
    inv = ROPE_THETA ** (-jnp.arange(ROPE_HALF, dtype=F32) / ROPE_HALF)
    ang = jnp.asarray(pos, dtype=F32)[:, None] * inv[None, :]
    cos, sin = jnp.cos(ang), jnp.sin(ang)
    n = ang.shape[0]
    gap = HEAD_DIM // 2 - ROPE_HALF
    c = jnp.concatenate([cos, jnp.ones((n, gap), F32), cos, jnp.ones((n, gap), F32)], axis=1)
    s = jnp.concatenate([-sin, jnp.zeros((n, gap), F32), sin, jnp.zeros((n, gap), F32)], axis=1)
    return c, s


def _layer(x2, c_pad, bsz, seq, w_ada, b_ada, norm1_g, norm2_g, w_in, b_forget, nsa_q_norm, nsa_k_norm,
           fox_q_norm, fox_k_norm, cmp_pos_k, cmp_pos_v, w_cmp_k1, w_cmp_k2, w_cmp_v1, w_cmp_v2,
           w_up_nsa, w_up_fox, w_out, w_ff1, w_ff2):
    d = x2.shape[1]
    hd = HEAD_DIM
    tm = 1024
    per_m = seq // tm

    mod = _ada(c_pad, w_ada, b_ada.reshape(1, -1))[:bsz]
    shift1, scale1, gate1, shift2, scale2, gate2 = [t.reshape(bsz, 1, d) for t in jnp.split(mod, 6, axis=-1)]

    h = _normmod(x2, norm1_g.reshape(1, d), scale1, shift1, seq)

    o_q = 0
    o_kv = o_q + NSA_HEADS * hd
    o_gate = o_kv + 3 * 2 * NSA_GROUPS * hd
    o_fox = o_gate + 3 * NSA_HEADS
    o_f = o_fox + 3 * FOX_HEADS * hd
    o_merge = o_f + FOX_HEADS
    gw = NSA_GROUPS * hd

    w_t = w_in.T.astype(BF16)
    fw = FOX_HEADS * hd

    pos = np.arange(seq)
    c_t, s_t = _rope_tables(pos)
    tab_specs = [pl.BlockSpec((tm, hd), lambda j, i: (i % per_m, 0))] * 2
    gspec = pl.BlockSpec((1, MXU_COLS), lambda j, i: (0, 0))
    ones_spec = pl.BlockSpec((MXU_COLS, MXU_COLS), lambda j, i: (0, 0))
    head_ones = jnp.asarray(np.kron(np.eye(MXU_COLS // hd), np.ones((hd, hd))), dtype=BF16)

    def chunk_gain(g):
        return jnp.tile(g.reshape(1, hd), (1, MXU_COLS // hd))

    q_scale = ATTN_SCALE * LOG2E
    def permuted_heads(lo, n_heads):
        return _head_perm(w_t[lo:lo + n_heads * hd].reshape(n_heads, hd, d), 1).reshape(n_heads * hd, d)
    w_qk = jnp.concatenate([permuted_heads(o_q, NSA_HEADS), permuted_heads(o_kv + 2 * gw, NSA_GROUPS),
                            permuted_heads(o_kv + 4 * gw, NSA_GROUPS)], axis=0)
    g_q = chunk_gain(q_scale * _head_perm(nsa_q_norm, 0))
    g_k = chunk_gain(_head_perm(nsa_k_norm, 0))
    rope_specs = [gspec, ones_spec] + tab_specs
    q_n = _matmul(h, w_qk, _ep_headnorm_rope, BF16, rows=(0, 1024, 2),
                  extras=(g_q, head_ones, c_t, s_t), extra_specs=rope_specs, name="proj_nsa_q")
    k_sw = _matmul(h, w_qk, _ep_headnorm_rope, BF16, rows=(NSA_HEADS * hd, 0, 1),
                   extras=(g_k, head_ones, c_t, s_t), extra_specs=rope_specs, name="proj_nsa_k")
    v_sw = _matmul(h, w_t, _ep_raw, BF16, tn=gw, rows=(o_kv + 3 * gw, 2 * gw, 2), name="proj_nsa_v")
    kv_c = _matmul(h, w_t, _ep_raw, F32, rows=(o_kv, 2 * gw, 1), name="proj_nsa_cmp")
    fq = _matmul(h, w_t, _ep_headnorm, BF16, rows=(o_fox, 1024, 2),
                 extras=(chunk_gain(q_scale * fox_q_norm), head_ones), extra_specs=[gspec, ones_spec],
                 name="proj_fox_q")
    fk = _matmul(h, w_t, _ep_headnorm, BF16, rows=(o_fox + fw, 1024, 2),
                 extras=(chunk_gain(fox_k_norm), head_ones), extra_specs=[gspec, ones_spec], name="proj_fox_k")
    fv = _matmul(h, w_t, _ep_raw, BF16, rows=(o_fox + 2 * fw, 1024, 2), name="proj_fox_v")
    g_merge = _matmul(h, w_t, _ep_sigmoid, BF16, rows=(o_merge, 1024, 2 * d // 1024), name="proj_merge")

    per_group = NSA_REP * 3
    gate_blocks = [jnp.pad(w_t[o_gate + g * per_group:o_gate + (g + 1) * per_group],
                           ((0, LANES - per_group), (0, 0))) for g in range(NSA_GROUPS)]
    f_block = jnp.pad(w_t[o_f:o_merge], ((0, LANES - FOX_HEADS), (0, 0)))
    w_small = jnp.concatenate(gate_blocks + [f_block], axis=0)
    small = _matmul(h, w_small, _ep_raw, F32, tn=w_small.shape[0], rows=(0, 0, 1), name="proj_small")

    n_rows = seq // CMP_STRIDE
    end_pos = np.arange(n_rows) * CMP_STRIDE + CMP_BLOCK - 1
    ce, se = _rope_tables(end_pos)
    k_cmp, v_cmp = _compress(
        kv_c, seq, cmp_pos_k.reshape(2, CMP_STRIDE * hd), cmp_pos_v.reshape(2, CMP_STRIDE * hd),
        w_cmp_k1.astype(BF16), _head_perm(w_cmp_k2, 1).astype(BF16), w_cmp_v1.astype(BF16), w_cmp_v2.astype(BF16),
        _head_perm(nsa_k_norm, 0).reshape(1, hd), ce, se)

    n_slc = seq // SEL_BLOCK
    ci = np.arange(LANES)[:, None] * CMP_STRIDE
    sj = np.arange(LANES)[None, :] * SEL_BLOCK
    overlap = ((ci < sj + SEL_BLOCK) & (ci + CMP_BLOCK > sj) & (np.arange(LANES)[None, :] < n_slc)
               & (np.arange(LANES)[:, None] < n_rows - 1)).astype(np.float32)
    o_c, sel = _cmp_attention(q_n, k_cmp, v_cmp, jnp.asarray(overlap.T), seq)

    expand = (np.arange(LANES)[:, None] == (np.arange(seq)[None, :] >> SEL_SHIFT)).astype(np.float32)
    o_nsa = _nsa_attention(q_n, k_sw, v_sw, sel, jnp.asarray(expand, dtype=BF16), o_c, small, seq)

    f_bias = jnp.pad(b_forget, (0, LANES - FOX_HEADS)).reshape(1, LANES)
    cum = _cum_forget(small, f_bias, seq, NSA_GROUPS)
    ck = cum[:, :FOX_HEADS].reshape(bsz, seq, FOX_HEADS).transpose(0, 2, 1).reshape(bsz * FOX_HEADS, 1, seq)
    o_fox, up_nsa_b, up_fox_b, w_out_b, w_ff1_b = _fox_attention(
        fq, fk, fv, ck, seq, casts=(w_up_nsa, w_up_fox, w_out, w_ff1))

    y = _merge(o_nsa, up_nsa_b, o_fox, up_fox_b, g_merge)
    tn_out = 512
    res_specs = [pl.BlockSpec((tm, tn_out), lambda j, i: (i, j)),
                 pl.BlockSpec((None, 1, tn_out), lambda j, i: (i // per_m, 0, j))]
    x_mid = _matmul(y, w_out_b, _ep_residual, F32, extras=(x2, gate1), extra_specs=res_specs,
                    tn=tn_out, name="out_proj")

    h2 = _normmod(x_mid, norm2_g.reshape(1, d), scale2, shift2, seq)
    hid, w_ff2_b = _matmul(h2, w_ff1_b, _ep_relu2, BF16, casts=(w_ff2,), name="ff1")
    return _matmul_k_residual(hid, w_ff2_b, x_mid, gate2, seq, name="ff2")


def kernel(x, c, w_ada, b_ada, norm1_g, norm2_g, w_in, b_forget, nsa_q_norm, nsa_k_norm, fox_q_norm, fox_k_norm, cmp_pos_k, cmp_pos_v, w_cmp_k1, w_cmp_k2, w_cmp_v1, w_cmp_v2, w_up_nsa, w_up_fox, w_out, w_ff1, w_ff2):
    bsz, seq, d = x.shape
    x2 = x.reshape(bsz * seq, d)
    c_pad = jnp.pad(c, ((0, 8 - bsz), (0, 0)))
    params = (w_ada, b_ada, norm1_g, norm2_g, w_in, b_forget, nsa_q_norm, nsa_k_norm, fox_q_norm, fox_k_norm,
              cmp_pos_k, cmp_pos_v, w_cmp_k1, w_cmp_k2, w_cmp_v1, w_cmp_v2, w_up_nsa, w_up_fox, w_out,
              w_ff1, w_ff2)
    for layer in range(w_ada.shape[0]):
        x2 = _layer(x2, c_pad, bsz, seq, *[p[layer] for p in params])
    return x2.reshape(bsz, seq, d)
```

```python
import functools

import numpy as np
import jax
import jax.numpy as jnp
from jax import lax
from jax.experimental import pallas as pl
from jax.experimental.pallas import tpu as pltpu

F32 = jnp.float32
BF16 = jnp.bfloat16

HEAD_DIM = 128
NSA_HEADS = 16
NSA_GROUPS = 4
NSA_REP = NSA_HEADS // NSA_GROUPS
FOX_HEADS = 16
CMP_BLOCK = 32
CMP_STRIDE = 16
SEL_BLOCK = 64
SEL_SHIFT = SEL_BLOCK.bit_length() - 1
SEL_TOPN = 16
WINDOW = 512
ROPE_THETA = 500000.0
ROT_DIM = HEAD_DIM // 4
RMS_EPS = 1e-6
ATTN_SCALE = HEAD_DIM ** -0.5

LOG2E = float(np.log2(np.e))
ROPE_HALF = ROT_DIM // 2
MXU_COLS = 256


def _head_perm(x, axis):
    mid = HEAD_DIM // 2
    cuts = [(0, ROPE_HALF), (mid, mid + ROPE_HALF), (ROT_DIM, mid), (ROPE_HALF, ROT_DIM), (mid + ROPE_HALF, HEAD_DIM)]
    return jnp.concatenate([lax.slice_in_dim(x, a, b, axis=axis) for a, b in cuts], axis=axis)
NEG = -1e30
BIG = 1e30
LANES = 128
VMEM_LIMIT = 60 * 1024 * 1024


def _cparams(*sem):
    return pltpu.CompilerParams(dimension_semantics=sem, vmem_limit_bytes=VMEM_LIMIT)


def _dot(a, b):
    return jnp.dot(a, b, preferred_element_type=F32)


def _dot_nt(a, b):
    return lax.dot_general(a, b, (((1,), (1,)), ((), ())), preferred_element_type=F32)


def _ada_kernel(c_ref, w_ref, b_ref, o_ref):
    c = c_ref[...]
    s = (c * jax.nn.sigmoid(c)).astype(BF16)
    o_ref[...] = _dot(s, w_ref[...].astype(BF16)) + b_ref[...]


def _ada(c_pad, w, b):
    rows, d = c_pad.shape
    n = w.shape[1]
    tn = 1024
    return pl.pallas_call(
        _ada_kernel,
        grid=(n // tn,),
        in_specs=[pl.BlockSpec((rows, d), lambda j: (0, 0)),
                  pl.BlockSpec((d, tn), lambda j: (0, j)),
                  pl.BlockSpec((1, tn), lambda j: (0, j))],
        out_specs=pl.BlockSpec((rows, tn), lambda j: (0, j)),
        out_shape=jax.ShapeDtypeStruct((rows, n), F32),
        compiler_params=_cparams("arbitrary"),
        name="ada",
    )(c_pad, w, b)


def _normmod_kernel(x_ref, g_ref, sc_ref, sh_ref, o_ref):
    x = x_ref[...]
    ms = jnp.mean(x * x, axis=-1, keepdims=True)
    y = x * lax.rsqrt(ms + RMS_EPS) * g_ref[...]
    o_ref[...] = (y * (1.0 + sc_ref[...]) + sh_ref[...]).astype(o_ref.dtype)


def _normmod(x2, g, scale, shift, seq):
    m, d = x2.shape
    tr = 512
    per = seq // tr
    return pl.pallas_call(
        _normmod_kernel,
        grid=(m // tr,),
        in_specs=[pl.BlockSpec((tr, d), lambda i: (i, 0)),
                  pl.BlockSpec((1, d), lambda i: (0, 0)),
                  pl.BlockSpec((None, 1, d), lambda i: (i // per, 0, 0)),
                  pl.BlockSpec((None, 1, d), lambda i: (i // per, 0, 0))],
        out_specs=pl.BlockSpec((tr, d), lambda i: (i, 0)),
        out_shape=jax.ShapeDtypeStruct((m, d), BF16),
        compiler_params=_cparams("arbitrary"),
        name="normmod",
    )(x2, g, scale, shift)


def _ep_raw(acc, o_ref):
    o_ref[...] = acc.astype(o_ref.dtype)


def _ep_sigmoid(acc, o_ref):
    o_ref[...] = jax.nn.sigmoid(acc).astype(o_ref.dtype)


def _ep_relu2(acc, o_ref):
    r = jnp.maximum(acc, 0.0)
    o_ref[...] = (r * r).astype(o_ref.dtype)


def _ep_residual(acc, o_ref, x_ref, g_ref):
    o_ref[...] = x_ref[...] + g_ref[...] * acc


def _head_norm(a, g):
    ms = jnp.mean(a * a, axis=-1, keepdims=True)
    return a * lax.rsqrt(ms + RMS_EPS) * g


def _rope(a, c, s):
    return a * c + pltpu.roll(a, HEAD_DIM // 2, 1) * s


def _chunk_head_norm(acc, g_ref, ones_ref):
    ss = _dot((acc * acc).astype(BF16), ones_ref[...])
    return acc * lax.rsqrt(ss * (1.0 / HEAD_DIM) + RMS_EPS) * g_ref[...]


def _ep_headnorm(acc, o_ref, g_ref):
    g = g_ref[...]
    for h0 in range(0, acc.shape[1], HEAD_DIM):
        o_ref[:, h0:h0 + HEAD_DIM] = _head_norm(acc[:, h0:h0 + HEAD_DIM], g).astype(o_ref.dtype)


def _ep_headnorm_rope(acc, o_ref, g_ref, ones_ref, c_ref, s_ref):
    c, s = c_ref[...], s_ref[...]
    for c0 in range(0, acc.shape[1], MXU_COLS):
        y = _chunk_head_norm(acc[:, c0:c0 + MXU_COLS], g_ref, ones_ref)
        for h0 in range(0, MXU_COLS, HEAD_DIM):
            dst = slice(c0 + h0, c0 + h0 + HEAD_DIM)
            o_ref[:, dst] = _rope(y[:, h0:h0 + HEAD_DIM], c, s).astype(o_ref.dtype)


def _cast_blocks(refs_in, refs_out):
    for src, dst in zip(refs_in, refs_out):
        dst[...] = src[...].astype(dst.dtype)


def _cast_specs(casts, steps, index_of_step):
    specs, shapes = [], []
    for w in casts:
        rows, cols = w.shape
        assert rows % (steps * 16) == 0
        specs.append(pl.BlockSpec((rows // steps, cols), lambda *g: (index_of_step(*g), 0)))
        shapes.append(jax.ShapeDtypeStruct((rows, cols), BF16))
    return specs, shapes


def _mm_kernel(a_ref, b_ref, *rest, epilogue, n_extra, n_cast, transposed):
    extras = rest[:n_extra]
    cast_in = rest[n_extra:n_extra + n_cast]
    o_ref = rest[n_extra + n_cast]
    cast_out = rest[n_extra + n_cast + 1:]
    acc = _dot_nt(a_ref[...], b_ref[...]) if transposed else _dot(a_ref[...], b_ref[...])
    epilogue(acc, o_ref, *extras)
    _cast_blocks(cast_in, cast_out)


def _matmul(a, b, epilogue, out_dtype, extras=(), extra_specs=(), tm=1024, tn=1024, name="mm",
            casts=(), rows=None):
    m, k = a.shape
    if rows is None:
        n = b.shape[1]
        tn = min(tn, n)
        b_spec = pl.BlockSpec((k, tn), lambda j, i: (0, j))
    else:
        first, stride, count = rows
        assert first % 16 == 0 and stride % 16 == 0
        n = count * tn
        b_spec = pl.BlockSpec((pl.Element(tn), pl.Element(k)),
                              lambda j, i: (pl.multiple_of(first + j * stride, 16), 0))
    grid = (n // tn, m // tm)
    cast_specs, cast_shapes = _cast_specs(casts, grid[0] * grid[1], lambda j, i: j * grid[1] + i)
    out = pl.pallas_call(
        functools.partial(_mm_kernel, epilogue=epilogue, n_extra=len(extras), n_cast=len(casts),
                          transposed=rows is not None),
        grid=grid,
        in_specs=[pl.BlockSpec((tm, k), lambda j, i: (i, 0)), b_spec] + list(extra_specs) + cast_specs,
        out_specs=[pl.BlockSpec((tm, tn), lambda j, i: (i, j))] + cast_specs,
        out_shape=[jax.ShapeDtypeStruct((m, n), out_dtype)] + cast_shapes,
        compiler_params=_cparams("arbitrary", "arbitrary"),
        name=name,
    )(a, b, *extras, *casts)
    return out if casts else out[0]


def _mmk_kernel(a_ref, b_ref, x_ref, g_ref, o_ref):
    kk = pl.program_id(2)

    @pl.when(kk == 0)
    def _():
        o_ref[...] = x_ref[...] + g_ref[...] * _dot(a_ref[...], b_ref[...])

    @pl.when(kk > 0)
    def _():
        o_ref[...] += g_ref[...] * _dot(a_ref[...], b_ref[...])


def _matmul_k_residual(a, b, x, gate, seq, tm=1024, tn=1024, tk=4096, name="mmk"):
    m, k = a.shape
    n = b.shape[1]
    per = seq // tm
    nk = k // tk
    return pl.pallas_call(
        _mmk_kernel,
        grid=(m // tm, n // tn, nk),
        in_specs=[pl.BlockSpec((tm, tk), lambda i, j, kk: (i, kk)),
                  pl.BlockSpec((tk, tn), lambda i, j, kk: (kk, j)),
                  pl.BlockSpec((tm, tn), lambda i, j, kk: (i, j)),
                  pl.BlockSpec((None, 1, tn), lambda i, j, kk: (i // per, 0, j))],
        out_specs=pl.BlockSpec((tm, tn), lambda i, j, kk: (i, j)),
        out_shape=jax.ShapeDtypeStruct((m, n), F32),
        compiler_params=_cparams("arbitrary", "arbitrary", "arbitrary"),
        name=name,
    )(a, b, x, gate)


def _compress_kernel(xk_ref, xv_ref, pek_ref, pev_ref, w1k_ref, w2k_ref, w1v_ref, w2v_ref,
                     g_ref, c_ref, s_ref, ko_ref, vo_ref):
    half = CMP_STRIDE * HEAD_DIM
    n_rows = xk_ref.shape[0] // CMP_STRIDE

    def comp(x_ref, pe_ref, w1_ref, w2_ref):
        x = jnp.concatenate([x_ref[pl.ds(l, n_rows, stride=CMP_STRIDE), :] for l in range(CMP_STRIDE)], axis=1)
        lo = (x + pe_ref[0:1, :]).astype(BF16)
        hi = (x + pe_ref[1:2, :]).astype(BF16)
        p = _dot(lo, w1_ref[0:half, :])
        q = _dot(hi, w1_ref[half:2 * half, :])
        h = p + pltpu.roll(q, q.shape[0] - 1, 0)
        h = jax.nn.gelu(h, approximate=True)
        return _dot(h.astype(BF16), w2_ref[...])

    kc = comp(xk_ref, pek_ref, w1k_ref, w2k_ref)
    kc = _rope(_head_norm(kc, g_ref[...]), c_ref[...], s_ref[...])
    ko_ref[...] = kc.astype(ko_ref.dtype)
    vo_ref[...] = comp(xv_ref, pev_ref, w1v_ref, w2v_ref).astype(vo_ref.dtype)


def _compress(kv_c, seq, pek, pev, w1k, w2k, w1v, w2v, g, c, s):
    b = kv_c.shape[0] // seq
    nr = seq // CMP_STRIDE
    kspec = pl.BlockSpec((seq, HEAD_DIM), lambda i, j: (i, j))
    vspec = pl.BlockSpec((seq, HEAD_DIM), lambda i, j: (i, NSA_GROUPS + j))
    full = lambda arr: pl.BlockSpec(arr.shape, lambda i, j: (0,) * arr.ndim)
    ospec = pl.BlockSpec((None, None, nr, HEAD_DIM), lambda i, j: (i, j, 0, 0))
    oshape = jax.ShapeDtypeStruct((b, NSA_GROUPS, nr, HEAD_DIM), BF16)
    return pl.pallas_call(
        _compress_kernel,
        grid=(b, NSA_GROUPS),
        in_specs=[kspec, vspec, full(pek), full(pev), full(w1k), full(w2k), full(w1v), full(w2v),
                  full(g), full(c), full(s)],
        out_specs=[ospec, ospec],
        out_shape=[oshape, oshape],
        compiler_params=_cparams("arbitrary", "arbitrary"),
        name="compress",
    )(kv_c, kv_c, pek, pev, w1k, w2k, w1v, w2v, g, c, s)


def _cmp_kernel(q_ref, k_ref, v_ref, ovt_ref, oc_ref, sel_ref, *, tt, n_slc):
    t0 = pl.program_id(2) * tt
    t = t0 + lax.broadcasted_iota(jnp.int32, (tt, LANES), 0)
    n = lax.broadcasted_iota(jnp.int32, (tt, LANES), 1)
    valid = n * CMP_STRIDE + (CMP_BLOCK - 1) <= t
    k = k_ref[...]
    v = v_ref[...]
    psum = jnp.zeros((tt, LANES), F32)
    for r in range(NSA_REP):
        sl = slice(r * HEAD_DIM, (r + 1) * HEAD_DIM)
        s = jnp.where(valid, _dot_nt(q_ref[:, sl], k), NEG)
        m = jnp.max(s, axis=-1, keepdims=True)
        p = jnp.where(valid, jnp.exp2(s - m), 0.0)
        d = jnp.sum(p, axis=-1, keepdims=True)
        p = p / jnp.where(d > 0, d, 1.0)
        psum = psum + p
        oc_ref[:, sl] = _dot(p.astype(BF16), v)
    imp = lax.dot_general(ovt_ref[...], psum, (((1,), (1,)), ((), ())), preferred_element_type=F32,
                          precision=lax.Precision.HIGHEST)[0:n_slc, :]
    j = lax.broadcasted_iota(jnp.int32, (n_slc, tt), 0)
    cur = (t0 + lax.broadcasted_iota(jnp.int32, (n_slc, tt), 1)) >> SEL_SHIFT
    score = jnp.where(j <= cur, imp, -BIG)
    score = jnp.where(j == 0, BIG, jnp.where(j == cur, BIG, jnp.where(j == cur - 1, BIG, score)))
    cnt = jnp.zeros((n_slc, tt), F32)
    for jp in range(n_slc):
        row = score[jp:jp + 1, :]
        ge = jnp.where(row >= score, 1.0, 0.0)
        gt = jnp.where(row > score, 1.0, 0.0)
        cnt = cnt + jnp.where(j > jp, ge, gt)
    sel = jnp.where(cnt < SEL_TOPN, jnp.where(score > -0.5 * BIG, 1.0, 0.0), 0.0)
    sel = jnp.concatenate([sel, jnp.zeros((LANES - n_slc, tt), F32)], axis=0)
    sel_ref[...] = sel.T.astype(sel_ref.dtype)


def _cmp_attention(q, kc, vc, overlap, seq):
    m, _ = q.shape
    b = m // seq
    tt = 512
    per = seq // tt
    n_cmp = (seq - CMP_BLOCK) // CMP_STRIDE + 1
    assert n_cmp <= LANES and n_cmp * CMP_STRIDE + CMP_BLOCK - 1 >= seq
    kspec = pl.BlockSpec((None, None, LANES, HEAD_DIM), lambda i, g, j: (i, g, 0, 0))
    return pl.pallas_call(
        functools.partial(_cmp_kernel, tt=tt, n_slc=seq // SEL_BLOCK),
        grid=(b, NSA_GROUPS, per),
        in_specs=[pl.BlockSpec((tt, NSA_REP * HEAD_DIM), lambda i, g, j: (i * per + j, g)),
                  kspec, kspec,
                  pl.BlockSpec((LANES, LANES), lambda i, g, j: (0, 0))],
        out_specs=[pl.BlockSpec((tt, NSA_REP * HEAD_DIM), lambda i, g, j: (i * per + j, g)),
                   pl.BlockSpec((None, None, tt, LANES), lambda i, g, j: (i, g, j, 0))],
        out_shape=[jax.ShapeDtypeStruct((m, NSA_HEADS * HEAD_DIM), F32),
                   jax.ShapeDtypeStruct((b, NSA_GROUPS, seq, LANES), BF16)],
        compiler_params=_cparams("arbitrary", "arbitrary", "arbitrary"),
        name="cmp_attn",
    )(q, kc, vc, overlap)


def _softmax_pv(t, v):
    m = jnp.max(t, axis=-1, keepdims=True)
    p = jnp.exp2(t - m)
    l = jnp.sum(p, axis=-1, keepdims=True)
    return _dot(p.astype(BF16), v) / l


def _tri_bias(n, lower):
    row = lax.broadcasted_iota(jnp.int32, (n, n), 0)
    col = lax.broadcasted_iota(jnp.int32, (n, n), 1)
    return jnp.where(col <= row, 0.0, NEG) if lower else jnp.where(col > row, 0.0, NEG)


def _nsa_kernel(q_ref, ks_ref, vs_ref, kw_ref, vw_ref, sel_ref, ex_ref, oc_ref, gl_ref, o_ref,
                t_sc, p_sc, l_sc, bs_sc, bw_sc, os_sc, *, tq):
    seq = q_ref.shape[0]
    live = pl.program_id(0) >= 0
    n_back = WINDOW // tq
    w_cols = (n_back + 1) * tq
    bw_sc[:, 0:tq] = _tri_bias(tq, False)
    bw_sc[:, tq:n_back * tq] = jnp.zeros((tq, (n_back - 1) * tq), F32)
    bw_sc[:, n_back * tq:w_cols] = _tri_bias(tq, True)

    jobs = []
    for qt in range(seq // tq):
        lo, hi = qt * tq, (qt + 1) * tq
        jobs.append(("sel", lo, hi, 0))
        jobs.append(("win", lo, hi, max(lo - WINDOW, 0)))

    def head_rows(r):
        return slice(r * tq, (r + 1) * tq)

    def stage_scores(j):
        kind, lo, hi, klo = jobs[j]
        slot = j % 2
        n = hi - klo
        q4 = jnp.concatenate([q_ref[lo:hi, r * HEAD_DIM:(r + 1) * HEAD_DIM] for r in range(NSA_REP)], axis=0)
        k_ref = ks_ref if kind == "sel" else kw_ref
        t_sc[slot, :, 0:n] = _dot_nt(q4, k_ref[klo:hi, :])
        if kind == "sel":
            picked = _dot(sel_ref[lo:hi, :], ex_ref[:, 0:hi])
            bias = jnp.where(picked > 0.5, 0.0, NEG)
            if lo > 0:
                bs_sc[slot, :, 0:lo] = bias[:, 0:lo]
            bs_sc[slot, :, lo:hi] = bias[:, lo:hi] + _tri_bias(tq, True)

    def stage_softmax(j):
        kind, lo, hi, klo = jobs[j]
        slot = j % 2
        n = hi - klo
        for r in range(NSA_REP):
            bias = bs_sc[slot, :, 0:n] if kind == "sel" else bw_sc[:, w_cols - n:w_cols]
            t = t_sc[slot, head_rows(r), 0:n] + bias
            p = jnp.exp2(t - jnp.max(t, axis=-1, keepdims=True))
            l_sc[slot, head_rows(r), :] = jnp.sum(p, axis=-1, keepdims=True)
            p_sc[slot, head_rows(r), 0:n] = p.astype(BF16)

    def stage_output(j):
        kind, lo, hi, klo = jobs[j]
        slot = j % 2
        n = hi - klo
        v_ref = vs_ref if kind == "sel" else vw_ref
        o4 = _dot(p_sc[slot, :, 0:n], v_ref[klo:hi, :]) / l_sc[slot]
        if kind == "sel":
            os_sc[...] = o4
            return
        gate = jax.nn.sigmoid(gl_ref[lo:hi, :])
        for r in range(NSA_REP):
            sl = slice(r * HEAD_DIM, (r + 1) * HEAD_DIM)
            o = (gate[:, 3 * r:3 * r + 1] * oc_ref[lo:hi, sl]
                 + gate[:, 3 * r + 1:3 * r + 2] * os_sc[head_rows(r), :]
                 + gate[:, 3 * r + 2:3 * r + 3] * o4[head_rows(r), :])
            o_ref[lo:hi, sl] = o.astype(o_ref.dtype)

    for i in range(len(jobs) + 2):
        @pl.when(live)
        def _region():
            if i < len(jobs):
                stage_scores(i)
            if 0 <= i - 1 < len(jobs):
                stage_softmax(i - 1)
            if 0 <= i - 2 < len(jobs):
                stage_output(i - 2)


def _nsa_attention(q, kk, vv, sel, expand, oc, gl, seq):
    m = q.shape[0]
    b = m // seq
    width = NSA_REP * HEAD_DIM
    qspec = pl.BlockSpec((seq, width), lambda i, g: (i, g))
    ks = pl.BlockSpec((seq, HEAD_DIM), lambda i, g: (i, g))
    kw = pl.BlockSpec((seq, HEAD_DIM), lambda i, g: (i, NSA_GROUPS + g))
    tq = 128
    rows = NSA_REP * tq
    return pl.pallas_call(
        functools.partial(_nsa_kernel, tq=tq),
        grid=(b, NSA_GROUPS),
        scratch_shapes=[pltpu.VMEM((2, rows, seq), F32), pltpu.VMEM((2, rows, seq), BF16),
                        pltpu.VMEM((2, rows, 1), F32), pltpu.VMEM((2, tq, seq), F32),
                        pltpu.VMEM((tq, WINDOW + tq), F32), pltpu.VMEM((rows, HEAD_DIM), F32)],
        in_specs=[qspec, ks, ks, kw, kw,
                  pl.BlockSpec((None, None, seq, LANES), lambda i, g: (i, g, 0, 0)),
                  pl.BlockSpec((LANES, seq), lambda i, g: (0, 0)),
                  qspec,
                  pl.BlockSpec((seq, LANES), lambda i, g: (i, g))],
        out_specs=qspec,
        out_shape=jax.ShapeDtypeStruct((m, NSA_HEADS * HEAD_DIM), BF16),
        compiler_params=_cparams("arbitrary", "arbitrary"),
        name="nsa_attn",
    )(q, kk, vv, kk, vv, sel, expand, oc, gl)


def _cum_kernel(f_ref, b_ref, o_ref):
    x = f_ref[...] + b_ref[...]
    ls = jnp.minimum(x, 0.0) - jnp.log1p(jnp.exp(-jnp.abs(x)))
    row = lax.broadcasted_iota(jnp.int32, ls.shape, 0)
    d = 1
    while d < ls.shape[0]:
        ls = ls + jnp.where(row >= d, pltpu.roll(ls, d, 0), 0.0)
        d *= 2
    o_ref[...] = ls


def _cum_forget(fl, bias, seq, col_block):
    m = fl.shape[0]
    return pl.pallas_call(
        _cum_kernel,
        grid=(m // seq,),
        in_specs=[pl.BlockSpec((seq, LANES), lambda i: (i, col_block)),
                  pl.BlockSpec((1, LANES), lambda i: (0, 0))],
        out_specs=pl.BlockSpec((seq, LANES), lambda i: (i, 0)),
        out_shape=jax.ShapeDtypeStruct((m, LANES), F32),
        compiler_params=_cparams("arbitrary"),
        name="cum_forget",
    )(fl, bias)


def _fox_kernel(q_ref, k_ref, v_ref, ck_ref, *rest, tq):
    n_cast = (len(rest) - 1) // 2
    o_ref = rest[n_cast]
    _cast_blocks(rest[:n_cast], rest[n_cast + 1:])
    seq = q_ref.shape[0]
    causal_b = _tri_bias(tq, True)
    ck = ck_ref[...] * LOG2E
    for qt in range(seq // tq):
        lo, hi = qt * tq, (qt + 1) * tq
        t = _dot_nt(q_ref[lo:hi, :], k_ref[0:hi, :]) - ck[:, 0:hi]
        diag = t[:, lo:hi] + causal_b
        t = diag if qt == 0 else jnp.concatenate([t[:, 0:lo], diag], axis=1)
        o_ref[lo:hi, :] = _softmax_pv(t, v_ref[0:hi, :]).astype(o_ref.dtype)


def _fox_attention(q, k, v, ck, seq, casts=()):
    m = q.shape[0]
    b = m // seq
    blk = pl.BlockSpec((seq, HEAD_DIM), lambda i, h: (i, h))
    cast_specs, cast_shapes = _cast_specs(casts, b * FOX_HEADS, lambda i, h: i * FOX_HEADS + h)
    out = pl.pallas_call(
        functools.partial(_fox_kernel, tq=256),
        grid=(b, FOX_HEADS),
        in_specs=[blk, blk, blk,
                  pl.BlockSpec((None, 1, seq), lambda i, h: (i * FOX_HEADS + h, 0, 0))] + cast_specs,
        out_specs=[blk] + cast_specs,
        out_shape=[jax.ShapeDtypeStruct((m, FOX_HEADS * HEAD_DIM), BF16)] + cast_shapes,
        compiler_params=_cparams("arbitrary", "arbitrary"),
        name="fox_attn",
    )(q, k, v, ck, *casts)
    return out if casts else out[0]


def _merge_kernel(a1_ref, w1_ref, a2_ref, w2_ref, g1_ref, g2_ref, *rest):
    n_cast = (len(rest) - 1) // 2
    o_ref = rest[n_cast]
    u1 = _dot(a1_ref[...], w1_ref[...])
    u2 = _dot(a2_ref[...], w2_ref[...])
    o_ref[...] = (g1_ref[...].astype(F32) * u1 + g2_ref[...].astype(F32) * u2).astype(o_ref.dtype)
    _cast_blocks(rest[:n_cast], rest[n_cast + 1:])


def _merge(a1, w1, a2, w2, gates, casts=()):
    m, k = a1.shape
    n = w1.shape[1]
    tm, tn = 1024, 512
    nb = n // tn
    mb = m // tm
    aspec = pl.BlockSpec((tm, k), lambda j, i: (i, 0))
    wspec = pl.BlockSpec((k, tn), lambda j, i: (0, j))
    cast_specs, cast_shapes = _cast_specs(casts, nb * mb, lambda j, i: j * mb + i)
    out = pl.pallas_call(
        _merge_kernel,
        grid=(nb, mb),
        in_specs=[aspec, wspec, aspec, wspec,
                  pl.BlockSpec((tm, tn), lambda j, i: (i, j)),
                  pl.BlockSpec((tm, tn), lambda j, i: (i, nb + j))] + cast_specs,
        out_specs=[pl.BlockSpec((tm, tn), lambda j, i: (i, j))] + cast_specs,
        out_shape=[jax.ShapeDtypeStruct((m, n), BF16)] + cast_shapes,
        compiler_params=_cparams("arbitrary", "arbitrary"),
        name="merge",
    )(a1, w1, a2, w2, gates, gates, *casts)
    return out if casts else out[0]


def _rope_tables(pos):
    inv = ROPE_THETA ** (-jnp.arange(ROPE_HALF, dtype=F32) / ROPE_HALF)
    ang = jnp.asarray(pos, dtype=F32)[:, None] * inv[None, :]
    cos, sin = jnp.cos(ang), jnp.sin(ang)
    n = ang.shape[0]
    gap = HEAD_DIM // 2 - ROPE_HALF
    c = jnp.concatenate([cos, jnp.ones((n, gap), F32), cos, jnp.ones((n, gap), F32)], axis=1)
    s = jnp.concatenate([-sin, jnp.zeros((n, gap), F32), sin, jnp.zeros((n, gap), F32)], axis=1)
    return c, s


def _layer(x2, c_pad, bsz, seq, w_ada, b_ada, norm1_g, norm2_g, w_in, b_forget, nsa_q_norm, nsa_k_norm,
           fox_q_norm, fox_k_norm, cmp_pos_k, cmp_pos_v, w_cmp_k1, w_cmp_k2, w_cmp_v1, w_cmp_v2,
           w_up_nsa, w_up_fox, w_out, w_ff1, w_ff2):
    d = x2.shape[1]
    hd = HEAD_DIM
    tm = 1024
    per_m = seq // tm

    mod = _ada(c_pad, w_ada, b_ada.reshape(1, -1))[:bsz]
    shift1, scale1, gate1, shift2, scale2, gate2 = [t.reshape(bsz, 1, d) for t in jnp.split(mod, 6, axis=-1)]

    h = _normmod(x2, norm1_g.reshape(1, d), scale1, shift1, seq)

    o_q = 0
    o_kv = o_q + NSA_HEADS * hd
    o_gate = o_kv + 3 * 2 * NSA_GROUPS * hd
    o_fox = o_gate + 3 * NSA_HEADS
    o_f = o_fox + 3 * FOX_HEADS * hd
    o_merge = o_f + FOX_HEADS
    gw = NSA_GROUPS * hd

    w_t = w_in.T.astype(BF16)
    fw = FOX_HEADS * hd

    pos = np.arange(seq)
    c_t, s_t = _rope_tables(pos)
    tab_specs = [pl.BlockSpec((tm, hd), lambda j, i: (i % per_m, 0))] * 2
    gspec = pl.BlockSpec((1, MXU_COLS), lambda j, i: (0, 0))
    hspec = pl.BlockSpec((1, hd), lambda j, i: (0, 0))
    ones_spec = pl.BlockSpec((MXU_COLS, MXU_COLS), lambda j, i: (0, 0))
    head_ones = jnp.asarray(np.kron(np.eye(MXU_COLS // hd), np.ones((hd, hd))), dtype=BF16)

    def chunk_gain(g):
        return jnp.tile(g.reshape(1, hd), (1, MXU_COLS // hd))

    q_scale = ATTN_SCALE * LOG2E
    def permuted_heads(lo, n_heads):
        return _head_perm(w_t[lo:lo + n_heads * hd].reshape(n_heads, hd, d), 1).reshape(n_heads * hd, d)
    w_qk = jnp.concatenate([permuted_heads(o_q, NSA_HEADS), permuted_heads(o_kv + 2 * gw, NSA_GROUPS),
                            permuted_heads(o_kv + 4 * gw, NSA_GROUPS)], axis=0)
    g_q = chunk_gain(q_scale * _head_perm(nsa_q_norm, 0))
    g_k = chunk_gain(_head_perm(nsa_k_norm, 0))
    rope_specs = [gspec, ones_spec] + tab_specs
    q_n = _matmul(h, w_qk, _ep_headnorm_rope, BF16, rows=(0, 1024, 2),
                  extras=(g_q, head_ones, c_t, s_t), extra_specs=rope_specs, name="proj_nsa_q")
    k_sw = _matmul(h, w_qk, _ep_headnorm_rope, BF16, rows=(NSA_HEADS * hd, 0, 1),
                   extras=(g_k, head_ones, c_t, s_t), extra_specs=rope_specs, name="proj_nsa_k")
    v_sw = _matmul(h, w_t, _ep_raw, BF16, tn=gw, rows=(o_kv + 3 * gw, 2 * gw, 2), name="proj_nsa_v")
    kv_c = _matmul(h, w_t, _ep_raw, F32, rows=(o_kv, 2 * gw, 1), name="proj_nsa_cmp")
    fq = _matmul(h, w_t, _ep_headnorm, BF16, rows=(o_fox, 1024, 2),
                 extras=(q_scale * fox_q_norm.reshape(1, hd),), extra_specs=[hspec], name="proj_fox_q")
    fk = _matmul(h, w_t, _ep_headnorm, BF16, rows=(o_fox + fw, 1024, 2),
                 extras=(fox_k_norm.reshape(1, hd),), extra_specs=[hspec], name="proj_fox_k")
    fv = _matmul(h, w_t, _ep_raw, BF16, rows=(o_fox + 2 * fw, 1024, 2), name="proj_fox_v")
    g_merge = _matmul(h, w_t, _ep_sigmoid, BF16, rows=(o_merge, 1024, 2 * d // 1024), name="proj_merge")

    per_group = NSA_REP * 3
    gate_blocks = [jnp.pad(w_t[o_gate + g * per_group:o_gate + (g + 1) * per_group],
                           ((0, LANES - per_group), (0, 0))) for g in range(NSA_GROUPS)]
    f_block = jnp.pad(w_t[o_f:o_merge], ((0, LANES - FOX_HEADS), (0, 0)))
    w_small = jnp.concatenate(gate_blocks + [f_block], axis=0)
    small = _matmul(h, w_small, _ep_raw, F32, tn=w_small.shape[0], rows=(0, 0, 1), name="proj_small")

    n_rows = seq // CMP_STRIDE
    end_pos = np.arange(n_rows) * CMP_STRIDE + CMP_BLOCK - 1
    ce, se = _rope_tables(end_pos)
    k_cmp, v_cmp = _compress(
        kv_c, seq, cmp_pos_k.reshape(2, CMP_STRIDE * hd), cmp_pos_v.reshape(2, CMP_STRIDE * hd),
        w_cmp_k1.astype(BF16), _head_perm(w_cmp_k2, 1).astype(BF16), w_cmp_v1.astype(BF16), w_cmp_v2.astype(BF16),
        _head_perm(nsa_k_norm, 0).reshape(1, hd), ce, se)

    n_slc = seq // SEL_BLOCK
    ci = np.arange(LANES)[:, None] * CMP_STRIDE
    sj = np.arange(LANES)[None, :] * SEL_BLOCK
    overlap = ((ci < sj + SEL_BLOCK) & (ci + CMP_BLOCK > sj) & (np.arange(LANES)[None, :] < n_slc)
               & (np.arange(LANES)[:, None] < n_rows - 1)).astype(np.float32)
    o_c, sel = _cmp_attention(q_n, k_cmp, v_cmp, jnp.asarray(overlap.T), seq)

    expand = (np.arange(LANES)[:, None] == (np.arange(seq)[None, :] >> SEL_SHIFT)).astype(np.float32)
    o_nsa = _nsa_attention(q_n, k_sw, v_sw, sel, jnp.asarray(expand, dtype=BF16), o_c, small, seq)

    f_bias = jnp.pad(b_forget, (0, LANES - FOX_HEADS)).reshape(1, LANES)
    cum = _cum_forget(small, f_bias, seq, NSA_GROUPS)
    ck = cum[:, :FOX_HEADS].reshape(bsz, seq, FOX_HEADS).transpose(0, 2, 1).reshape(bsz * FOX_HEADS, 1, seq)
    o_fox, up_nsa_b, up_fox_b, w_out_b, w_ff1_b = _fox_attention(
        fq, fk, fv, ck, seq, casts=(w_up_nsa, w_up_fox, w_out, w_ff1))

    y = _merge(o_nsa, up_nsa_b, o_fox, up_fox_b, g_merge)
    tn_out = 1024
    res_specs = [pl.BlockSpec((tm, tn_out), lambda j, i: (i, j)),
                 pl.BlockSpec((None, 1, tn_out), lambda j, i: (i // per_m, 0, j))]
    x_mid = _matmul(y, w_out_b, _ep_residual, F32, extras=(x2, gate1), extra_specs=res_specs,
                    tn=tn_out, name="out_proj")

    h2 = _normmod(x_mid, norm2_g.reshape(1, d), scale2, shift2, seq)
    hid, w_ff2_b = _matmul(h2, w_ff1_b, _ep_relu2, BF16, casts=(w_ff2,), name="ff1")
    return _matmul_k_residual(hid, w_ff2_b, x_mid, gate2, seq, name="ff2")


def kernel(x, c, w_ada, b_ada, norm1_g, norm2_g, w_in, b_forget, nsa_q_norm, nsa_k_norm, fox_q_norm, fox_k_norm, cmp_pos_k, cmp_pos_v, w_cmp_k1, w_cmp_k2, w_cmp_v1, w_cmp_v2, w_up_nsa, w_up_fox, w_out, w_ff1, w_ff2):
    bsz, seq, d = x.shape
    x2 = x.reshape(bsz * seq, d)
    c_pad = jnp.pad(c, ((0, 8 - bsz), (0, 0)))
    params = (w_ada, b_ada, norm1_g, norm2_g, w_in, b_forget, nsa_q_norm, nsa_k_norm, fox_q_norm, fox_k_norm,
              cmp_pos_k, cmp_pos_v, w_cmp_k1, w_cmp_k2, w_cmp_v1, w_cmp_v2, w_up_nsa, w_up_fox, w_out,
              w_ff1, w_ff2)
    for layer in range(w_ada.shape[0]):
        x2 = _layer(x2, c_pad, bsz, seq, *[p[layer] for p in params])
    return x2.reshape(bsz, seq, d)
```

```python
import functools

import numpy as np
import jax
import jax.numpy as jnp
from jax import lax
from jax.experimental import pallas as pl
from jax.experimental.pallas import tpu as pltpu

F32 = jnp.float32
BF16 = jnp.bfloat16

HEAD_DIM = 128
NSA_HEADS = 16
NSA_GROUPS = 4
NSA_REP = NSA_HEADS // NSA_GROUPS
FOX_HEADS = 16
CMP_BLOCK = 32
CMP_STRIDE = 16
SEL_BLOCK = 64
SEL_SHIFT = SEL_BLOCK.bit_length() - 1
SEL_TOPN = 16
WINDOW = 512
ROPE_THETA = 500000.0
ROT_DIM = HEAD_DIM // 4
RMS_EPS = 1e-6
ATTN_SCALE = HEAD_DIM ** -0.5

LOG2E = float(np.log2(np.e))
ROPE_HALF = ROT_DIM // 2
MXU_COLS = 256
REGION_JOBS = 2


def _head_perm(x, axis):
    mid = HEAD_DIM // 2
    cuts = [(0, ROPE_HALF), (mid, mid + ROPE_HALF), (ROT_DIM, mid), (ROPE_HALF, ROT_DIM), (mid + ROPE_HALF, HEAD_DIM)]
    return jnp.concatenate([lax.slice_in_dim(x, a, b, axis=axis) for a, b in cuts], axis=axis)
NEG = -1e30
BIG = 1e30
LANES = 128
VMEM_LIMIT = 60 * 1024 * 1024


def _cparams(*sem):
    return pltpu.CompilerParams(dimension_semantics=sem, vmem_limit_bytes=VMEM_LIMIT)


def _dot(a, b):
    return jnp.dot(a, b, preferred_element_type=F32)


def _dot_nt(a, b):
    return lax.dot_general(a, b, (((1,), (1,)), ((), ())), preferred_element_type=F32)


def _ada_kernel(c_ref, w_ref, b_ref, o_ref):
    c = c_ref[...]
    s = (c * jax.nn.sigmoid(c)).astype(BF16)
    o_ref[...] = _dot(s, w_ref[...].astype(BF16)) + b_ref[...]


def _ada(c_pad, w, b):
    rows, d = c_pad.shape
    n = w.shape[1]
    tn = 1024
    return pl.pallas_call(
        _ada_kernel,
        grid=(n // tn,),
        in_specs=[pl.BlockSpec((rows, d), lambda j: (0, 0)),
                  pl.BlockSpec((d, tn), lambda j: (0, j)),
                  pl.BlockSpec((1, tn), lambda j: (0, j))],
        out_specs=pl.BlockSpec((rows, tn), lambda j: (0, j)),
        out_shape=jax.ShapeDtypeStruct((rows, n), F32),
        compiler_params=_cparams("arbitrary"),
        name="ada",
    )(c_pad, w, b)


def _normmod_kernel(x_ref, g_ref, sc_ref, sh_ref, o_ref):
    x = x_ref[...]
    ms = jnp.mean(x * x, axis=-1, keepdims=True)
    y = x * lax.rsqrt(ms + RMS_EPS) * g_ref[...]
    o_ref[...] = (y * (1.0 + sc_ref[...]) + sh_ref[...]).astype(o_ref.dtype)


def _normmod(x2, g, scale, shift, seq):
    m, d = x2.shape
    tr = 512
    per = seq // tr
    return pl.pallas_call(
        _normmod_kernel,
        grid=(m // tr,),
        in_specs=[pl.BlockSpec((tr, d), lambda i: (i, 0)),
                  pl.BlockSpec((1, d), lambda i: (0, 0)),
                  pl.BlockSpec((None, 1, d), lambda i: (i // per, 0, 0)),
                  pl.BlockSpec((None, 1, d), lambda i: (i // per, 0, 0))],
        out_specs=pl.BlockSpec((tr, d), lambda i: (i, 0)),
        out_shape=jax.ShapeDtypeStruct((m, d), BF16),
        compiler_params=_cparams("arbitrary"),
        name="normmod",
    )(x2, g, scale, shift)


def _ep_raw(acc, o_ref):
    o_ref[...] = acc.astype(o_ref.dtype)


def _ep_sigmoid(acc, o_ref):
    o_ref[...] = jax.nn.sigmoid(acc).astype(o_ref.dtype)


def _ep_relu2(acc, o_ref):
    r = jnp.maximum(acc, 0.0)
    o_ref[...] = (r * r).astype(o_ref.dtype)


def _ep_residual(acc, o_ref, x_ref, g_ref):
    o_ref[...] = x_ref[...] + g_ref[...] * acc


def _head_norm(a, g):
    ms = jnp.mean(a * a, axis=-1, keepdims=True)
    return a * lax.rsqrt(ms + RMS_EPS) * g


def _rope(a, c, s):
    return a * c + pltpu.roll(a, HEAD_DIM // 2, 1) * s


def _chunk_head_norm(acc, g_ref, ones_ref):
    ss = _dot((acc * acc).astype(BF16), ones_ref[...])
    return acc * lax.rsqrt(ss * (1.0 / HEAD_DIM) + RMS_EPS) * g_ref[...]


def _ep_headnorm(acc, o_ref, g_ref):
    g = g_ref[...]
    for h0 in range(0, acc.shape[1], HEAD_DIM):
        o_ref[:, h0:h0 + HEAD_DIM] = _head_norm(acc[:, h0:h0 + HEAD_DIM], g).astype(o_ref.dtype)


def _ep_headnorm_rope(acc, o_ref, g_ref, ones_ref, c_ref, s_ref):
    c, s = c_ref[...], s_ref[...]
    for c0 in range(0, acc.shape[1], MXU_COLS):
        y = _chunk_head_norm(acc[:, c0:c0 + MXU_COLS], g_ref, ones_ref)
        for h0 in range(0, MXU_COLS, HEAD_DIM):
            dst = slice(c0 + h0, c0 + h0 + HEAD_DIM)
            o_ref[:, dst] = _rope(y[:, h0:h0 + HEAD_DIM], c, s).astype(o_ref.dtype)


def _cast_blocks(refs_in, refs_out):
    for src, dst in zip(refs_in, refs_out):
        dst[...] = src[...].astype(dst.dtype)


def _cast_specs(casts, steps, index_of_step):
    specs, shapes = [], []
    for w in casts:
        rows, cols = w.shape
        assert rows % (steps * 16) == 0
        specs.append(pl.BlockSpec((rows // steps, cols), lambda *g: (index_of_step(*g), 0)))
        shapes.append(jax.ShapeDtypeStruct((rows, cols), BF16))
    return specs, shapes


def _mm_kernel(a_ref, b_ref, *rest, epilogue, n_extra, n_cast, transposed):
    extras = rest[:n_extra]
    cast_in = rest[n_extra:n_extra + n_cast]
    o_ref = rest[n_extra + n_cast]
    cast_out = rest[n_extra + n_cast + 1:]
    acc = _dot_nt(a_ref[...], b_ref[...]) if transposed else _dot(a_ref[...], b_ref[...])
    epilogue(acc, o_ref, *extras)
    _cast_blocks(cast_in, cast_out)


def _matmul(a, b, epilogue, out_dtype, extras=(), extra_specs=(), tm=1024, tn=1024, name="mm",
            casts=(), rows=None):
    m, k = a.shape
    if rows is None:
        n = b.shape[1]
        tn = min(tn, n)
        b_spec = pl.BlockSpec((k, tn), lambda j, i: (0, j))
    else:
        first, stride, count = rows
        assert first % 16 == 0 and stride % 16 == 0
        n = count * tn
        b_spec = pl.BlockSpec((pl.Element(tn), pl.Element(k)),
                              lambda j, i: (pl.multiple_of(first + j * stride, 16), 0))
    grid = (n // tn, m // tm)
    cast_specs, cast_shapes = _cast_specs(casts, grid[0] * grid[1], lambda j, i: j * grid[1] + i)
    out = pl.pallas_call(
        functools.partial(_mm_kernel, epilogue=epilogue, n_extra=len(extras), n_cast=len(casts),
                          transposed=rows is not None),
        grid=grid,
        in_specs=[pl.BlockSpec((tm, k), lambda j, i: (i, 0)), b_spec] + list(extra_specs) + cast_specs,
        out_specs=[pl.BlockSpec((tm, tn), lambda j, i: (i, j))] + cast_specs,
        out_shape=[jax.ShapeDtypeStruct((m, n), out_dtype)] + cast_shapes,
        compiler_params=_cparams("arbitrary", "arbitrary"),
        name=name,
    )(a, b, *extras, *casts)
    return out if casts else out[0]


def _mmk_kernel(a_ref, b_ref, x_ref, g_ref, o_ref):
    kk = pl.program_id(2)

    @pl.when(kk == 0)
    def _():
        o_ref[...] = x_ref[...] + g_ref[...] * _dot(a_ref[...], b_ref[...])

    @pl.when(kk > 0)
    def _():
        o_ref[...] += g_ref[...] * _dot(a_ref[...], b_ref[...])


def _matmul_k_residual(a, b, x, gate, seq, tm=1024, tn=1024, tk=4096, name="mmk"):
    m, k = a.shape
    n = b.shape[1]
    per = seq // tm
    nk = k // tk
    return pl.pallas_call(
        _mmk_kernel,
        grid=(m // tm, n // tn, nk),
        in_specs=[pl.BlockSpec((tm, tk), lambda i, j, kk: (i, kk)),
                  pl.BlockSpec((tk, tn), lambda i, j, kk: (kk, j)),
                  pl.BlockSpec((tm, tn), lambda i, j, kk: (i, j)),
                  pl.BlockSpec((None, 1, tn), lambda i, j, kk: (i // per, 0, j))],
        out_specs=pl.BlockSpec((tm, tn), lambda i, j, kk: (i, j)),
        out_shape=jax.ShapeDtypeStruct((m, n), F32),
        compiler_params=_cparams("arbitrary", "arbitrary", "arbitrary"),
        name=name,
    )(a, b, x, gate)


def _compress_kernel(xk_ref, xv_ref, pek_ref, pev_ref, w1k_ref, w2k_ref, w1v_ref, w2v_ref,
                     g_ref, c_ref, s_ref, ko_ref, vo_ref):
    half = CMP_STRIDE * HEAD_DIM
    n_rows = xk_ref.shape[0] // CMP_STRIDE

    def comp(x_ref, pe_ref, w1_ref, w2_ref):
        x = jnp.concatenate([x_ref[pl.ds(l, n_rows, stride=CMP_STRIDE), :] for l in range(CMP_STRIDE)], axis=1)
        lo = (x + pe_ref[0:1, :]).astype(BF16)
        hi = (x + pe_ref[1:2, :]).astype(BF16)
        p = _dot(lo, w1_ref[0:half, :])
        q = _dot(hi, w1_ref[half:2 * half, :])
        h = p + pltpu.roll(q, q.shape[0] - 1, 0)
        h = jax.nn.gelu(h, approximate=True)
        return _dot(h.astype(BF16), w2_ref[...])

    kc = comp(xk_ref, pek_ref, w1k_ref, w2k_ref)
    kc = _rope(_head_norm(kc, g_ref[...]), c_ref[...], s_ref[...])
    ko_ref[...] = kc.astype(ko_ref.dtype)
    vo_ref[...] = comp(xv_ref, pev_ref, w1v_ref, w2v_ref).astype(vo_ref.dtype)


def _compress(kv_c, seq, pek, pev, w1k, w2k, w1v, w2v, g, c, s):
    b = kv_c.shape[0] // seq
    nr = seq // CMP_STRIDE
    kspec = pl.BlockSpec((seq, HEAD_DIM), lambda i, j: (i, j))
    vspec = pl.BlockSpec((seq, HEAD_DIM), lambda i, j: (i, NSA_GROUPS + j))
    full = lambda arr: pl.BlockSpec(arr.shape, lambda i, j: (0,) * arr.ndim)
    ospec = pl.BlockSpec((None, None, nr, HEAD_DIM), lambda i, j: (i, j, 0, 0))
    oshape = jax.ShapeDtypeStruct((b, NSA_GROUPS, nr, HEAD_DIM), BF16)
    return pl.pallas_call(
        _compress_kernel,
        grid=(b, NSA_GROUPS),
        in_specs=[kspec, vspec, full(pek), full(pev), full(w1k), full(w2k), full(w1v), full(w2v),
                  full(g), full(c), full(s)],
        out_specs=[ospec, ospec],
        out_shape=[oshape, oshape],
        compiler_params=_cparams("arbitrary", "arbitrary"),
        name="compress",
    )(kv_c, kv_c, pek, pev, w1k, w2k, w1v, w2v, g, c, s)


def _cmp_kernel(q_ref, k_ref, v_ref, ovt_ref, oc_ref, sel_ref, *, tt, n_slc):
    t0 = pl.program_id(2) * tt
    t = t0 + lax.broadcasted_iota(jnp.int32, (tt, LANES), 0)
    n = lax.broadcasted_iota(jnp.int32, (tt, LANES), 1)
    valid = n * CMP_STRIDE + (CMP_BLOCK - 1) <= t
    k = k_ref[...]
    v = v_ref[...]
    psum = jnp.zeros((tt, LANES), F32)
    for r in range(NSA_REP):
        sl = slice(r * HEAD_DIM, (r + 1) * HEAD_DIM)
        s = jnp.where(valid, _dot_nt(q_ref[:, sl], k), NEG)
        m = jnp.max(s, axis=-1, keepdims=True)
        p = jnp.where(valid, jnp.exp2(s - m), 0.0)
        d = jnp.sum(p, axis=-1, keepdims=True)
        p = p / jnp.where(d > 0, d, 1.0)
        psum = psum + p
        oc_ref[:, sl] = _dot(p.astype(BF16), v)
    imp = lax.dot_general(ovt_ref[...], psum, (((1,), (1,)), ((), ())), preferred_element_type=F32,
                          precision=lax.Precision.HIGHEST)[0:n_slc, :]
    j = lax.broadcasted_iota(jnp.int32, (n_slc, tt), 0)
    cur = (t0 + lax.broadcasted_iota(jnp.int32, (n_slc, tt), 1)) >> SEL_SHIFT
    score = jnp.where(j <= cur, imp, -BIG)
    score = jnp.where(j == 0, BIG, jnp.where(j == cur, BIG, jnp.where(j == cur - 1, BIG, score)))
    cnt = jnp.zeros((n_slc, tt), F32)
    for jp in range(n_slc):
        row = score[jp:jp + 1, :]
        ge = jnp.where(row >= score, 1.0, 0.0)
        gt = jnp.where(row > score, 1.0, 0.0)
        cnt = cnt + jnp.where(j > jp, ge, gt)
    sel = jnp.where(cnt < SEL_TOPN, jnp.where(score > -0.5 * BIG, 1.0, 0.0), 0.0)
    sel = jnp.concatenate([sel, jnp.zeros((LANES - n_slc, tt), F32)], axis=0)
    sel_ref[...] = sel.T.astype(sel_ref.dtype)


def _cmp_attention(q, kc, vc, overlap, seq):
    m, _ = q.shape
    b = m // seq
    tt = 2048
    per = seq // tt
    n_cmp =(seq - CMP_BLOCK) // CMP_STRIDE + 1
    assert n_cmp <= LANES and n_cmp * CMP_STRIDE + CMP_BLOCK - 1 >= seq
    kspec = pl.BlockSpec((None, None, LANES, HEAD_DIM), lambda i, g, j: (i, g, 0, 0))
    return pl.pallas_call(
        functools.partial(_cmp_kernel, tt=tt, n_slc=seq // SEL_BLOCK),
        grid=(b, NSA_GROUPS, per),
        in_specs=[pl.BlockSpec((tt, NSA_REP * HEAD_DIM), lambda i, g, j: (i * per + j, g)),
                  kspec, kspec,
                  pl.BlockSpec((LANES, LANES), lambda i, g, j: (0, 0))],
        out_specs=[pl.BlockSpec((tt, NSA_REP * HEAD_DIM), lambda i, g, j: (i * per + j, g)),
                   pl.BlockSpec((None, None, tt, LANES), lambda i, g, j: (i, g, j, 0))],
        out_shape=[jax.ShapeDtypeStruct((m, NSA_HEADS * HEAD_DIM), F32),
                   jax.ShapeDtypeStruct((b, NSA_GROUPS, seq, LANES), BF16)],
        compiler_params=_cparams("arbitrary", "arbitrary", "arbitrary"),
        name="cmp_attn",
    )(q, kc, vc, overlap)


def _softmax_pv(t, v):
    m = jnp.max(t, axis=-1, keepdims=True)
    p = jnp.exp2(t - m)
    l = jnp.sum(p, axis=-1, keepdims=True)
    return _dot(p.astype(BF16), v) / l


def _tri_bias(n, lower):
    row = lax.broadcasted_iota(jnp.int32, (n, n), 0)
    col = lax.broadcasted_iota(jnp.int32, (n, n), 1)
    return jnp.where(col <= row, 0.0, NEG) if lower else jnp.where(col > row, 0.0, NEG)


def _nsa_kernel(q_ref, ks_ref, vs_ref, kw_ref, vw_ref, sel_ref, ex_ref, oc_ref, gl_ref, o_ref,
                t_sc, p_sc, l_sc, bs_sc, bw_sc, os_sc, *, tq):
    seq = q_ref.shape[0]
    live = pl.program_id(0) >= 0
    n_back = WINDOW // tq
    w_cols = (n_back + 1) * tq
    bw_sc[:, 0:tq] = _tri_bias(tq, False)
    bw_sc[:, tq:n_back * tq] = jnp.zeros((tq, (n_back - 1) * tq), F32)
    bw_sc[:, n_back * tq:w_cols] = _tri_bias(tq, True)

    jobs = []
    for qt in range(seq // tq):
        lo, hi = qt * tq, (qt + 1) * tq
        jobs.append(("sel", lo, hi, 0))
        jobs.append(("win", lo, hi, max(lo - WINDOW, 0)))

    assert REGION_JOBS % 2 == 0 and len(jobs) % REGION_JOBS == 0
    n_slots = 2 * REGION_JOBS

    def bias_slot(j):
        return (j // 2) % REGION_JOBS

    def head_rows(r):
        return slice(r * tq, (r + 1) * tq)

    def stage_scores(j):
        kind, lo, hi, klo = jobs[j]
        slot = j % n_slots
        n = hi - klo
        q4 = jnp.concatenate([q_ref[lo:hi, r * HEAD_DIM:(r + 1) * HEAD_DIM] for r in range(NSA_REP)], axis=0)
        k_ref = ks_ref if kind == "sel" else kw_ref
        t_sc[slot, :, 0:n] = _dot_nt(q4, k_ref[klo:hi, :])
        if kind == "sel":
            picked = _dot(sel_ref[lo:hi, :], ex_ref[:, 0:hi])
            bias = jnp.where(picked > 0.5, 0.0, NEG)
            if lo > 0:
                bs_sc[bias_slot(j), :, 0:lo] = bias[:, 0:lo]
            bs_sc[bias_slot(j), :, lo:hi] = bias[:, lo:hi] + _tri_bias(tq, True)

    def stage_softmax(j):
        kind, lo, hi, klo = jobs[j]
        slot = j % n_slots
        n = hi - klo
        for r in range(NSA_REP):
            bias = bs_sc[bias_slot(j), :, 0:n] if kind == "sel" else bw_sc[:, w_cols - n:w_cols]
            t = t_sc[slot, head_rows(r), 0:n] + bias
            p = jnp.exp2(t - jnp.max(t, axis=-1, keepdims=True))
            l_sc[slot, head_rows(r), :] = jnp.sum(p, axis=-1, keepdims=True)
            p_sc[slot, head_rows(r), 0:n] = p.astype(BF16)

    def stage_output(j):
        kind, lo, hi, klo = jobs[j]
        slot = j % n_slots
        n = hi - klo
        v_ref = vs_ref if kind == "sel" else vw_ref
        o4 = _dot(p_sc[slot, :, 0:n], v_ref[klo:hi, :]) / l_sc[slot]
        if kind == "sel":
            os_sc[...] = o4
            return
        gate = jax.nn.sigmoid(gl_ref[lo:hi, :])
        for r in range(NSA_REP):
            sl = slice(r * HEAD_DIM, (r + 1) * HEAD_DIM)
            o = (gate[:, 3 * r:3 * r + 1] * oc_ref[lo:hi, sl]
                 + gate[:, 3 * r + 1:3 * r + 2] * os_sc[head_rows(r), :]
                 + gate[:, 3 * r + 2:3 * r + 3] * o4[head_rows(r), :])
            o_ref[lo:hi, sl] = o.astype(o_ref.dtype)

    for r0 in range(0, len(jobs) + 2 * REGION_JOBS, REGION_JOBS):
        @pl.when(live)
        def _region():
            for j in range(r0, r0 + REGION_JOBS):
                if j < len(jobs):
                    stage_scores(j)
                if 0 <= j - REGION_JOBS < len(jobs):
                    stage_softmax(j - REGION_JOBS)
                if 0 <= j - 2 * REGION_JOBS < len(jobs):
                    stage_output(j - 2 * REGION_JOBS)


def _nsa_attention(q, kk, vv, sel, expand, oc, gl, seq):
    m = q.shape[0]
    b = m // seq
    width = NSA_REP * HEAD_DIM
    qspec = pl.BlockSpec((seq, width), lambda i, g: (i, g))
    ks = pl.BlockSpec((seq, HEAD_DIM), lambda i, g: (i, g))
    kw = pl.BlockSpec((seq, HEAD_DIM), lambda i, g: (i, NSA_GROUPS + g))
    tq = 128
    rows = NSA_REP * tq
    return pl.pallas_call(
        functools.partial(_nsa_kernel, tq=tq),
        grid=(b, NSA_GROUPS),
        scratch_shapes=[pltpu.VMEM((2 * REGION_JOBS, rows, seq), F32), pltpu.VMEM((2 * REGION_JOBS, rows, seq), BF16),
                        pltpu.VMEM((2 * REGION_JOBS, rows, 1), F32), pltpu.VMEM((REGION_JOBS, tq, seq), F32),
                        pltpu.VMEM((tq, WINDOW + tq), F32), pltpu.VMEM((rows, HEAD_DIM), F32)],
        in_specs=[qspec, ks, ks, kw, kw,
                  pl.BlockSpec((None, None, seq, LANES), lambda i, g: (i, g, 0, 0)),
                  pl.BlockSpec((LANES, seq), lambda i, g: (0, 0)),
                  qspec,
                  pl.BlockSpec((seq, LANES), lambda i, g: (i, g))],
        out_specs=qspec,
        out_shape=jax.ShapeDtypeStruct((m, NSA_HEADS * HEAD_DIM), BF16),
        compiler_params=_cparams("arbitrary", "arbitrary"),
        name="nsa_attn",
    )(q, kk, vv, kk, vv, sel, expand, oc, gl)


def _cum_kernel(f_ref, b_ref, o_ref):
    x = f_ref[...] + b_ref[...]
    ls = jnp.minimum(x, 0.0) - jnp.log1p(jnp.exp(-jnp.abs(x)))
    row = lax.broadcasted_iota(jnp.int32, ls.shape, 0)
    d = 1
    while d < ls.shape[0]:
        ls = ls + jnp.where(row >= d, pltpu.roll(ls, d, 0), 0.0)
        d *= 2
    o_ref[...] = ls


def _cum_forget(fl, bias, seq, col_block):
    m = fl.shape[0]
    return pl.pallas_call(
        _cum_kernel,
        grid=(m // seq,),
        in_specs=[pl.BlockSpec((seq, LANES), lambda i: (i, col_block)),
                  pl.BlockSpec((1, LANES), lambda i: (0, 0))],
        out_specs=pl.BlockSpec((seq, LANES), lambda i: (i, 0)),
        out_shape=jax.ShapeDtypeStruct((m, LANES), F32),
        compiler_params=_cparams("arbitrary"),
        name="cum_forget",
    )(fl, bias)


def _fox_kernel(q_ref, k_ref, v_ref, ck_ref, *rest, tq):
    n_cast = (len(rest) - 1) // 2
    o_ref = rest[n_cast]
    _cast_blocks(rest[:n_cast], rest[n_cast + 1:])
    seq = q_ref.shape[0]
    causal_b = _tri_bias(tq, True)
    ck = ck_ref[...] * LOG2E
    for qt in range(seq // tq):
        lo, hi = qt * tq, (qt + 1) * tq
        t = _dot_nt(q_ref[lo:hi, :], k_ref[0:hi, :]) - ck[:, 0:hi]
        diag = t[:, lo:hi] + causal_b
        t = diag if qt == 0 else jnp.concatenate([t[:, 0:lo], diag], axis=1)
        o_ref[lo:hi, :] = _softmax_pv(t, v_ref[0:hi, :]).astype(o_ref.dtype)


def _fox_attention(q, k, v, ck, seq, casts=()):
    m = q.shape[0]
    b = m // seq
    blk = pl.BlockSpec((seq, HEAD_DIM), lambda i, h: (i, h))
    cast_specs, cast_shapes = _cast_specs(casts, b * FOX_HEADS, lambda i, h: i * FOX_HEADS + h)
    out = pl.pallas_call(
        functools.partial(_fox_kernel, tq=256),
        grid=(b, FOX_HEADS),
        in_specs=[blk, blk, blk,
                  pl.BlockSpec((None, 1, seq), lambda i, h: (i * FOX_HEADS + h, 0, 0))] + cast_specs,
        out_specs=[blk] + cast_specs,
        out_shape=[jax.ShapeDtypeStruct((m, FOX_HEADS * HEAD_DIM), BF16)] + cast_shapes,
        compiler_params=_cparams("arbitrary", "arbitrary"),
        name="fox_attn",
    )(q, k, v, ck, *casts)
    return out if casts else out[0]


def _merge_kernel(a1_ref, w1_ref, a2_ref, w2_ref, g1_ref, g2_ref, *rest):
    n_cast = (len(rest) - 1) // 2
    o_ref = rest[n_cast]
    u1 = _dot(a1_ref[...], w1_ref[...])
    u2 = _dot(a2_ref[...], w2_ref[...])
    o_ref[...] = (g1_ref[...].astype(F32) * u1 + g2_ref[...].astype(F32) * u2).astype(o_ref.dtype)
    _cast_blocks(rest[:n_cast], rest[n_cast + 1:])


def _merge(a1, w1, a2, w2, gates, casts=()):
    m, k = a1.shape
    n = w1.shape[1]
    tm, tn = 1024, 1024
    nb = n // tn
    mb = m // tm
    aspec = pl.BlockSpec((tm, k), lambda j, i: (i, 0))
    wspec = pl.BlockSpec((k, tn), lambda j, i: (0, j))
    cast_specs, cast_shapes = _cast_specs(casts, nb * mb, lambda j, i: j * mb + i)
    out = pl.pallas_call(
        _merge_kernel,
        grid=(nb, mb),
        in_specs=[aspec, wspec, aspec, wspec,
                  pl.BlockSpec((tm, tn), lambda j, i: (i, j)),
                  pl.BlockSpec((tm, tn), lambda j, i: (i, nb + j))] + cast_specs,
        out_specs=[pl.BlockSpec((tm, tn), lambda j, i: (i, j))] + cast_specs,
        out_shape=[jax.ShapeDtypeStruct((m, n), BF16)] + cast_shapes,
        compiler_params=_cparams("arbitrary", "arbitrary"),
        name="merge",
    )(a1, w1, a2, w2, gates, gates, *casts)
    return out if casts else out[0]


def _rope_tables(pos):
    inv = ROPE_THETA ** (-jnp.arange(ROPE_HALF, dtype=F32) / ROPE_HALF)
    ang = jnp.asarray(pos, dtype=F32)[:, None] * inv[None, :]
    cos, sin = jnp.cos(ang), jnp.sin(ang)
    n = ang.shape[0]
    gap = HEAD_DIM // 2 - ROPE_HALF
    c = jnp.concatenate([cos, jnp.ones((n, gap), F32), cos, jnp.ones((n, gap), F32)], axis=1)
    s = jnp.concatenate([-sin, jnp.zeros((n, gap), F32), sin, jnp.zeros((n, gap), F32)], axis=1)
    return c, s


def _layer(x2, c_pad, bsz, seq, w_ada, b_ada, norm1_g, norm2_g, w_in, b_forget, nsa_q_norm, nsa_k_norm,
           fox_q_norm, fox_k_norm, cmp_pos_k, cmp_pos_v, w_cmp_k1, w_cmp_k2, w_cmp_v1, w_cmp_v2,
           w_up_nsa, w_up_fox, w_out, w_ff1, w_ff2):
    d = x2.shape[1]
    hd = HEAD_DIM
    tm = 1024
    per_m = seq // tm

    mod = _ada(c_pad, w_ada, b_ada.reshape(1, -1))[:bsz]
    shift1, scale1, gate1, shift2, scale2, gate2 = [t.reshape(bsz, 1, d) for t in jnp.split(mod, 6, axis=-1)]

    h = _normmod(x2, norm1_g.reshape(1, d), scale1, shift1, seq)

    o_q = 0
    o_kv = o_q + NSA_HEADS * hd
    o_gate = o_kv + 3 * 2 * NSA_GROUPS * hd
    o_fox = o_gate + 3 * NSA_HEADS
    o_f = o_fox + 3 * FOX_HEADS * hd
    o_merge = o_f + FOX_HEADS
    gw = NSA_GROUPS * hd

    w_t = w_in.T.astype(BF16)
    fw = FOX_HEADS * hd

    pos = np.arange(seq)
    c_t, s_t = _rope_tables(pos)
    tab_specs = [pl.BlockSpec((tm, hd), lambda j, i: (i % per_m, 0))] * 2
    gspec = pl.BlockSpec((1, MXU_COLS), lambda j, i: (0, 0))
    hspec = pl.BlockSpec((1, hd), lambda j, i: (0, 0))
    ones_spec = pl.BlockSpec((MXU_COLS, MXU_COLS), lambda j, i: (0, 0))
    head_ones = jnp.asarray(np.kron(np.eye(MXU_COLS // hd), np.ones((hd, hd))), dtype=BF16)

    def chunk_gain(g):
        return jnp.tile(g.reshape(1, hd), (1, MXU_COLS // hd))

    q_scale = ATTN_SCALE * LOG2E
    def permuted_heads(lo, n_heads):
        return _head_perm(w_t[lo:lo + n_heads * hd].reshape(n_heads, hd, d), 1).reshape(n_heads * hd, d)
    w_qk = jnp.concatenate([permuted_heads(o_q, NSA_HEADS), permuted_heads(o_kv + 2 * gw, NSA_GROUPS),
                            permuted_heads(o_kv + 4 * gw, NSA_GROUPS)], axis=0)
    g_q = chunk_gain(q_scale * _head_perm(nsa_q_norm, 0))
    g_k = chunk_gain(_head_perm(nsa_k_norm, 0))
    rope_specs = [gspec, ones_spec] + tab_specs
    q_n = _matmul(h, w_qk, _ep_headnorm_rope, BF16, rows=(0, 1024, 2),
                  extras=(g_q, head_ones, c_t, s_t), extra_specs=rope_specs, name="proj_nsa_q")
    k_sw = _matmul(h, w_qk, _ep_headnorm_rope, BF16, rows=(NSA_HEADS * hd, 0, 1),
                   extras=(g_k, head_ones, c_t, s_t), extra_specs=rope_specs, name="proj_nsa_k")
    v_sw = _matmul(h, w_t, _ep_raw, BF16, tn=gw, rows=(o_kv + 3 * gw, 2 * gw, 2), name="proj_nsa_v")
    kv_c = _matmul(h, w_t, _ep_raw, F32, rows=(o_kv, 2 * gw, 1), name="proj_nsa_cmp")
    fq = _matmul(h, w_t, _ep_headnorm, BF16, rows=(o_fox, 1024, 2),
                 extras=(q_scale * fox_q_norm.reshape(1, hd),), extra_specs=[hspec], name="proj_fox_q")
    fk = _matmul(h, w_t, _ep_headnorm, BF16, rows=(o_fox + fw, 1024, 2),
                 extras=(fox_k_norm.reshape(1, hd),), extra_specs=[hspec], name="proj_fox_k")
    fv = _matmul(h, w_t, _ep_raw, BF16, rows=(o_fox + 2 * fw, 1024, 2), name="proj_fox_v")
    g_merge = _matmul(h, w_t, _ep_sigmoid, BF16, rows=(o_merge, 1024, 2 * d // 1024), name="proj_merge")

    per_group = NSA_REP * 3
    gate_blocks = [jnp.pad(w_t[o_gate + g * per_group:o_gate + (g + 1) * per_group],
                           ((0, LANES - per_group), (0, 0))) for g in range(NSA_GROUPS)]
    f_block = jnp.pad(w_t[o_f:o_merge], ((0, LANES - FOX_HEADS), (0, 0)))
    w_small = jnp.concatenate(gate_blocks + [f_block], axis=0)
    small = _matmul(h, w_small, _ep_raw, F32, tn=w_small.shape[0], rows=(0, 0, 1), name="proj_small")

    n_rows = seq // CMP_STRIDE
    end_pos = np.arange(n_rows) * CMP_STRIDE + CMP_BLOCK - 1
    ce, se = _rope_tables(end_pos)
    k_cmp, v_cmp = _compress(
        kv_c, seq, cmp_pos_k.reshape(2, CMP_STRIDE * hd), cmp_pos_v.reshape(2, CMP_STRIDE * hd),
        w_cmp_k1.astype(BF16), _head_perm(w_cmp_k2, 1).astype(BF16), w_cmp_v1.astype(BF16), w_cmp_v2.astype(BF16),
        _head_perm(nsa_k_norm, 0).reshape(1, hd), ce, se)

    n_slc = seq // SEL_BLOCK
    ci = np.arange(LANES)[:, None] * CMP_STRIDE
    sj = np.arange(LANES)[None, :] * SEL_BLOCK
    overlap = ((ci < sj + SEL_BLOCK) & (ci + CMP_BLOCK > sj) & (np.arange(LANES)[None, :] < n_slc)
               & (np.arange(LANES)[:, None] < n_rows - 1)).astype(np.float32)
    o_c, sel = _cmp_attention(q_n, k_cmp, v_cmp, jnp.asarray(overlap.T), seq)

    expand = (np.arange(LANES)[:, None] == (np.arange(seq)[None, :] >> SEL_SHIFT)).astype(np.float32)
    o_nsa = _nsa_attention(q_n, k_sw, v_sw, sel, jnp.asarray(expand, dtype=BF16), o_c, small, seq)

    f_bias = jnp.pad(b_forget, (0, LANES - FOX_HEADS)).reshape(1, LANES)
    cum = _cum_forget(small, f_bias, seq, NSA_GROUPS)
    ck = cum[:, :FOX_HEADS].reshape(bsz, seq, FOX_HEADS).transpose(0, 2, 1).reshape(bsz * FOX_HEADS, 1, seq)
    o_fox, up_nsa_b, up_fox_b, w_out_b, w_ff1_b = _fox_attention(
        fq, fk, fv, ck, seq, casts=(w_up_nsa, w_up_fox, w_out, w_ff1))

    y = _merge(o_nsa, up_nsa_b, o_fox, up_fox_b, g_merge)
    tn_out = 1024
    res_specs = [pl.BlockSpec((tm, tn_out), lambda j, i: (i, j)),
                 pl.BlockSpec((None, 1, tn_out), lambda j, i: (i // per_m, 0, j))]
    x_mid = _matmul(y, w_out_b, _ep_residual, F32, extras=(x2, gate1), extra_specs=res_specs,
                    tn=tn_out, name="out_proj")

    h2 = _normmod(x_mid, norm2_g.reshape(1, d), scale2, shift2, seq)
    hid, w_ff2_b = _matmul(h2, w_ff1_b, _ep_relu2, BF16, casts=(w_ff2,), name="ff1")
    return _matmul_k_residual(hid, w_ff2_b, x_mid, gate2, seq, name="ff2")


def kernel(x, c, w_ada, b_ada, norm1_g, norm2_g, w_in, b_forget, nsa_q_norm, nsa_k_norm, fox_q_norm, fox_k_norm, cmp_pos_k, cmp_pos_v, w_cmp_k1, w_cmp_k2, w_cmp_v1, w_cmp_v2, w_up_nsa, w_up_fox, w_out, w_ff1, w_ff2):
    bsz, seq, d = x.shape
    x2 = x.reshape(bsz * seq, d)
    c_pad = jnp.pad(c, ((0, 8 - bsz), (0, 0)))
    params = (w_ada, b_ada, norm1_g, norm2_g, w_in, b_forget, nsa_q_norm, nsa_k_norm, fox_q_norm, fox_k_norm,
              cmp_pos_k, cmp_pos_v, w_cmp_k1, w_cmp_k2, w_cmp_v1, w_cmp_v2, w_up_nsa, w_up_fox, w_out,
              w_ff1, w_ff2)
    for layer in range(w_ada.shape[0]):
        x2 = _layer(x2, c_pad, bsz, seq, *[p[layer] for p in params])
    return x2.reshape(bsz, seq, d)
```

```python
import functools

import numpy as np
import jax
import jax.numpy as jnp
from jax import lax
from jax.experimental import pallas as pl
from jax.experimental.pallas import tpu as pltpu

F32 = jnp.float32
BF16 = jnp.bfloat16

HEAD_DIM = 128
NSA_HEADS = 16
NSA_GROUPS = 4
NSA_REP = NSA_HEADS // NSA_GROUPS
FOX_HEADS = 16
CMP_BLOCK = 32
CMP_STRIDE = 16
SEL_BLOCK = 64
SEL_SHIFT = SEL_BLOCK.bit_length() - 1
SEL_TOPN = 16
WINDOW = 512
ROPE_THETA = 500000.0
ROT_DIM = HEAD_DIM // 4
RMS_EPS = 1e-6
ATTN_SCALE = HEAD_DIM ** -0.5

LOG2E = float(np.log2(np.e))
ROPE_HALF = ROT_DIM // 2
MXU_COLS = 256
REGION_JOBS = 2


def _head_perm(x, axis):
    mid = HEAD_DIM // 2
    cuts = [(0, ROPE_HALF), (mid, mid + ROPE_HALF), (ROT_DIM, mid), (ROPE_HALF, ROT_DIM), (mid + ROPE_HALF, HEAD_DIM)]
    return jnp.concatenate([lax.slice_in_dim(x, a, b, axis=axis) for a, b in cuts], axis=axis)
NEG = -1e30
BIG = 1e30
LANES = 128
VMEM_LIMIT = 60 * 1024 * 1024


def _cparams(*sem):
    return pltpu.CompilerParams(dimension_semantics=sem, vmem_limit_bytes=VMEM_LIMIT)


def _dot(a, b):
    return jnp.dot(a, b, preferred_element_type=F32)


def _dot_nt(a, b):
    return lax.dot_general(a, b, (((1,), (1,)), ((), ())), preferred_element_type=F32)


def _ada_kernel(c_ref, w_ref, b_ref, o_ref):
    c = c_ref[...]
    s = (c * jax.nn.sigmoid(c)).astype(BF16)
    o_ref[...] = _dot(s, w_ref[...].astype(BF16)) + b_ref[...]


def _ada(c_pad, w, b):
    rows, d = c_pad.shape
    n = w.shape[1]
    tn = 1024
    return pl.pallas_call(
        _ada_kernel,
        grid=(n // tn,),
        in_specs=[pl.BlockSpec((rows, d), lambda j: (0, 0)),
                  pl.BlockSpec((d, tn), lambda j: (0, j)),
                  pl.BlockSpec((1, tn), lambda j: (0, j))],
        out_specs=pl.BlockSpec((rows, tn), lambda j: (0, j)),
        out_shape=jax.ShapeDtypeStruct((rows, n), F32),
        compiler_params=_cparams("arbitrary"),
        name="ada",
    )(c_pad, w, b)


def _normmod_kernel(x_ref, g_ref, sc_ref, sh_ref, o_ref):
    x = x_ref[...]
    ms = jnp.mean(x * x, axis=-1, keepdims=True)
    y = x * lax.rsqrt(ms + RMS_EPS) * g_ref[...]
    o_ref[...] = (y * (1.0 + sc_ref[...]) + sh_ref[...]).astype(o_ref.dtype)


def _normmod(x2, g, scale, shift, seq):
    m, d = x2.shape
    tr = 512
    per = seq // tr
    return pl.pallas_call(
        _normmod_kernel,
        grid=(m // tr,),
        in_specs=[pl.BlockSpec((tr, d), lambda i: (i, 0)),
                  pl.BlockSpec((1, d), lambda i: (0, 0)),
                  pl.BlockSpec((None, 1, d), lambda i: (i // per, 0, 0)),
                  pl.BlockSpec((None, 1, d), lambda i: (i // per, 0, 0))],
        out_specs=pl.BlockSpec((tr, d), lambda i: (i, 0)),
        out_shape=jax.ShapeDtypeStruct((m, d), BF16),
        compiler_params=_cparams("arbitrary"),
        name="normmod",
    )(x2, g, scale, shift)


def _ep_raw(acc, o_ref):
    o_ref[...] = acc.astype(o_ref.dtype)


def _ep_sigmoid(acc, o_ref):
    o_ref[...] = jax.nn.sigmoid(acc).astype(o_ref.dtype)


def _ep_relu2(acc, o_ref):
    r = jnp.maximum(acc, 0.0)
    o_ref[...] = (r * r).astype(o_ref.dtype)


def _ep_residual(acc, o_ref, x_ref, g_ref):
    o_ref[...] = x_ref[...] + g_ref[...] * acc


def _head_norm(a, g):
    ms = jnp.mean(a * a, axis=-1, keepdims=True)
    return a * lax.rsqrt(ms + RMS_EPS) * g


def _rope(a, c, s):
    return a * c + pltpu.roll(a, HEAD_DIM // 2, 1) * s


def _chunk_head_norm(acc, g_ref, ones_ref):
    ss = _dot((acc * acc).astype(BF16), ones_ref[...])
    return acc * lax.rsqrt(ss * (1.0 / HEAD_DIM) + RMS_EPS) * g_ref[...]


def _ep_headnorm(acc, o_ref, g_ref):
    g = g_ref[...]
    for h0 in range(0, acc.shape[1], HEAD_DIM):
        o_ref[:, h0:h0 + HEAD_DIM] = _head_norm(acc[:, h0:h0 + HEAD_DIM], g).astype(o_ref.dtype)


def _ep_headnorm_rope(acc, o_ref, g_ref, ones_ref, c_ref, s_ref):
    c, s = c_ref[...], s_ref[...]
    for c0 in range(0, acc.shape[1], MXU_COLS):
        y = _chunk_head_norm(acc[:, c0:c0 + MXU_COLS], g_ref, ones_ref)
        for h0 in range(0, MXU_COLS, HEAD_DIM):
            dst = slice(c0 + h0, c0 + h0 + HEAD_DIM)
            o_ref[:, dst] = _rope(y[:, h0:h0 + HEAD_DIM], c, s).astype(o_ref.dtype)


def _cast_blocks(refs_in, refs_out):
    for src, dst in zip(refs_in, refs_out):
        dst[...] = src[...].astype(dst.dtype)


def _cast_specs(casts, steps, index_of_step):
    specs, shapes = [], []
    for w in casts:
        rows, cols = w.shape
        assert rows % (steps * 16) == 0
        specs.append(pl.BlockSpec((rows // steps, cols), lambda *g: (index_of_step(*g), 0)))
        shapes.append(jax.ShapeDtypeStruct((rows, cols), BF16))
    return specs, shapes


def _mm_kernel(a_ref, b_ref, *rest, epilogue, n_extra, n_cast, transposed):
    extras = rest[:n_extra]
    cast_in = rest[n_extra:n_extra + n_cast]
    o_ref = rest[n_extra + n_cast]
    cast_out = rest[n_extra + n_cast + 1:]
    acc = _dot_nt(a_ref[...], b_ref[...]) if transposed else _dot(a_ref[...], b_ref[...])
    epilogue(acc, o_ref, *extras)
    _cast_blocks(cast_in, cast_out)


def _matmul(a, b, epilogue, out_dtype, extras=(), extra_specs=(), tm=1024, tn=1024, name="mm",
            casts=(), rows=None):
    m, k = a.shape
    if rows is None:
        n = b.shape[1]
        tn = min(tn, n)
        b_spec = pl.BlockSpec((k, tn), lambda j, i: (0, j))
    else:
        first, stride, count = rows
        assert first % 16 == 0 and stride % 16 == 0
        n = count * tn
        b_spec = pl.BlockSpec((pl.Element(tn), pl.Element(k)),
                              lambda j, i: (pl.multiple_of(first + j * stride, 16), 0))
    grid = (n // tn, m // tm)
    cast_specs, cast_shapes = _cast_specs(casts, grid[0] * grid[1], lambda j, i: j * grid[1] + i)
    out = pl.pallas_call(
        functools.partial(_mm_kernel, epilogue=epilogue, n_extra=len(extras), n_cast=len(casts),
                          transposed=rows is not None),
        grid=grid,
        in_specs=[pl.BlockSpec((tm, k), lambda j, i: (i, 0)), b_spec] + list(extra_specs) + cast_specs,
        out_specs=[pl.BlockSpec((tm, tn), lambda j, i: (i, j))] + cast_specs,
        out_shape=[jax.ShapeDtypeStruct((m, n), out_dtype)] + cast_shapes,
        compiler_params=_cparams("arbitrary", "arbitrary"),
        name=name,
    )(a, b, *extras, *casts)
    return out if casts else out[0]


def _mmk_kernel(a_ref, b_ref, x_ref, g_ref, o_ref):
    kk = pl.program_id(2)

    @pl.when(kk == 0)
    def _():
        o_ref[...] = x_ref[...] + g_ref[...] * _dot(a_ref[...], b_ref[...])

    @pl.when(kk > 0)
    def _():
        o_ref[...] += g_ref[...] * _dot(a_ref[...], b_ref[...])


def _matmul_k_residual(a, b, x, gate, seq, tm=1024, tn=1024, tk=4096, name="mmk"):
    m, k = a.shape
    n = b.shape[1]
    per = seq // tm
    nk = k // tk
    return pl.pallas_call(
        _mmk_kernel,
        grid=(m // tm, n // tn, nk),
        in_specs=[pl.BlockSpec((tm, tk), lambda i, j, kk: (i, kk)),
                  pl.BlockSpec((tk, tn), lambda i, j, kk: (kk, j)),
                  pl.BlockSpec((tm, tn), lambda i, j, kk: (i, j)),
                  pl.BlockSpec((None, 1, tn), lambda i, j, kk: (i // per, 0, j))],
        out_specs=pl.BlockSpec((tm, tn), lambda i, j, kk: (i, j)),
        out_shape=jax.ShapeDtypeStruct((m, n), F32),
        compiler_params=_cparams("arbitrary", "arbitrary", "arbitrary"),
        name=name,
    )(a, b, x, gate)


def _compress_kernel(xk_ref, xv_ref, pek_ref, pev_ref, w1k_ref, w2k_ref, w1v_ref, w2v_ref,
                     g_ref, c_ref, s_ref, ko_ref, vo_ref):
    half = CMP_STRIDE * HEAD_DIM
    n_rows = xk_ref.shape[0] // CMP_STRIDE

    def comp(x_ref, pe_ref, w1_ref, w2_ref):
        x = jnp.concatenate([x_ref[pl.ds(l, n_rows, stride=CMP_STRIDE), :] for l in range(CMP_STRIDE)], axis=1)
        lo = (x + pe_ref[0:1, :]).astype(BF16)
        hi = (x + pe_ref[1:2, :]).astype(BF16)
        p = _dot(lo, w1_ref[0:half, :])
        q = _dot(hi, w1_ref[half:2 * half, :])
        h = p + pltpu.roll(q, q.shape[0] - 1, 0)
        h = jax.nn.gelu(h, approximate=True)
        return _dot(h.astype(BF16), w2_ref[...])

    kc = comp(xk_ref, pek_ref, w1k_ref, w2k_ref)
    kc = _rope(_head_norm(kc, g_ref[...]), c_ref[...], s_ref[...])
    ko_ref[...] = kc.astype(ko_ref.dtype)
    vo_ref[...] = comp(xv_ref, pev_ref, w1v_ref, w2v_ref).astype(vo_ref.dtype)


def _compress(kv_c, seq, pek, pev, w1k, w2k, w1v, w2v, g, c, s):
    b = kv_c.shape[0] // seq
    nr = seq // CMP_STRIDE
    kspec = pl.BlockSpec((seq, HEAD_DIM), lambda i, j: (i, j))
    vspec = pl.BlockSpec((seq, HEAD_DIM), lambda i, j: (i, NSA_GROUPS + j))
    full = lambda arr: pl.BlockSpec(arr.shape, lambda i, j: (0,) * arr.ndim)
    ospec = pl.BlockSpec((None, None, nr, HEAD_DIM), lambda i, j: (i, j, 0, 0))
    oshape = jax.ShapeDtypeStruct((b, NSA_GROUPS, nr, HEAD_DIM), BF16)
    return pl.pallas_call(
        _compress_kernel,
        grid=(b, NSA_GROUPS),
        in_specs=[kspec, vspec, full(pek), full(pev), full(w1k), full(w2k), full(w1v), full(w2v),
                  full(g), full(c), full(s)],
        out_specs=[ospec, ospec],
        out_shape=[oshape, oshape],
        compiler_params=_cparams("arbitrary", "arbitrary"),
        name="compress",
    )(kv_c, kv_c, pek, pev, w1k, w2k, w1v, w2v, g, c, s)


def _cmp_kernel(q_ref, k_ref, v_ref, ovt_ref, oc_ref, sel_ref, *, tt, n_slc):
    t0 = pl.program_id(2) * tt
    t = t0 + lax.broadcasted_iota(jnp.int32, (tt, LANES), 0)
    n = lax.broadcasted_iota(jnp.int32, (tt, LANES), 1)
    valid = n * CMP_STRIDE + (CMP_BLOCK - 1) <= t
    k = k_ref[...]
    v = v_ref[...]
    psum = jnp.zeros((tt, LANES), F32)
    for r in range(NSA_REP):
        sl = slice(r * HEAD_DIM, (r + 1) * HEAD_DIM)
        s = jnp.where(valid, _dot_nt(q_ref[:, sl], k), NEG)
        m = jnp.max(s, axis=-1, keepdims=True)
        p = jnp.where(valid, jnp.exp2(s - m), 0.0)
        d = jnp.sum(p, axis=-1, keepdims=True)
        p = p / jnp.where(d > 0, d, 1.0)
        psum = psum + p
        oc_ref[:, sl] = _dot(p.astype(BF16), v)
    imp = lax.dot_general(ovt_ref[...], psum, (((1,), (1,)), ((), ())), preferred_element_type=F32,
                          precision=lax.Precision.HIGHEST)[0:n_slc, :]
    j = lax.broadcasted_iota(jnp.int32, (n_slc, tt), 0)
    cur = (t0 + lax.broadcasted_iota(jnp.int32, (n_slc, tt), 1)) >> SEL_SHIFT
    score = jnp.where(j <= cur, imp, -BIG)
    score = jnp.where(j == 0, BIG, jnp.where(j == cur, BIG, jnp.where(j == cur - 1, BIG, score)))
    cnt = jnp.zeros((n_slc, tt), F32)
    for jp in range(n_slc):
        row = score[jp:jp + 1, :]
        ge = jnp.where(row >= score, 1.0, 0.0)
        gt = jnp.where(row > score, 1.0, 0.0)
        cnt = cnt + jnp.where(j > jp, ge, gt)
    sel = jnp.where(cnt < SEL_TOPN, jnp.where(score > -0.5 * BIG, 1.0, 0.0), 0.0)
    sel = jnp.concatenate([sel, jnp.zeros((LANES - n_slc, tt), F32)], axis=0)
    sel_ref[...] = sel.T.astype(sel_ref.dtype)


def _cmp_attention(q, kc, vc, overlap, seq):
    m, _ = q.shape
    b = m // seq
    tt = 2048
    per = seq // tt
    n_cmp =(seq - CMP_BLOCK) // CMP_STRIDE + 1
    assert n_cmp <= LANES and n_cmp * CMP_STRIDE + CMP_BLOCK - 1 >= seq
    kspec = pl.BlockSpec((None, None, LANES, HEAD_DIM), lambda i, g, j: (i, g, 0, 0))
    return pl.pallas_call(
        functools.partial(_cmp_kernel, tt=tt, n_slc=seq // SEL_BLOCK),
        grid=(b, NSA_GROUPS, per),
        in_specs=[pl.BlockSpec((tt, NSA_REP * HEAD_DIM), lambda i, g, j: (i * per + j, g)),
                  kspec, kspec,
                  pl.BlockSpec((LANES, LANES), lambda i, g, j: (0, 0))],
        out_specs=[pl.BlockSpec((tt, NSA_REP * HEAD_DIM), lambda i, g, j: (i * per + j, g)),
                   pl.BlockSpec((None, None, tt, LANES), lambda i, g, j: (i, g, j, 0))],
        out_shape=[jax.ShapeDtypeStruct((m, NSA_HEADS * HEAD_DIM), F32),
                   jax.ShapeDtypeStruct((b, NSA_GROUPS, seq, LANES), BF16)],
        compiler_params=_cparams("arbitrary", "arbitrary", "arbitrary"),
        name="cmp_attn",
    )(q, kc, vc, overlap)


def _softmax_pv(t, v):
    m = jnp.max(t, axis=-1, keepdims=True)
    p = jnp.exp2(t - m)
    l = jnp.sum(p, axis=-1, keepdims=True)
    return _dot(p.astype(BF16), v) / l


def _tri_bias(n, lower):
    row = lax.broadcasted_iota(jnp.int32, (n, n), 0)
    col = lax.broadcasted_iota(jnp.int32, (n, n), 1)
    return jnp.where(col <= row, 0.0, NEG) if lower else jnp.where(col > row, 0.0, NEG)


def _nsa_kernel(q_ref, ks_ref, vs_ref, kw_ref, vw_ref, sel_ref, ex_ref, oc_ref, gl_ref, o_ref,
                t_sc, p_sc, l_sc, bs_sc, bw_sc, os_sc, *, tq):
    seq = q_ref.shape[0]
    live = pl.program_id(0) >= 0
    n_back = WINDOW // tq
    w_cols = (n_back + 1) * tq
    bw_sc[:, 0:tq] = _tri_bias(tq, False)
    bw_sc[:, tq:n_back * tq] = jnp.zeros((tq, (n_back - 1) * tq), F32)
    bw_sc[:, n_back * tq:w_cols] = _tri_bias(tq, True)

    jobs = []
    for qt in reversed(range(seq // tq)):
        lo, hi = qt * tq, (qt + 1) * tq
        jobs.append(("sel", lo, hi, 0))
        jobs.append(("win", lo, hi, max(lo - WINDOW, 0)))

    assert REGION_JOBS % 2 == 0 and len(jobs) % REGION_JOBS == 0
    n_slots = 2 * REGION_JOBS

    def bias_slot(j):
        return (j // 2) % REGION_JOBS

    def head_rows(r):
        return slice(r * tq, (r + 1) * tq)

    def stage_scores(j):
        kind, lo, hi, klo = jobs[j]
        slot = j % n_slots
        n = hi - klo
        q4 = jnp.concatenate([q_ref[lo:hi, r * HEAD_DIM:(r + 1) * HEAD_DIM] for r in range(NSA_REP)], axis=0)
        k_ref = ks_ref if kind == "sel" else kw_ref
        t_sc[slot, :, 0:n] = _dot_nt(q4, k_ref[klo:hi, :])
        if kind == "sel":
            picked = _dot(sel_ref[lo:hi, :], ex_ref[:, 0:hi])
            bias = jnp.where(picked > 0.5, 0.0, NEG)
            if lo > 0:
                bs_sc[bias_slot(j), :, 0:lo] = bias[:, 0:lo]
            bs_sc[bias_slot(j), :, lo:hi] = bias[:, lo:hi] + _tri_bias(tq, True)

    def stage_softmax(j):
        kind, lo, hi, klo = jobs[j]
        slot = j % n_slots
        n = hi - klo
        for r in range(NSA_REP):
            bias = bs_sc[bias_slot(j), :, 0:n] if kind == "sel" else bw_sc[:, w_cols - n:w_cols]
            t = t_sc[slot, head_rows(r), 0:n] + bias
            p = jnp.exp2(t - jnp.max(t, axis=-1, keepdims=True))
            l_sc[slot, head_rows(r), :] = jnp.sum(p, axis=-1, keepdims=True)
            p_sc[slot, head_rows(r), 0:n] = p.astype(BF16)

    def stage_output(j):
        kind, lo, hi, klo = jobs[j]
        slot = j % n_slots
        n = hi - klo
        v_ref = vs_ref if kind == "sel" else vw_ref
        o4 = _dot(p_sc[slot, :, 0:n], v_ref[klo:hi, :]) / l_sc[slot]
        if kind == "sel":
            os_sc[...] = o4
            return
        gate = jax.nn.sigmoid(gl_ref[lo:hi, :])
        for r in range(NSA_REP):
            sl = slice(r * HEAD_DIM, (r + 1) * HEAD_DIM)
            o = (gate[:, 3 * r:3 * r + 1] * oc_ref[lo:hi, sl]
                 + gate[:, 3 * r + 1:3 * r + 2] * os_sc[head_rows(r), :]
                 + gate[:, 3 * r + 2:3 * r + 3] * o4[head_rows(r), :])
            o_ref[lo:hi, sl] = o.astype(o_ref.dtype)

    for r0 in range(0, len(jobs) + 2 * REGION_JOBS, REGION_JOBS):
        @pl.when(live)
        def _region():
            for j in range(r0, r0 + REGION_JOBS):
                if j < len(jobs):
                    stage_scores(j)
                if 0 <= j - REGION_JOBS < len(jobs):
                    stage_softmax(j - REGION_JOBS)
                if 0 <= j - 2 * REGION_JOBS < len(jobs):
                    stage_output(j - 2 * REGION_JOBS)


def _nsa_attention(q, kk, vv, sel, expand, oc, gl, seq):
    m = q.shape[0]
    b = m // seq
    width = NSA_REP * HEAD_DIM
    qspec = pl.BlockSpec((seq, width), lambda i, g: (i, g))
    ks = pl.BlockSpec((seq, HEAD_DIM), lambda i, g: (i, g))
    kw = pl.BlockSpec((seq, HEAD_DIM), lambda i, g: (i, NSA_GROUPS + g))
    tq = 128
    rows = NSA_REP * tq
    return pl.pallas_call(
        functools.partial(_nsa_kernel, tq=tq),
        grid=(b, NSA_GROUPS),
        scratch_shapes=[pltpu.VMEM((2 * REGION_JOBS, rows, seq), F32), pltpu.VMEM((2 * REGION_JOBS, rows, seq), BF16),
                        pltpu.VMEM((2 * REGION_JOBS, rows, 1), F32), pltpu.VMEM((REGION_JOBS, tq, seq), F32),
                        pltpu.VMEM((tq, WINDOW + tq), F32), pltpu.VMEM((rows, HEAD_DIM), F32)],
        in_specs=[qspec, ks, ks, kw, kw,
                  pl.BlockSpec((None, None, seq, LANES), lambda i, g: (i, g, 0, 0)),
                  pl.BlockSpec((LANES, seq), lambda i, g: (0, 0)),
                  qspec,
                  pl.BlockSpec((seq, LANES), lambda i, g: (i, g))],
        out_specs=qspec,
        out_shape=jax.ShapeDtypeStruct((m, NSA_HEADS * HEAD_DIM), BF16),
        compiler_params=_cparams("arbitrary", "arbitrary"),
        name="nsa_attn",
    )(q, kk, vv, kk, vv, sel, expand, oc, gl)


def _cum_kernel(f_ref, b_ref, o_ref):
    x = f_ref[...] + b_ref[...]
    ls = jnp.minimum(x, 0.0) - jnp.log1p(jnp.exp(-jnp.abs(x)))
    row = lax.broadcasted_iota(jnp.int32, ls.shape, 0)
    d = 1
    while d < ls.shape[0]:
        ls = ls + jnp.where(row >= d, pltpu.roll(ls, d, 0), 0.0)
        d *= 2
    o_ref[...] = ls


def _cum_forget(fl, bias, seq, col_block):
    m = fl.shape[0]
    return pl.pallas_call(
        _cum_kernel,
        grid=(m // seq,),
        in_specs=[pl.BlockSpec((seq, LANES), lambda i: (i, col_block)),
                  pl.BlockSpec((1, LANES), lambda i: (0, 0))],
        out_specs=pl.BlockSpec((seq, LANES), lambda i: (i, 0)),
        out_shape=jax.ShapeDtypeStruct((m, LANES), F32),
        compiler_params=_cparams("arbitrary"),
        name="cum_forget",
    )(fl, bias)


def _fox_kernel(q_ref, k_ref, v_ref, ck_ref, *rest, tq):
    n_cast = (len(rest) - 1) // 2
    o_ref = rest[n_cast]
    _cast_blocks(rest[:n_cast], rest[n_cast + 1:])
    seq = q_ref.shape[0]
    causal_b = _tri_bias(tq, True)
    ck = ck_ref[...] * LOG2E
    for qt in reversed(range(seq // tq)):
        lo, hi = qt * tq, (qt + 1) * tq
        t = _dot_nt(q_ref[lo:hi, :], k_ref[0:hi, :]) - ck[:, 0:hi]
        diag = t[:, lo:hi] + causal_b
        t = diag if qt == 0 else jnp.concatenate([t[:, 0:lo], diag], axis=1)
        o_ref[lo:hi, :] = _softmax_pv(t, v_ref[0:hi, :]).astype(o_ref.dtype)


def _fox_attention(q, k, v, ck, seq, casts=()):
    m = q.shape[0]
    b = m // seq
    blk = pl.BlockSpec((seq, HEAD_DIM), lambda i, h: (i, h))
    cast_specs, cast_shapes = _cast_specs(casts, b * FOX_HEADS, lambda i, h: i * FOX_HEADS + h)
    out = pl.pallas_call(
        functools.partial(_fox_kernel, tq=512),
        grid=(b, FOX_HEADS),
        in_specs=[blk, blk, blk,
                  pl.BlockSpec((None, 1, seq), lambda i, h: (i * FOX_HEADS + h, 0, 0))] + cast_specs,
        out_specs=[blk] + cast_specs,
        out_shape=[jax.ShapeDtypeStruct((m, FOX_HEADS * HEAD_DIM), BF16)] + cast_shapes,
        compiler_params=_cparams("arbitrary", "arbitrary"),
        name="fox_attn",
    )(q, k, v, ck, *casts)
    return out if casts else out[0]


def _merge_kernel(a1_ref, w1_ref, a2_ref, w2_ref, g1_ref, g2_ref, *rest):
    n_cast = (len(rest) - 1) // 2
    o_ref = rest[n_cast]
    u1 = _dot(a1_ref[...], w1_ref[...])
    u2 = _dot(a2_ref[...], w2_ref[...])
    o_ref[...] = (g1_ref[...].astype(F32) * u1 + g2_ref[...].astype(F32) * u2).astype(o_ref.dtype)
    _cast_blocks(rest[:n_cast], rest[n_cast + 1:])


def _merge(a1, w1, a2, w2, gates, casts=()):
    m, k = a1.shape
    n = w1.shape[1]
    tm, tn = 1024, 1024
    nb = n // tn
    mb = m // tm
    aspec = pl.BlockSpec((tm, k), lambda j, i: (i, 0))
    wspec = pl.BlockSpec((k, tn), lambda j, i: (0, j))
    cast_specs, cast_shapes = _cast_specs(casts, nb * mb, lambda j, i: j * mb + i)
    out = pl.pallas_call(
        _merge_kernel,
        grid=(nb, mb),
        in_specs=[aspec, wspec, aspec, wspec,
                  pl.BlockSpec((tm, tn), lambda j, i: (i, j)),
                  pl.BlockSpec((tm, tn), lambda j, i: (i, nb + j))] + cast_specs,
        out_specs=[pl.BlockSpec((tm, tn), lambda j, i: (i, j))] + cast_specs,
        out_shape=[jax.ShapeDtypeStruct((m, n), BF16)] + cast_shapes,
        compiler_params=_cparams("arbitrary", "arbitrary"),
        name="merge",
    )(a1, w1, a2, w2, gates, gates, *casts)
    return out if casts else out[0]


def _rope_tables(pos):
    inv = ROPE_THETA ** (-jnp.arange(ROPE_HALF, dtype=F32) / ROPE_HALF)
    ang = jnp.asarray(pos, dtype=F32)[:, None] * inv[None, :]
    cos, sin = jnp.cos(ang), jnp.sin(ang)
    n = ang.shape[0]
    gap = HEAD_DIM // 2 - ROPE_HALF
    c = jnp.concatenate([cos, jnp.ones((n, gap), F32), cos, jnp.ones((n, gap), F32)], axis=1)
    s = jnp.concatenate([-sin, jnp.zeros((n, gap), F32), sin, jnp.zeros((n, gap), F32)], axis=1)
    return c, s


def _layer(x2, c_pad, bsz, seq, w_ada, b_ada, norm1_g, norm2_g, w_in, b_forget, nsa_q_norm, nsa_k_norm,
           fox_q_norm, fox_k_norm, cmp_pos_k, cmp_pos_v, w_cmp_k1, w_cmp_k2, w_cmp_v1, w_cmp_v2,
           w_up_nsa, w_up_fox, w_out, w_ff1, w_ff2):
    d = x2.shape[1]
    hd = HEAD_DIM
    tm = 1024
    per_m = seq // tm

    mod = _ada(c_pad, w_ada, b_ada.reshape(1, -1))[:bsz]
    shift1, scale1, gate1, shift2, scale2, gate2 = [t.reshape(bsz, 1, d) for t in jnp.split(mod, 6, axis=-1)]

    h = _normmod(x2, norm1_g.reshape(1, d), scale1, shift1, seq)

    o_q = 0
    o_kv = o_q + NSA_HEADS * hd
    o_gate = o_kv + 3 * 2 * NSA_GROUPS * hd
    o_fox = o_gate + 3 * NSA_HEADS
    o_f = o_fox + 3 * FOX_HEADS * hd
    o_merge = o_f + FOX_HEADS
    gw = NSA_GROUPS * hd

    w_t = w_in.T.astype(BF16)
    fw = FOX_HEADS * hd

    pos = np.arange(seq)
    c_t, s_t = _rope_tables(pos)
    tab_specs = [pl.BlockSpec((tm, hd), lambda j, i: (i % per_m, 0))] * 2
    gspec = pl.BlockSpec((1, MXU_COLS), lambda j, i: (0, 0))
    hspec = pl.BlockSpec((1, hd), lambda j, i: (0, 0))
    ones_spec = pl.BlockSpec((MXU_COLS, MXU_COLS), lambda j, i: (0, 0))
    head_ones = jnp.asarray(np.kron(np.eye(MXU_COLS // hd), np.ones((hd, hd))), dtype=BF16)

    def chunk_gain(g):
        return jnp.tile(g.reshape(1, hd), (1, MXU_COLS // hd))

    q_scale = ATTN_SCALE * LOG2E
    def permuted_heads(lo, n_heads):
        return _head_perm(w_t[lo:lo + n_heads * hd].reshape(n_heads, hd, d), 1).reshape(n_heads * hd, d)
    w_qk = jnp.concatenate([permuted_heads(o_q, NSA_HEADS), permuted_heads(o_kv + 2 * gw, NSA_GROUPS),
                            permuted_heads(o_kv + 4 * gw, NSA_GROUPS)], axis=0)
    g_q = chunk_gain(q_scale * _head_perm(nsa_q_norm, 0))
    g_k = chunk_gain(_head_perm(nsa_k_norm, 0))
    rope_specs = [gspec, ones_spec] + tab_specs
    q_n = _matmul(h, w_qk, _ep_headnorm_rope, BF16, rows=(0, 1024, 2),
                  extras=(g_q, head_ones, c_t, s_t), extra_specs=rope_specs, name="proj_nsa_q")
    k_sw = _matmul(h, w_qk, _ep_headnorm_rope, BF16, rows=(NSA_HEADS * hd, 0, 1),
                   extras=(g_k, head_ones, c_t, s_t), extra_specs=rope_specs, name="proj_nsa_k")
    v_sw = _matmul(h, w_t, _ep_raw, BF16, tn=gw, rows=(o_kv + 3 * gw, 2 * gw, 2), name="proj_nsa_v")
    kv_c = _matmul(h, w_t, _ep_raw, F32, rows=(o_kv, 2 * gw, 1), name="proj_nsa_cmp")
    fq = _matmul(h, w_t, _ep_headnorm, BF16, rows=(o_fox, 1024, 2),
                 extras=(q_scale * fox_q_norm.reshape(1, hd),), extra_specs=[hspec], name="proj_fox_q")
    fk = _matmul(h, w_t, _ep_headnorm, BF16, rows=(o_fox + fw, 1024, 2),
                 extras=(fox_k_norm.reshape(1, hd),), extra_specs=[hspec], name="proj_fox_k")
    fv = _matmul(h, w_t, _ep_raw, BF16, rows=(o_fox + 2 * fw, 1024, 2), name="proj_fox_v")
    g_merge = _matmul(h, w_t, _ep_sigmoid, BF16, rows=(o_merge, 1024, 2 * d // 1024), name="proj_merge")

    per_group = NSA_REP * 3
    gate_blocks = [jnp.pad(w_t[o_gate + g * per_group:o_gate + (g + 1) * per_group],
                           ((0, LANES - per_group), (0, 0))) for g in range(NSA_GROUPS)]
    f_block = jnp.pad(w_t[o_f:o_merge], ((0, LANES - FOX_HEADS), (0, 0)))
    w_small = jnp.concatenate(gate_blocks + [f_block], axis=0)
    small = _matmul(h, w_small, _ep_raw, F32, tn=w_small.shape[0], rows=(0, 0, 1), name="proj_small")

    n_rows = seq // CMP_STRIDE
    end_pos = np.arange(n_rows) * CMP_STRIDE + CMP_BLOCK - 1
    ce, se = _rope_tables(end_pos)
    k_cmp, v_cmp = _compress(
        kv_c, seq, cmp_pos_k.reshape(2, CMP_STRIDE * hd), cmp_pos_v.reshape(2, CMP_STRIDE * hd),
        w_cmp_k1.astype(BF16), _head_perm(w_cmp_k2, 1).astype(BF16), w_cmp_v1.astype(BF16), w_cmp_v2.astype(BF16),
        _head_perm(nsa_k_norm, 0).reshape(1, hd), ce, se)

    n_slc = seq // SEL_BLOCK
    ci = np.arange(LANES)[:, None] * CMP_STRIDE
    sj = np.arange(LANES)[None, :] * SEL_BLOCK
    overlap = ((ci < sj + SEL_BLOCK) & (ci + CMP_BLOCK > sj) & (np.arange(LANES)[None, :] < n_slc)
               & (np.arange(LANES)[:, None] < n_rows - 1)).astype(np.float32)
    o_c, sel = _cmp_attention(q_n, k_cmp, v_cmp, jnp.asarray(overlap.T), seq)

    expand = (np.arange(LANES)[:, None] == (np.arange(seq)[None, :] >> SEL_SHIFT)).astype(np.float32)
    o_nsa = _nsa_attention(q_n, k_sw, v_sw, sel, jnp.asarray(expand, dtype=BF16), o_c, small, seq)

    f_bias = jnp.pad(b_forget, (0, LANES - FOX_HEADS)).reshape(1, LANES)
    cum = _cum_forget(small, f_bias, seq, NSA_GROUPS)
    ck = cum[:, :FOX_HEADS].reshape(bsz, seq, FOX_HEADS).transpose(0, 2, 1).reshape(bsz * FOX_HEADS, 1, seq)
    o_fox, up_nsa_b, up_fox_b, w_out_b, w_ff1_b = _fox_attention(
        fq, fk, fv, ck, seq, casts=(w_up_nsa, w_up_fox, w_out, w_ff1))

    y = _merge(o_nsa, up_nsa_b, o_fox, up_fox_b, g_merge)
    tn_out = 1024
    res_specs = [pl.BlockSpec((tm, tn_out), lambda j, i: (i, j)),
                 pl.BlockSpec((None, 1, tn_out), lambda j, i: (i // per_m, 0, j))]
    x_mid = _matmul(y, w_out_b, _ep_residual, F32, extras=(x2, gate1), extra_specs=res_specs,
                    tn=tn_out, name="out_proj")

    h2 = _normmod(x_mid, norm2_g.reshape(1, d), scale2, shift2, seq)
    hid, w_ff2_b = _matmul(h2, w_ff1_b, _ep_relu2, BF16, casts=(w_ff2,), name="ff1")
    return _matmul_k_residual(hid, w_ff2_b, x_mid, gate2, seq, name="ff2")


def kernel(x, c, w_ada, b_ada, norm1_g, norm2_g, w_in, b_forget, nsa_q_norm, nsa_k_norm, fox_q_norm, fox_k_norm, cmp_pos_k, cmp_pos_v, w_cmp_k1, w_cmp_k2, w_cmp_v1, w_cmp_v2, w_up_nsa, w_up_fox, w_out, w_ff1, w_ff2):
    bsz, seq, d = x.shape
    x2 = x.reshape(bsz * seq, d)
    c_pad = jnp.pad(c, ((0, 8 - bsz), (0, 0)))
    params = (w_ada, b_ada, norm1_g, norm2_g, w_in, b_forget, nsa_q_norm, nsa_k_norm, fox_q_norm, fox_k_norm,
              cmp_pos_k, cmp_pos_v, w_cmp_k1, w_cmp_k2, w_cmp_v1, w_cmp_v2, w_up_nsa, w_up_fox, w_out,
              w_ff1, w_ff2)
    for layer in range(w_ada.shape[0]):
        x2 = _layer(x2, c_pad, bsz, seq, *[p[layer] for p in params])
    return x2.reshape(bsz, seq, d)
```

```python
import functools

import numpy as np
import jax
import jax.numpy as jnp
from jax import lax
from jax.experimental import pallas as pl
from jax.experimental.pallas import tpu as pltpu

F32 = jnp.float32
BF16 = jnp.bfloat16

HEAD_DIM = 128
NSA_HEADS = 16
NSA_GROUPS = 4
NSA_REP = NSA_HEADS // NSA_GROUPS
FOX_HEADS = 16
CMP_BLOCK = 32
CMP_STRIDE = 16
SEL_BLOCK = 64
SEL_SHIFT = SEL_BLOCK.bit_length() - 1
SEL_TOPN = 16
WINDOW = 512
ROPE_THETA = 500000.0
ROT_DIM = HEAD_DIM // 4
RMS_EPS = 1e-6
ATTN_SCALE = HEAD_DIM ** -0.5

LOG2E = float(np.log2(np.e))
ROPE_HALF = ROT_DIM // 2
MXU_COLS = 256
REGION_JOBS = 2


def _head_perm(x, axis):
    mid = HEAD_DIM // 2
    cuts = [(0, ROPE_HALF), (mid, mid + ROPE_HALF), (ROT_DIM, mid), (ROPE_HALF, ROT_DIM), (mid + ROPE_HALF, HEAD_DIM)]
    return jnp.concatenate([lax.slice_in_dim(x, a, b, axis=axis) for a, b in cuts], axis=axis)
NEG = -1e30
BIG = 1e30
LANES = 128
VMEM_LIMIT = 60 * 1024 * 1024


def _cparams(*sem):
    return pltpu.CompilerParams(dimension_semantics=sem, vmem_limit_bytes=VMEM_LIMIT)


def _dot(a, b):
    return jnp.dot(a, b, preferred_element_type=F32)


def _dot_nt(a, b):
    return lax.dot_general(a, b, (((1,), (1,)), ((), ())), preferred_element_type=F32)


def _ada_kernel(c_ref, w_ref, b_ref, o_ref):
    c = c_ref[...]
    s = (c * jax.nn.sigmoid(c)).astype(BF16)
    o_ref[...] = _dot(s, w_ref[...].astype(BF16)) + b_ref[...]


def _ada(c_pad, w, b):
    rows, d = c_pad.shape
    n = w.shape[1]
    tn = 1024
    return pl.pallas_call(
        _ada_kernel,
        grid=(n // tn,),
        in_specs=[pl.BlockSpec((rows, d), lambda j: (0, 0)),
                  pl.BlockSpec((d, tn), lambda j: (0, j)),
                  pl.BlockSpec((1, tn), lambda j: (0, j))],
        out_specs=pl.BlockSpec((rows, tn), lambda j: (0, j)),
        out_shape=jax.ShapeDtypeStruct((rows, n), F32),
        compiler_params=_cparams("arbitrary"),
        name="ada",
    )(c_pad, w, b)


def _normmod_kernel(x_ref, g_ref, sc_ref, sh_ref, o_ref):
    x = x_ref[...]
    ms = jnp.mean(x * x, axis=-1, keepdims=True)
    y = x * lax.rsqrt(ms + RMS_EPS) * g_ref[...]
    o_ref[...] = (y * (1.0 + sc_ref[...]) + sh_ref[...]).astype(o_ref.dtype)


def _normmod(x2, g, scale, shift, seq):
    m, d = x2.shape
    tr = 512
    per = seq // tr
    return pl.pallas_call(
        _normmod_kernel,
        grid=(m // tr,),
        in_specs=[pl.BlockSpec((tr, d), lambda i: (i, 0)),
                  pl.BlockSpec((1, d), lambda i: (0, 0)),
                  pl.BlockSpec((None, 1, d), lambda i: (i // per, 0, 0)),
                  pl.BlockSpec((None, 1, d), lambda i: (i // per, 0, 0))],
        out_specs=pl.BlockSpec((tr, d), lambda i: (i, 0)),
        out_shape=jax.ShapeDtypeStruct((m, d), BF16),
        compiler_params=_cparams("arbitrary"),
        name="normmod",
    )(x2, g, scale, shift)


def _ep_raw(acc, o_ref):
    o_ref[...] = acc.astype(o_ref.dtype)


def _ep_sigmoid(acc, o_ref):
    o_ref[...] = jax.nn.sigmoid(acc).astype(o_ref.dtype)


def _ep_relu2(acc, o_ref):
    r = jnp.maximum(acc, 0.0)
    o_ref[...] = (r * r).astype(o_ref.dtype)


def _ep_residual(acc, o_ref, x_ref, g_ref):
    o_ref[...] = x_ref[...] + g_ref[...] * acc


def _head_norm(a, g):
    ms = jnp.mean(a * a, axis=-1, keepdims=True)
    return a * lax.rsqrt(ms + RMS_EPS) * g


def _rope(a, c, s):
    return a * c + pltpu.roll(a, HEAD_DIM // 2, 1) * s


def _chunk_head_norm(acc, g_ref, ones_ref):
    ss = _dot((acc * acc).astype(BF16), ones_ref[...])
    return acc * lax.rsqrt(ss * (1.0 / HEAD_DIM) + RMS_EPS) * g_ref[...]


def _ep_headnorm(acc, o_ref, g_ref):
    g = g_ref[...]
    for h0 in range(0, acc.shape[1], HEAD_DIM):
        o_ref[:, h0:h0 + HEAD_DIM] = _head_norm(acc[:, h0:h0 + HEAD_DIM], g).astype(o_ref.dtype)


def _ep_headnorm_rope(acc, o_ref, g_ref, ones_ref, c_ref, s_ref):
    c, s = c_ref[...], s_ref[...]
    for c0 in range(0, acc.shape[1], MXU_COLS):
        y = _chunk_head_norm(acc[:, c0:c0 + MXU_COLS], g_ref, ones_ref)
        for h0 in range(0, MXU_COLS, HEAD_DIM):
            dst = slice(c0 + h0, c0 + h0 + HEAD_DIM)
            o_ref[:, dst] = _rope(y[:, h0:h0 + HEAD_DIM], c, s).astype(o_ref.dtype)


def _cast_blocks(refs_in, refs_out):
    for src, dst in zip(refs_in, refs_out):
        dst[...] = src[...].astype(dst.dtype)


def _cast_specs(casts, steps, index_of_step):
    specs, shapes = [], []
    for w in casts:
        rows, cols = w.shape
        assert rows % (steps * 16) == 0
        specs.append(pl.BlockSpec((rows // steps, cols), lambda *g: (index_of_step(*g), 0)))
        shapes.append(jax.ShapeDtypeStruct((rows, cols), BF16))
    return specs, shapes


def _mm_kernel(a_ref, b_ref, *rest, epilogue, n_extra, n_cast, transposed):
    extras = rest[:n_extra]
    cast_in = rest[n_extra:n_extra + n_cast]
    o_ref = rest[n_extra + n_cast]
    cast_out = rest[n_extra + n_cast + 1:]
    acc = _dot_nt(a_ref[...], b_ref[...]) if transposed else _dot(a_ref[...], b_ref[...])
    epilogue(acc, o_ref, *extras)
    _cast_blocks(cast_in, cast_out)


def _matmul(a, b, epilogue, out_dtype, extras=(), extra_specs=(), tm=1024, tn=1024, name="mm",
            casts=(), rows=None):
    m, k = a.shape
    if rows is None:
        n = b.shape[1]
        tn = min(tn, n)
        b_spec = pl.BlockSpec((k, tn), lambda j, i: (0, j))
    else:
        first, stride, count = rows
        assert first % 16 == 0 and stride % 16 == 0
        n = count * tn
        b_spec = pl.BlockSpec((pl.Element(tn), pl.Element(k)),
                              lambda j, i: (pl.multiple_of(first + j * stride, 16), 0))
    grid = (n // tn, m // tm)
    cast_specs, cast_shapes = _cast_specs(casts, grid[0] * grid[1], lambda j, i: j * grid[1] + i)
    out = pl.pallas_call(
        functools.partial(_mm_kernel, epilogue=epilogue, n_extra=len(extras), n_cast=len(casts),
                          transposed=rows is not None),
        grid=grid,
        in_specs=[pl.BlockSpec((tm, k), lambda j, i: (i, 0)), b_spec] + list(extra_specs) + cast_specs,
        out_specs=[pl.BlockSpec((tm, tn), lambda j, i: (i, j))] + cast_specs,
        out_shape=[jax.ShapeDtypeStruct((m, n), out_dtype)] + cast_shapes,
        compiler_params=_cparams("arbitrary", "arbitrary"),
        name=name,
    )(a, b, *extras, *casts)
    return out if casts else out[0]


def _mmk_kernel(a_ref, b_ref, x_ref, g_ref, o_ref):
    kk = pl.program_id(2)

    @pl.when(kk == 0)
    def _():
        o_ref[...] = x_ref[...] + g_ref[...] * _dot(a_ref[...], b_ref[...])

    @pl.when(kk > 0)
    def _():
        o_ref[...] += g_ref[...] * _dot(a_ref[...], b_ref[...])


def _matmul_k_residual(a, b, x, gate, seq, tm=1024, tn=1024, tk=4096, name="mmk"):
    m, k = a.shape
    n = b.shape[1]
    per = seq // tm
    nk = k // tk
    return pl.pallas_call(
        _mmk_kernel,
        grid=(m // tm, n // tn, nk),
        in_specs=[pl.BlockSpec((tm, tk), lambda i, j, kk: (i, kk)),
                  pl.BlockSpec((tk, tn), lambda i, j, kk: (kk, j)),
                  pl.BlockSpec((tm, tn), lambda i, j, kk: (i, j)),
                  pl.BlockSpec((None, 1, tn), lambda i, j, kk: (i // per, 0, j))],
        out_specs=pl.BlockSpec((tm, tn), lambda i, j, kk: (i, j)),
        out_shape=jax.ShapeDtypeStruct((m, n), F32),
        compiler_params=_cparams("arbitrary", "arbitrary", "arbitrary"),
        name=name,
    )(a, b, x, gate)


def _compress_kernel(xk_ref, xv_ref, pek_ref, pev_ref, w1k_ref, w2k_ref, w1v_ref, w2v_ref,
                     g_ref, c_ref, s_ref, ko_ref, vo_ref):
    half = CMP_STRIDE * HEAD_DIM
    n_rows = xk_ref.shape[0] // CMP_STRIDE

    def comp(x_ref, pe_ref, w1_ref, w2_ref):
        x = jnp.concatenate([x_ref[pl.ds(l, n_rows, stride=CMP_STRIDE), :] for l in range(CMP_STRIDE)], axis=1)
        lo = (x + pe_ref[0:1, :]).astype(BF16)
        hi = (x + pe_ref[1:2, :]).astype(BF16)
        p = _dot(lo, w1_ref[0:half, :])
        q = _dot(hi, w1_ref[half:2 * half, :])
        h = p + pltpu.roll(q, q.shape[0] - 1, 0)
        h = jax.nn.gelu(h, approximate=True)
        return _dot(h.astype(BF16), w2_ref[...])

    kc = comp(xk_ref, pek_ref, w1k_ref, w2k_ref)
    kc = _rope(_head_norm(kc, g_ref[...]), c_ref[...], s_ref[...])
    ko_ref[...] = kc.astype(ko_ref.dtype)
    vo_ref[...] = comp(xv_ref, pev_ref, w1v_ref, w2v_ref).astype(vo_ref.dtype)


def _compress(kv_c, seq, pek, pev, w1k, w2k, w1v, w2v, g, c, s):
    b = kv_c.shape[0] // seq
    nr = seq // CMP_STRIDE
    kspec = pl.BlockSpec((seq, HEAD_DIM), lambda i, j: (i, j))
    vspec = pl.BlockSpec((seq, HEAD_DIM), lambda i, j: (i, NSA_GROUPS + j))
    full = lambda arr: pl.BlockSpec(arr.shape, lambda i, j: (0,) * arr.ndim)
    ospec = pl.BlockSpec((None, None, nr, HEAD_DIM), lambda i, j: (i, j, 0, 0))
    oshape = jax.ShapeDtypeStruct((b, NSA_GROUPS, nr, HEAD_DIM), BF16)
    return pl.pallas_call(
        _compress_kernel,
        grid=(b, NSA_GROUPS),
        in_specs=[kspec, vspec, full(pek), full(pev), full(w1k), full(w2k), full(w1v), full(w2v),
                  full(g), full(c), full(s)],
        out_specs=[ospec, ospec],
        out_shape=[oshape, oshape],
        compiler_params=_cparams("arbitrary", "arbitrary"),
        name="compress",
    )(kv_c, kv_c, pek, pev, w1k, w2k, w1v, w2v, g, c, s)


def _cmp_kernel(q_ref, k_ref, v_ref, ovt_ref, oc_ref, sel_ref, *, tt, n_slc):
    t0 = pl.program_id(2) * tt
    t = t0 + lax.broadcasted_iota(jnp.int32, (tt, LANES), 0)
    n = lax.broadcasted_iota(jnp.int32, (tt, LANES), 1)
    valid = n * CMP_STRIDE + (CMP_BLOCK - 1) <= t
    k = k_ref[...]
    v = v_ref[...]
    psum = jnp.zeros((tt, LANES), F32)
    for r in range(NSA_REP):
        sl = slice(r * HEAD_DIM, (r + 1) * HEAD_DIM)
        s = jnp.where(valid, _dot_nt(q_ref[:, sl], k), NEG)
        m = jnp.max(s, axis=-1, keepdims=True)
        p = jnp.where(valid, jnp.exp2(s - m), 0.0)
        d = jnp.sum(p, axis=-1, keepdims=True)
        p = p / jnp.where(d > 0, d, 1.0)
        psum = psum + p
        oc_ref[:, sl] = _dot(p.astype(BF16), v)
    imp = lax.dot_general(ovt_ref[...], psum, (((1,), (1,)), ((), ())), preferred_element_type=F32,
                          precision=lax.Precision.HIGHEST)[0:n_slc, :]
    j = lax.broadcasted_iota(jnp.int32, (n_slc, tt), 0)
    cur = (t0 + lax.broadcasted_iota(jnp.int32, (n_slc, tt), 1)) >> SEL_SHIFT
    score = jnp.where(j <= cur, imp, -BIG)
    score = jnp.where(j == 0, BIG, jnp.where(j == cur, BIG, jnp.where(j == cur - 1, BIG, score)))
    cnt = jnp.zeros((n_slc, tt), F32)
    for jp in range(n_slc):
        row = score[jp:jp + 1, :]
        ge = jnp.where(row >= score, 1.0, 0.0)
        gt = jnp.where(row > score, 1.0, 0.0)
        cnt = cnt + jnp.where(j > jp, ge, gt)
    sel = jnp.where(cnt < SEL_TOPN, jnp.where(score > -0.5 * BIG, 1.0, 0.0), 0.0)
    sel = jnp.concatenate([sel, jnp.zeros((LANES - n_slc, tt), F32)], axis=0)
    sel_ref[...] = sel.T.astype(sel_ref.dtype)


def _cmp_attention(q, kc, vc, overlap, seq):
    m, _ = q.shape
    b = m // seq
    tt = 2048
    per = seq // tt
    n_cmp =(seq - CMP_BLOCK) // CMP_STRIDE + 1
    assert n_cmp <= LANES and n_cmp * CMP_STRIDE + CMP_BLOCK - 1 >= seq
    kspec = pl.BlockSpec((None, None, LANES, HEAD_DIM), lambda i, g, j: (i, g, 0, 0))
    return pl.pallas_call(
        functools.partial(_cmp_kernel, tt=tt, n_slc=seq // SEL_BLOCK),
        grid=(b, NSA_GROUPS, per),
        in_specs=[pl.BlockSpec((tt, NSA_REP * HEAD_DIM), lambda i, g, j: (i * per + j, g)),
                  kspec, kspec,
                  pl.BlockSpec((LANES, LANES), lambda i, g, j: (0, 0))],
        out_specs=[pl.BlockSpec((tt, NSA_REP * HEAD_DIM), lambda i, g, j: (i * per + j, g)),
                   pl.BlockSpec((None, None, tt, LANES), lambda i, g, j: (i, g, j, 0))],
        out_shape=[jax.ShapeDtypeStruct((m, NSA_HEADS * HEAD_DIM), F32),
                   jax.ShapeDtypeStruct((b, NSA_GROUPS, seq, LANES), BF16)],
        compiler_params=_cparams("arbitrary", "arbitrary", "arbitrary"),
        name="cmp_attn",
    )(q, kc, vc, overlap)


def _softmax_pv(t, v):
    m = jnp.max(t, axis=-1, keepdims=True)
    p = jnp.exp2(t - m)
    l = jnp.sum(p, axis=-1, keepdims=True)
    return _dot(p.astype(BF16), v) / l


def _tri_bias(n, lower):
    row = lax.broadcasted_iota(jnp.int32, (n, n), 0)
    col = lax.broadcasted_iota(jnp.int32, (n, n), 1)
    return jnp.where(col <= row, 0.0, NEG) if lower else jnp.where(col > row, 0.0, NEG)


def _nsa_kernel(q_ref, ks_ref, vs_ref, kw_ref, vw_ref, sel_ref, ex_ref, oc_ref, gl_ref, o_ref,
                t_sc, p_sc, l_sc, bs_sc, bw_sc, os_sc, *, tq):
    seq = q_ref.shape[0]
    live = pl.program_id(0) >= 0
    n_back = WINDOW // tq
    w_cols = (n_back + 1) * tq
    bw_sc[:, 0:tq] = _tri_bias(tq, False)
    bw_sc[:, tq:n_back * tq] = jnp.zeros((tq, (n_back - 1) * tq), F32)
    bw_sc[:, n_back * tq:w_cols] = _tri_bias(tq, True)

    jobs = []
    for qt in reversed(range(seq // tq)):
        lo, hi = qt * tq, (qt + 1) * tq
        jobs.append(("sel", lo, hi, 0))
        jobs.append(("win", lo, hi, max(lo - WINDOW, 0)))

    assert REGION_JOBS % 2 == 0 and len(jobs) % REGION_JOBS == 0
    n_slots = 2 * REGION_JOBS

    def bias_slot(j):
        return (j // 2) % REGION_JOBS

    def head_rows(r):
        return slice(r * tq, (r + 1) * tq)

    def stage_scores(j):
        kind, lo, hi, klo = jobs[j]
        slot = j % n_slots
        n = hi - klo
        q4 = jnp.concatenate([q_ref[lo:hi, r * HEAD_DIM:(r + 1) * HEAD_DIM] for r in range(NSA_REP)], axis=0)
        k_ref = ks_ref if kind == "sel" else kw_ref
        t_sc[slot, :, 0:n] = _dot_nt(q4, k_ref[klo:hi, :])
        if kind == "sel":
            picked = _dot(sel_ref[lo:hi, :], ex_ref[:, 0:hi])
            bias = jnp.where(picked > 0.5, 0.0, NEG)
            if lo > 0:
                bs_sc[bias_slot(j), :, 0:lo] = bias[:, 0:lo]
            bs_sc[bias_slot(j), :, lo:hi] = bias[:, lo:hi] + _tri_bias(tq, True)

    def stage_softmax(j):
        kind, lo, hi, klo = jobs[j]
        slot = j % n_slots
        n = hi - klo
        for r in range(NSA_REP):
            bias = bs_sc[bias_slot(j), :, 0:n] if kind == "sel" else bw_sc[:, w_cols - n:w_cols]
            t = t_sc[slot, head_rows(r), 0:n] + bias
            p = jnp.exp2(t - jnp.max(t, axis=-1, keepdims=True))
            l_sc[slot, head_rows(r), :] = jnp.sum(p, axis=-1, keepdims=True)
            p_sc[slot, head_rows(r), 0:n] = p.astype(BF16)

    def stage_output(j):
        kind, lo, hi, klo = jobs[j]
        slot = j % n_slots
        n = hi - klo
        v_ref = vs_ref if kind == "sel" else vw_ref
        o4 = _dot(p_sc[slot, :, 0:n], v_ref[klo:hi, :]) / l_sc[slot]
        if kind == "sel":
            os_sc[...] = o4
            return
        gate = jax.nn.sigmoid(gl_ref[lo:hi, :])
        for r in range(NSA_REP):
            sl = slice(r * HEAD_DIM, (r + 1) * HEAD_DIM)
            o = (gate[:, 3 * r:3 * r + 1] * oc_ref[lo:hi, sl]
                 + gate[:, 3 * r + 1:3 * r + 2] * os_sc[head_rows(r), :]
                 + gate[:, 3 * r + 2:3 * r + 3] * o4[head_rows(r), :])
            o_ref[lo:hi, sl] = o.astype(o_ref.dtype)

    for r0 in range(0, len(jobs) + 2 * REGION_JOBS, REGION_JOBS):
        @pl.when(live)
        def _region():
            for j in range(r0, r0 + REGION_JOBS):
                if j < len(jobs):
                    stage_scores(j)
                if 0 <= j - REGION_JOBS < len(jobs):
                    stage_softmax(j - REGION_JOBS)
                if 0 <= j - 2 * REGION_JOBS < len(jobs):
                    stage_output(j - 2 * REGION_JOBS)


def _nsa_attention(q, kk, vv, sel, expand, oc, gl, seq):
    m = q.shape[0]
    b = m // seq
    width = NSA_REP * HEAD_DIM
    qspec = pl.BlockSpec((seq, width), lambda i, g: (i, g))
    ks = pl.BlockSpec((seq, HEAD_DIM), lambda i, g: (i, g))
    kw = pl.BlockSpec((seq, HEAD_DIM), lambda i, g: (i, NSA_GROUPS + g))
    tq = 128
    rows = NSA_REP * tq
    return pl.pallas_call(
        functools.partial(_nsa_kernel, tq=tq),
        grid=(b, NSA_GROUPS),
        scratch_shapes=[pltpu.VMEM((2 * REGION_JOBS, rows, seq), F32), pltpu.VMEM((2 * REGION_JOBS, rows, seq), BF16),
                        pltpu.VMEM((2 * REGION_JOBS, rows, 1), F32), pltpu.VMEM((REGION_JOBS, tq, seq), F32),
                        pltpu.VMEM((tq, WINDOW + tq), F32), pltpu.VMEM((rows, HEAD_DIM), F32)],
        in_specs=[qspec, ks, ks, kw, kw,
                  pl.BlockSpec((None, None, seq, LANES), lambda i, g: (i, g, 0, 0)),
                  pl.BlockSpec((LANES, seq), lambda i, g: (0, 0)),
                  qspec,
                  pl.BlockSpec((seq, LANES), lambda i, g: (i, g))],
        out_specs=qspec,
        out_shape=jax.ShapeDtypeStruct((m, NSA_HEADS * HEAD_DIM), BF16),
        compiler_params=_cparams("arbitrary", "arbitrary"),
        name="nsa_attn",
    )(q, kk, vv, kk, vv, sel, expand, oc, gl)


def _cum_kernel(f_ref, b_ref, o_ref):
    x = f_ref[...] + b_ref[...]
    ls = jnp.minimum(x, 0.0) - jnp.log1p(jnp.exp(-jnp.abs(x)))
    row = lax.broadcasted_iota(jnp.int32, ls.shape, 0)
    d = 1
    while d < ls.shape[0]:
        ls = ls + jnp.where(row >= d, pltpu.roll(ls, d, 0), 0.0)
        d *= 2
    o_ref[...] = ls


def _cum_forget(fl, bias, seq, col_block):
    m = fl.shape[0]
    return pl.pallas_call(
        _cum_kernel,
        grid=(m // seq,),
        in_specs=[pl.BlockSpec((seq, LANES), lambda i: (i, col_block)),
                  pl.BlockSpec((1, LANES), lambda i: (0, 0))],
        out_specs=pl.BlockSpec((seq, LANES), lambda i: (i, 0)),
        out_shape=jax.ShapeDtypeStruct((m, LANES), F32),
        compiler_params=_cparams("arbitrary"),
        name="cum_forget",
    )(fl, bias)


def _fox_kernel(q_ref, k_ref, v_ref, ck_ref, *rest, tq):
    n_cast = (len(rest) - 1) // 2
    o_ref = rest[n_cast]
    _cast_blocks(rest[:n_cast], rest[n_cast + 1:])
    seq = q_ref.shape[0]
    causal_b = _tri_bias(tq, True)
    ck = ck_ref[...] * LOG2E
    for qt in reversed(range(seq // tq)):
        lo, hi = qt * tq, (qt + 1) * tq
        t = _dot_nt(q_ref[lo:hi, :], k_ref[0:hi, :]) - ck[:, 0:hi]
        diag = t[:, lo:hi] + causal_b
        t = diag if qt == 0 else jnp.concatenate([t[:, 0:lo], diag], axis=1)
        o_ref[lo:hi, :] = _softmax_pv(t, v_ref[0:hi, :]).astype(o_ref.dtype)


def _fox_attention(q, k, v, ck, seq, casts=()):
    m = q.shape[0]
    b = m // seq
    blk = pl.BlockSpec((seq, HEAD_DIM), lambda i, h: (i, h))
    cast_specs, cast_shapes = _cast_specs(casts, b * FOX_HEADS, lambda i, h: i * FOX_HEADS + h)
    out = pl.pallas_call(
        functools.partial(_fox_kernel, tq=512),
        grid=(b, FOX_HEADS),
        in_specs=[blk, blk, blk,
                  pl.BlockSpec((None, 1, seq), lambda i, h: (i * FOX_HEADS + h, 0, 0))] + cast_specs,
        out_specs=[blk] + cast_specs,
        out_shape=[jax.ShapeDtypeStruct((m, FOX_HEADS * HEAD_DIM), BF16)] + cast_shapes,
        compiler_params=_cparams("arbitrary", "arbitrary"),
        name="fox_attn",
    )(q, k, v, ck, *casts)
    return out if casts else out[0]


def _merge_kernel(a1_ref, w1_ref, a2_ref, w2_ref, g1_ref, g2_ref, *rest):
    n_cast = (len(rest) - 1) // 2
    o_ref = rest[n_cast]
    u1 = _dot(a1_ref[...], w1_ref[...])
    u2 = _dot(a2_ref[...], w2_ref[...])
    o_ref[...] = (g1_ref[...].astype(F32) * u1 + g2_ref[...].astype(F32) * u2).astype(o_ref.dtype)
    _cast_blocks(rest[:n_cast], rest[n_cast + 1:])


def _merge(a1, w1, a2, w2, gates, casts=()):
    m, k = a1.shape
    n = w1.shape[1]
    tm, tn = 1024, 1024
    nb = n // tn
    mb = m // tm
    aspec = pl.BlockSpec((tm, k), lambda j, i: (i, 0))
    wspec = pl.BlockSpec((k, tn), lambda j, i: (0, j))
    cast_specs, cast_shapes = _cast_specs(casts, nb * mb, lambda j, i: j * mb + i)
    out = pl.pallas_call(
        _merge_kernel,
        grid=(nb, mb),
        in_specs=[aspec, wspec, aspec, wspec,
                  pl.BlockSpec((tm, tn), lambda j, i: (i, j)),
                  pl.BlockSpec((tm, tn), lambda j, i: (i, nb + j))] + cast_specs,
        out_specs=[pl.BlockSpec((tm, tn), lambda j, i: (i, j))] + cast_specs,
        out_shape=[jax.ShapeDtypeStruct((m, n), BF16)] + cast_shapes,
        compiler_params=_cparams("arbitrary", "arbitrary"),
        name="merge",
    )(a1, w1, a2, w2, gates, gates, *casts)
    return out if casts else out[0]


def _rope_tables(pos):
    inv = ROPE_THETA ** (-jnp.arange(ROPE_HALF, dtype=F32) / ROPE_HALF)
    ang = jnp.asarray(pos, dtype=F32)[:, None] * inv[None, :]
    cos, sin = jnp.cos(ang), jnp.sin(ang)
    n = ang.shape[0]
    gap = HEAD_DIM // 2 - ROPE_HALF
    c = jnp.concatenate([cos, jnp.ones((n, gap), F32), cos, jnp.ones((n, gap), F32)], axis=1)
    s = jnp.concatenate([-sin, jnp.zeros((n, gap), F32), sin, jnp.zeros((n, gap), F32)], axis=1)
    return c, s


def _layer(x2, c_pad, bsz, seq, w_ada, b_ada, norm1_g, norm2_g, w_in, b_forget, nsa_q_norm, nsa_k_norm,
           fox_q_norm, fox_k_norm, cmp_pos_k, cmp_pos_v, w_cmp_k1, w_cmp_k2, w_cmp_v1, w_cmp_v2,
           w_up_nsa, w_up_fox, w_out, w_ff1, w_ff2):
    d = x2.shape[1]
    hd = HEAD_DIM
    tm = 1024
    per_m = seq // tm

    mod = _ada(c_pad, w_ada, b_ada.reshape(1, -1))[:bsz]
    shift1, scale1, gate1, shift2, scale2, gate2 = [t.reshape(bsz, 1, d) for t in jnp.split(mod, 6, axis=-1)]

    h = _normmod(x2, norm1_g.reshape(1, d), scale1, shift1, seq)

    o_q = 0
    o_kv = o_q + NSA_HEADS * hd
    o_gate = o_kv + 3 * 2 * NSA_GROUPS * hd
    o_fox = o_gate + 3 * NSA_HEADS
    o_f = o_fox + 3 * FOX_HEADS * hd
    o_merge = o_f + FOX_HEADS
    gw = NSA_GROUPS * hd

    w_t = w_in.T.astype(BF16)
    fw = FOX_HEADS * hd

    pos = np.arange(seq)
    c_t, s_t = _rope_tables(pos)
    tab_specs = [pl.BlockSpec((tm, hd), lambda j, i: (i % per_m, 0))] * 2
    gspec = pl.BlockSpec((1, MXU_COLS), lambda j, i: (0, 0))
    hspec = pl.BlockSpec((1, hd), lambda j, i: (0, 0))
    ones_spec = pl.BlockSpec((MXU_COLS, MXU_COLS), lambda j, i: (0, 0))
    head_ones = jnp.asarray(np.kron(np.eye(MXU_COLS // hd), np.ones((hd, hd))), dtype=BF16)

    def chunk_gain(g):
        return jnp.tile(g.reshape(1, hd), (1, MXU_COLS // hd))

    q_scale = ATTN_SCALE * LOG2E
    def permuted_heads(lo, n_heads):
        return _head_perm(w_t[lo:lo + n_heads * hd].reshape(n_heads, hd, d), 1).reshape(n_heads * hd, d)
    w_qk = jnp.concatenate([permuted_heads(o_q, NSA_HEADS), permuted_heads(o_kv + 2 * gw, NSA_GROUPS),
                            permuted_heads(o_kv + 4 * gw, NSA_GROUPS)], axis=0)
    g_q = chunk_gain(q_scale * _head_perm(nsa_q_norm, 0))
    g_k = chunk_gain(_head_perm(nsa_k_norm, 0))
    rope_specs = [gspec, ones_spec] + tab_specs
    q_n = _matmul(h, w_qk, _ep_headnorm_rope, BF16, rows=(0, 1024, 2),
                  extras=(g_q, head_ones, c_t, s_t), extra_specs=rope_specs, name="proj_nsa_q")
    k_sw = _matmul(h, w_qk, _ep_headnorm_rope, BF16, rows=(NSA_HEADS * hd, 0, 1),
                   extras=(g_k, head_ones, c_t, s_t), extra_specs=rope_specs, name="proj_nsa_k")
    v_sw = _matmul(h, w_t, _ep_raw, BF16, tn=gw, rows=(o_kv + 3 * gw, 2 * gw, 2), name="proj_nsa_v")
    kv_c = _matmul(h, w_t, _ep_raw, F32, rows=(o_kv, 2 * gw, 1), name="proj_nsa_cmp")
    fq = _matmul(h, w_t, _ep_headnorm, BF16, rows=(o_fox, 1024, 2),
                 extras=(q_scale * fox_q_norm.reshape(1, hd),), extra_specs=[hspec], name="proj_fox_q")
    fk = _matmul(h, w_t, _ep_headnorm, BF16, rows=(o_fox + fw, 1024, 2),
                 extras=(fox_k_norm.reshape(1, hd),), extra_specs=[hspec], name="proj_fox_k")
    fv = _matmul(h, w_t, _ep_raw, BF16, rows=(o_fox + 2 * fw, 1024, 2), name="proj_fox_v")
    g_merge = _matmul(h, w_t, _ep_sigmoid, BF16, rows=(o_merge, 1024, 2 * d // 1024), name="proj_merge")

    per_group = NSA_REP * 3
    n_gate = NSA_HEADS * 3
    w_small = jnp.concatenate([w_t[o_gate:o_fox], w_t[o_f:o_merge]], axis=0)
    w_small = jnp.pad(w_small, ((0, LANES - w_small.shape[0]), (0, 0)))
    logits = _matmul(h, w_small, _ep_raw, F32, tn=LANES, rows=(0, 0, 1), name="proj_small")
    lane_blocks = [logits[:, g * per_group:(g + 1) * per_group] for g in range(NSA_GROUPS)]
    lane_blocks.append(logits[:, n_gate:n_gate + FOX_HEADS])
    small = jnp.concatenate([jnp.pad(blk, ((0, 0), (0, LANES - blk.shape[1]))) for blk in lane_blocks], axis=1)

    n_rows = seq // CMP_STRIDE
    end_pos = np.arange(n_rows) * CMP_STRIDE + CMP_BLOCK - 1
    ce, se = _rope_tables(end_pos)
    k_cmp, v_cmp = _compress(
        kv_c, seq, cmp_pos_k.reshape(2, CMP_STRIDE * hd), cmp_pos_v.reshape(2, CMP_STRIDE * hd),
        w_cmp_k1.astype(BF16), _head_perm(w_cmp_k2, 1).astype(BF16), w_cmp_v1.astype(BF16), w_cmp_v2.astype(BF16),
        _head_perm(nsa_k_norm, 0).reshape(1, hd), ce, se)

    n_slc = seq // SEL_BLOCK
    ci = np.arange(LANES)[:, None] * CMP_STRIDE
    sj = np.arange(LANES)[None, :] * SEL_BLOCK
    overlap = ((ci < sj + SEL_BLOCK) & (ci + CMP_BLOCK > sj) & (np.arange(LANES)[None, :] < n_slc)
               & (np.arange(LANES)[:, None] < n_rows - 1)).astype(np.float32)
    o_c, sel = _cmp_attention(q_n, k_cmp, v_cmp, jnp.asarray(overlap.T), seq)

    expand = (np.arange(LANES)[:, None] == (np.arange(seq)[None, :] >> SEL_SHIFT)).astype(np.float32)
    o_nsa = _nsa_attention(q_n, k_sw, v_sw, sel, jnp.asarray(expand, dtype=BF16), o_c, small, seq)

    f_bias = jnp.pad(b_forget, (0, LANES - FOX_HEADS)).reshape(1, LANES)
    cum = _cum_forget(small, f_bias, seq, NSA_GROUPS)
    ck = cum[:, :FOX_HEADS].reshape(bsz, seq, FOX_HEADS).transpose(0, 2, 1).reshape(bsz * FOX_HEADS, 1, seq)
    o_fox, up_nsa_b, up_fox_b, w_out_b, w_ff1_b = _fox_attention(
        fq, fk, fv, ck, seq, casts=(w_up_nsa, w_up_fox, w_out, w_ff1))

    y = _merge(o_nsa, up_nsa_b, o_fox, up_fox_b, g_merge)
    tn_out = 1024
    res_specs = [pl.BlockSpec((tm, tn_out), lambda j, i: (i, j)),
                 pl.BlockSpec((None, 1, tn_out), lambda j, i: (i // per_m, 0, j))]
    x_mid = _matmul(y, w_out_b, _ep_residual, F32, extras=(x2, gate1), extra_specs=res_specs,
                    tn=tn_out, name="out_proj")

    h2 = _normmod(x_mid, norm2_g.reshape(1, d), scale2, shift2, seq)
    hid, w_ff2_b = _matmul(h2, w_ff1_b, _ep_relu2, BF16, casts=(w_ff2,), name="ff1")
    return _matmul_k_residual(hid, w_ff2_b, x_mid, gate2, seq, name="ff2")


def kernel(x, c, w_ada, b_ada, norm1_g, norm2_g, w_in, b_forget, nsa_q_norm, nsa_k_norm, fox_q_norm, fox_k_norm, cmp_pos_k, cmp_pos_v, w_cmp_k1, w_cmp_k2, w_cmp_v1, w_cmp_v2, w_up_nsa, w_up_fox, w_out, w_ff1, w_ff2):
    bsz, seq, d = x.shape
    x2 = x.reshape(bsz * seq, d)
    c_pad = jnp.pad(c, ((0, 8 - bsz), (0, 0)))
    params = (w_ada, b_ada, norm1_g, norm2_g, w_in, b_forget, nsa_q_norm, nsa_k_norm, fox_q_norm, fox_k_norm,
              cmp_pos_k, cmp_pos_v, w_cmp_k1, w_cmp_k2, w_cmp_v1, w_cmp_v2, w_up_nsa, w_up_fox, w_out,
              w_ff1, w_ff2)
    for layer in range(w_ada.shape[0]):
        x2 = _layer(x2, c_pad, bsz, seq, *[p[layer] for p in params])
    return x2.reshape(bsz, seq, d)
```

```python
import functools

import numpy as np
import jax
import jax.numpy as jnp
from jax import lax
from jax.experimental import pallas as pl
from jax.experimental.pallas import tpu as pltpu

F32 = jnp.float32
BF16 = jnp.bfloat16

HEAD_DIM = 128
NSA_HEADS = 16
NSA_GROUPS = 4
NSA_REP = NSA_HEADS // NSA_GROUPS
FOX_HEADS = 16
CMP_BLOCK = 32
CMP_STRIDE = 16
SEL_BLOCK = 64
SEL_SHIFT = SEL_BLOCK.bit_length() - 1
SEL_TOPN = 16
WINDOW = 512
ROPE_THETA = 500000.0
ROT_DIM = HEAD_DIM // 4
RMS_EPS = 1e-6
ATTN_SCALE = HEAD_DIM ** -0.5

LOG2E = float(np.log2(np.e))
ROPE_HALF = ROT_DIM // 2
NEG = -1e30
BIG = 1e30
LANES = 128
MXU_COLS = 256
VMEM_LIMIT = 60 * 1024 * 1024
REGION_JOBS = 2


def _head_perm(x, axis):
    mid = HEAD_DIM // 2
    cuts = [(0, ROPE_HALF), (mid, mid + ROPE_HALF), (ROT_DIM, mid), (ROPE_HALF, ROT_DIM), (mid + ROPE_HALF, HEAD_DIM)]
    return jnp.concatenate([lax.slice_in_dim(x, a, b, axis=axis) for a, b in cuts], axis=axis)


def _cparams(*sem):
    return pltpu.CompilerParams(dimension_semantics=sem, vmem_limit_bytes=VMEM_LIMIT)


def _dot(a, b):
    return jnp.dot(a, b, preferred_element_type=F32)


def _dot_nt(a, b):
    return lax.dot_general(a, b, (((1,), (1,)), ((), ())), preferred_element_type=F32)


def _ada_kernel(c_ref, w_ref, b_ref, o_ref):
    c = c_ref[...]
    s = (c * jax.nn.sigmoid(c)).astype(BF16)
    o_ref[...] = _dot(s, w_ref[...].astype(BF16)) + b_ref[...]


def _ada(c_pad, w, b):
    rows, d = c_pad.shape
    n = w.shape[1]
    tn = 1024
    return pl.pallas_call(
        _ada_kernel,
        grid=(n // tn,),
        in_specs=[pl.BlockSpec((rows, d), lambda j: (0, 0)),
                  pl.BlockSpec((d, tn), lambda j: (0, j)),
                  pl.BlockSpec((1, tn), lambda j: (0, j))],
        out_specs=pl.BlockSpec((rows, tn), lambda j: (0, j)),
        out_shape=jax.ShapeDtypeStruct((rows, n), F32),
        compiler_params=_cparams("arbitrary"),
        name="ada",
    )(c_pad, w, b)


def _normmod_kernel(x_ref, g_ref, sc_ref, sh_ref, o_ref):
    x = x_ref[...]
    ms = jnp.mean(x * x, axis=-1, keepdims=True)
    y = x * lax.rsqrt(ms + RMS_EPS) * g_ref[...]
    o_ref[...] = (y * (1.0 + sc_ref[...]) + sh_ref[...]).astype(o_ref.dtype)


def _normmod(x2, g, scale, shift, seq):
    m, d = x2.shape
    tr = 512
    per = seq // tr
    return pl.pallas_call(
        _normmod_kernel,
        grid=(m // tr,),
        in_specs=[pl.BlockSpec((tr, d), lambda i: (i, 0)),
                  pl.BlockSpec((1, d), lambda i: (0, 0)),
                  pl.BlockSpec((None, 1, d), lambda i: (i // per, 0, 0)),
                  pl.BlockSpec((None, 1, d), lambda i: (i // per, 0, 0))],
        out_specs=pl.BlockSpec((tr, d), lambda i: (i, 0)),
        out_shape=jax.ShapeDtypeStruct((m, d), BF16),
        compiler_params=_cparams("arbitrary"),
        name="normmod",
    )(x2, g, scale, shift)


def _ep_raw(acc, o_ref):
    o_ref[...] = acc.astype(o_ref.dtype)


def _ep_sigmoid(acc, o_ref):
    o_ref[...] = jax.nn.sigmoid(acc).astype(o_ref.dtype)


def _ep_relu2(acc, o_ref):
    r = jnp.maximum(acc, 0.0)
    o_ref[...] = (r * r).astype(o_ref.dtype)


def _ep_residual(acc, o_ref, x_ref, g_ref):
    o_ref[...] = x_ref[...] + g_ref[...] * acc


def _head_norm(a, g):
    ms = jnp.mean(a * a, axis=-1, keepdims=True)
    return a * lax.rsqrt(ms + RMS_EPS) * g


def _rope(a, c, s):
    return a * c + pltpu.roll(a, HEAD_DIM // 2, 1) * s


def _chunk_head_norm(acc, g_ref, ones_ref):
    ss = _dot((acc * acc).astype(BF16), ones_ref[...])
    return acc * lax.rsqrt(ss * (1.0 / HEAD_DIM) + RMS_EPS) * g_ref[...]


def _ep_headnorm(acc, o_ref, g_ref):
    g = g_ref[...]
    for h0 in range(0, acc.shape[1], HEAD_DIM):
        o_ref[:, h0:h0 + HEAD_DIM] = _head_norm(acc[:, h0:h0 + HEAD_DIM], g).astype(o_ref.dtype)


def _ep_headnorm_rope(acc, o_ref, g_ref, ones_ref, c_ref, s_ref):
    c, s = c_ref[...], s_ref[...]
    for c0 in range(0, acc.shape[1], MXU_COLS):
        y = _chunk_head_norm(acc[:, c0:c0 + MXU_COLS], g_ref, ones_ref)
        for h0 in range(0, MXU_COLS, HEAD_DIM):
            dst = slice(c0 + h0, c0 + h0 + HEAD_DIM)
            o_ref[:, dst] = _rope(y[:, h0:h0 + HEAD_DIM], c, s).astype(o_ref.dtype)


def _cast_blocks(refs_in, refs_out):
    for src, dst in zip(refs_in, refs_out):
        dst[...] = src[...].astype(dst.dtype)


def _cast_specs(casts, steps, index_of_step):
    specs, shapes = [], []
    for w in casts:
        rows, cols = w.shape
        assert rows % (steps * 16) == 0
        specs.append(pl.BlockSpec((rows // steps, cols), lambda *g: (index_of_step(*g), 0)))
        shapes.append(jax.ShapeDtypeStruct((rows, cols), BF16))
    return specs, shapes


def _mm_kernel(a_ref, b_ref, *rest, epilogue, n_extra, n_cast, transposed):
    extras = rest[:n_extra]
    cast_in = rest[n_extra:n_extra + n_cast]
    o_ref = rest[n_extra + n_cast]
    cast_out = rest[n_extra + n_cast + 1:]
    acc = _dot_nt(a_ref[...], b_ref[...]) if transposed else _dot(a_ref[...], b_ref[...])
    epilogue(acc, o_ref, *extras)
    _cast_blocks(cast_in, cast_out)


def _matmul(a, b, epilogue, out_dtype, extras=(), extra_specs=(), tm=1024, tn=1024, name="mm",
            casts=(), rows=None):
    m, k = a.shape
    if rows is None:
        n = b.shape[1]
        tn = min(tn, n)
        b_spec = pl.BlockSpec((k, tn), lambda j, i: (0, j))
    else:
        first, stride, count = rows
        assert first % 16 == 0 and stride % 16 == 0
        n = count * tn
        b_spec = pl.BlockSpec((pl.Element(tn), pl.Element(k)),
                              lambda j, i: (pl.multiple_of(first + j * stride, 16), 0))
    grid = (n // tn, m // tm)
    cast_specs, cast_shapes = _cast_specs(casts, grid[0] * grid[1], lambda j, i: j * grid[1] + i)
    out = pl.pallas_call(
        functools.partial(_mm_kernel, epilogue=epilogue, n_extra=len(extras), n_cast=len(casts),
                          transposed=rows is not None),
        grid=grid,
        in_specs=[pl.BlockSpec((tm, k), lambda j, i: (i, 0)), b_spec] + list(extra_specs) + cast_specs,
        out_specs=[pl.BlockSpec((tm, tn), lambda j, i: (i, j))] + cast_specs,
        out_shape=[jax.ShapeDtypeStruct((m, n), out_dtype)] + cast_shapes,
        compiler_params=_cparams("arbitrary", "arbitrary"),
        name=name,
    )(a, b, *extras, *casts)
    return out if casts else out[0]


def _mmk_kernel(a_ref, b_ref, x_ref, g_ref, o_ref):
    kk = pl.program_id(2)

    @pl.when(kk == 0)
    def _():
        o_ref[...] = x_ref[...] + g_ref[...] * _dot(a_ref[...], b_ref[...])

    @pl.when(kk > 0)
    def _():
        o_ref[...] += g_ref[...] * _dot(a_ref[...], b_ref[...])


def _matmul_k_residual(a, b, x, gate, seq, tm=1024, tn=1024, tk=4096, name="mmk"):
    m, k = a.shape
    n = b.shape[1]
    per = seq // tm
    nk = k // tk
    return pl.pallas_call(
        _mmk_kernel,
        grid=(m // tm, n // tn, nk),
        in_specs=[pl.BlockSpec((tm, tk), lambda i, j, kk: (i, kk)),
                  pl.BlockSpec((tk, tn), lambda i, j, kk: (kk, j)),
                  pl.BlockSpec((tm, tn), lambda i, j, kk: (i, j)),
                  pl.BlockSpec((None, 1, tn), lambda i, j, kk: (i // per, 0, j))],
        out_specs=pl.BlockSpec((tm, tn), lambda i, j, kk: (i, j)),
        out_shape=jax.ShapeDtypeStruct((m, n), F32),
        compiler_params=_cparams("arbitrary", "arbitrary", "arbitrary"),
        name=name,
    )(a, b, x, gate)


def _compress_kernel(xk_ref, xv_ref, pek_ref, pev_ref, w1k_ref, w2k_ref, w1v_ref, w2v_ref,
                     g_ref, c_ref, s_ref, ko_ref, vo_ref):
    half = CMP_STRIDE * HEAD_DIM
    n_rows = xk_ref.shape[0] // CMP_STRIDE

    def comp(x_ref, pe_ref, w1_ref, w2_ref):
        x = jnp.concatenate([x_ref[pl.ds(l, n_rows, stride=CMP_STRIDE), :] for l in range(CMP_STRIDE)], axis=1)
        lo = (x + pe_ref[0:1, :]).astype(BF16)
        hi = (x + pe_ref[1:2, :]).astype(BF16)
        p = _dot(lo, w1_ref[0:half, :])
        q = _dot(hi, w1_ref[half:2 * half, :])
        h = p + pltpu.roll(q, q.shape[0] - 1, 0)
        h = jax.nn.gelu(h, approximate=True)
        return _dot(h.astype(BF16), w2_ref[...])

    kc = comp(xk_ref, pek_ref, w1k_ref, w2k_ref)
    kc = _rope(_head_norm(kc, g_ref[...]), c_ref[...], s_ref[...])
    ko_ref[...] = kc.astype(ko_ref.dtype)
    vo_ref[...] = comp(xv_ref, pev_ref, w1v_ref, w2v_ref).astype(vo_ref.dtype)


def _compress(kv_c, seq, pek, pev, w1k, w2k, w1v, w2v, g, c, s):
    b = kv_c.shape[0] // seq
    nr = seq // CMP_STRIDE
    kspec = pl.BlockSpec((seq, HEAD_DIM), lambda i, j: (i, j))
    vspec = pl.BlockSpec((seq, HEAD_DIM), lambda i, j: (i, NSA_GROUPS + j))
    full = lambda arr: pl.BlockSpec(arr.shape, lambda i, j: (0,) * arr.ndim)
    ospec = pl.BlockSpec((None, None, nr, HEAD_DIM), lambda i, j: (i, j, 0, 0))
    oshape = jax.ShapeDtypeStruct((b, NSA_GROUPS, nr, HEAD_DIM), BF16)
    return pl.pallas_call(
        _compress_kernel,
        grid=(b, NSA_GROUPS),
        in_specs=[kspec, vspec, full(pek), full(pev), full(w1k), full(w2k), full(w1v), full(w2v),
                  full(g), full(c), full(s)],
        out_specs=[ospec, ospec],
        out_shape=[oshape, oshape],
        compiler_params=_cparams("arbitrary", "arbitrary"),
        name="compress",
    )(kv_c, kv_c, pek, pev, w1k, w2k, w1v, w2v, g, c, s)


def _cmp_kernel(q_ref, k_ref, v_ref, ovt_ref, oc_ref, sel_ref, *, tt, n_slc):
    t0 = pl.program_id(2) * tt
    t = t0 + lax.broadcasted_iota(jnp.int32, (tt, LANES), 0)
    n = lax.broadcasted_iota(jnp.int32, (tt, LANES), 1)
    valid = n * CMP_STRIDE + (CMP_BLOCK - 1) <= t
    k = k_ref[...]
    v = v_ref[...]
    psum = jnp.zeros((tt, LANES), F32)
    for r in range(NSA_REP):
        sl = slice(r * HEAD_DIM, (r + 1) * HEAD_DIM)
        s = jnp.where(valid, _dot_nt(q_ref[:, sl], k), NEG)
        m = jnp.max(s, axis=-1, keepdims=True)
        p = jnp.where(valid, jnp.exp2(s - m), 0.0)
        d = jnp.sum(p, axis=-1, keepdims=True)
        p = p / jnp.where(d > 0, d, 1.0)
        psum = psum + p
        oc_ref[:, sl] = _dot(p.astype(BF16), v)
    imp = lax.dot_general(ovt_ref[...], psum, (((1,), (1,)), ((), ())), preferred_element_type=F32,
                          precision=lax.Precision.HIGHEST)[0:n_slc, :]
    j = lax.broadcasted_iota(jnp.int32, (n_slc, tt), 0)
    cur = (t0 + lax.broadcasted_iota(jnp.int32, (n_slc, tt), 1)) >> SEL_SHIFT
    score = jnp.where(j <= cur, imp, -BIG)
    score = jnp.where(j == 0, BIG, jnp.where(j == cur, BIG, jnp.where(j == cur - 1, BIG, score)))
    cnt = jnp.zeros((n_slc, tt), F32)
    for jp in range(n_slc):
        row = score[jp:jp + 1, :]
        ge = jnp.where(row >= score, 1.0, 0.0)
        gt = jnp.where(row > score, 1.0, 0.0)
        cnt = cnt + jnp.where(j > jp, ge, gt)
    sel = jnp.where(cnt < SEL_TOPN, jnp.where(score > -0.5 * BIG, 1.0, 0.0), 0.0)
    sel = jnp.concatenate([sel, jnp.zeros((LANES - n_slc, tt), F32)], axis=0)
    sel_ref[...] = sel.T.astype(sel_ref.dtype)


def _cmp_attention(q, kc, vc, overlap, seq):
    m, _ = q.shape
    b = m // seq
    tt = 2048
    per = seq // tt
    n_cmp =(seq - CMP_BLOCK) // CMP_STRIDE + 1
    assert n_cmp <= LANES and n_cmp * CMP_STRIDE + CMP_BLOCK - 1 >= seq
    kspec = pl.BlockSpec((None, None, LANES, HEAD_DIM), lambda i, g, j: (i, g, 0, 0))
    return pl.pallas_call(
        functools.partial(_cmp_kernel, tt=tt, n_slc=seq // SEL_BLOCK),
        grid=(b, NSA_GROUPS, per),
        in_specs=[pl.BlockSpec((tt, NSA_REP * HEAD_DIM), lambda i, g, j: (i * per + j, g)),
                  kspec, kspec,
                  pl.BlockSpec((LANES, LANES), lambda i, g, j: (0, 0))],
        out_specs=[pl.BlockSpec((tt, NSA_REP * HEAD_DIM), lambda i, g, j: (i * per + j, g)),
                   pl.BlockSpec((None, None, tt, LANES), lambda i, g, j: (i, g, j, 0))],
        out_shape=[jax.ShapeDtypeStruct((m, NSA_HEADS * HEAD_DIM), F32),
                   jax.ShapeDtypeStruct((b, NSA_GROUPS, seq, LANES), BF16)],
        compiler_params=_cparams("arbitrary", "arbitrary", "arbitrary"),
        name="cmp_attn",
    )(q, kc, vc, overlap)


def _softmax_pv(t, v):
    m = jnp.max(t, axis=-1, keepdims=True)
    p = jnp.exp2(t - m)
    l = jnp.sum(p, axis=-1, keepdims=True)
    return _dot(p.astype(BF16), v) / l


def _tri_bias(n, lower):
    row = lax.broadcasted_iota(jnp.int32, (n, n), 0)
    col = lax.broadcasted_iota(jnp.int32, (n, n), 1)
    return jnp.where(col <= row, 0.0, NEG) if lower else jnp.where(col > row, 0.0, NEG)


def _nsa_kernel(q_ref, ks_ref, vs_ref, kw_ref, vw_ref, sel_ref, ex_ref, oc_ref, gl_ref, o_ref,
                t_sc, p_sc, l_sc, bs_sc, bw_sc, os_sc, *, tq):
    seq = q_ref.shape[0]
    live = pl.program_id(0) >= 0
    n_back = WINDOW // tq
    w_cols = (n_back + 1) * tq
    bw_sc[:, 0:tq] = _tri_bias(tq, False)
    bw_sc[:, tq:n_back * tq] = jnp.zeros((tq, (n_back - 1) * tq), F32)
    bw_sc[:, n_back * tq:w_cols] = _tri_bias(tq, True)

    jobs = []
    for qt in reversed(range(seq // tq)):
        lo, hi = qt * tq, (qt + 1) * tq
        jobs.append(("sel", lo, hi, 0))
        jobs.append(("win", lo, hi, max(lo - WINDOW, 0)))

    assert REGION_JOBS % 2 == 0 and len(jobs) % REGION_JOBS == 0
    n_slots = 2 * REGION_JOBS

    def bias_slot(j):
        return (j // 2) % REGION_JOBS

    def head_rows(r):
        return slice(r * tq, (r + 1) * tq)

    def stage_scores(j):
        kind, lo, hi, klo = jobs[j]
        slot = j % n_slots
        n = hi - klo
        q4 = jnp.concatenate([q_ref[lo:hi, r * HEAD_DIM:(r + 1) * HEAD_DIM] for r in range(NSA_REP)], axis=0)
        k_ref = ks_ref if kind == "sel" else kw_ref
        t_sc[slot, :, 0:n] = _dot_nt(q4, k_ref[klo:hi, :])
        if kind == "sel":
            picked = _dot(sel_ref[lo:hi, :], ex_ref[:, 0:hi])
            bias = jnp.where(picked > 0.5, 0.0, NEG)
            if lo > 0:
                bs_sc[bias_slot(j), :, 0:lo] = bias[:, 0:lo]
            bs_sc[bias_slot(j), :, lo:hi] = bias[:, lo:hi] + _tri_bias(tq, True)

    def stage_softmax(j):
        kind, lo, hi, klo = jobs[j]
        slot = j % n_slots
        n = hi - klo
        for r in range(NSA_REP):
            bias = bs_sc[bias_slot(j), :, 0:n] if kind == "sel" else bw_sc[:, w_cols - n:w_cols]
            t = t_sc[slot, head_rows(r), 0:n] + bias
            p = jnp.exp2(t - jnp.max(t, axis=-1, keepdims=True))
            l_sc[slot, head_rows(r), :] = jnp.sum(p, axis=-1, keepdims=True)
            p_sc[slot, head_rows(r), 0:n] = p.astype(BF16)

    def stage_output(j):
        kind, lo, hi, klo = jobs[j]
        slot = j % n_slots
        n = hi - klo
        v_ref = vs_ref if kind == "sel" else vw_ref
        o4 = _dot(p_sc[slot, :, 0:n], v_ref[klo:hi, :]) / l_sc[slot]
        if kind == "sel":
            os_sc[...] = o4
            return
        gate = jax.nn.sigmoid(gl_ref[lo:hi, :])
        for r in range(NSA_REP):
            sl = slice(r * HEAD_DIM, (r + 1) * HEAD_DIM)
            o = (gate[:, 3 * r:3 * r + 1] * oc_ref[lo:hi, sl]
                 + gate[:, 3 * r + 1:3 * r + 2] * os_sc[head_rows(r), :]
                 + gate[:, 3 * r + 2:3 * r + 3] * o4[head_rows(r), :])
            o_ref[lo:hi, sl] = o.astype(o_ref.dtype)

    for r0 in range(0, len(jobs) + 2 * REGION_JOBS, REGION_JOBS):
        @pl.when(live)
        def _region():
            for j in range(r0, r0 + REGION_JOBS):
                if j < len(jobs):
                    stage_scores(j)
                if 0 <= j - REGION_JOBS < len(jobs):
                    stage_softmax(j - REGION_JOBS)
                if 0 <= j - 2 * REGION_JOBS < len(jobs):
                    stage_output(j - 2 * REGION_JOBS)


def _nsa_attention(q, kk, vv, sel, expand, oc, gl, seq):
    m = q.shape[0]
    b = m // seq
    width = NSA_REP * HEAD_DIM
    qspec = pl.BlockSpec((seq, width), lambda i, g: (i, g))
    ks = pl.BlockSpec((seq, HEAD_DIM), lambda i, g: (i, g))
    kw = pl.BlockSpec((seq, HEAD_DIM), lambda i, g: (i, NSA_GROUPS + g))
    tq = 128
    rows = NSA_REP * tq
    return pl.pallas_call(
        functools.partial(_nsa_kernel, tq=tq),
        grid=(b, NSA_GROUPS),
        scratch_shapes=[pltpu.VMEM((2 * REGION_JOBS, rows, seq), F32), pltpu.VMEM((2 * REGION_JOBS, rows, seq), BF16),
                        pltpu.VMEM((2 * REGION_JOBS, rows, 1), F32), pltpu.VMEM((REGION_JOBS, tq, seq), F32),
                        pltpu.VMEM((tq, WINDOW + tq), F32), pltpu.VMEM((rows, HEAD_DIM), F32)],
        in_specs=[qspec, ks, ks, kw, kw,
                  pl.BlockSpec((None, None, seq, LANES), lambda i, g: (i, g, 0, 0)),
                  pl.BlockSpec((LANES, seq), lambda i, g: (0, 0)),
                  qspec,
                  pl.BlockSpec((seq, LANES), lambda i, g: (i, g))],
        out_specs=qspec,
        out_shape=jax.ShapeDtypeStruct((m, NSA_HEADS * HEAD_DIM), BF16),
        compiler_params=_cparams("arbitrary", "arbitrary"),
        name="nsa_attn",
    )(q, kk, vv, kk, vv, sel, expand, oc, gl)


def _cum_kernel(f_ref, b_ref, o_ref):
    x = f_ref[...] + b_ref[...]
    ls = jnp.minimum(x, 0.0) - jnp.log1p(jnp.exp(-jnp.abs(x)))
    row = lax.broadcasted_iota(jnp.int32, ls.shape, 0)
    d = 1
    while d < ls.shape[0]:
        ls = ls + jnp.where(row >= d, pltpu.roll(ls, d, 0), 0.0)
        d *= 2
    o_ref[...] = ls


def _cum_forget(fl, bias, seq, col_block):
    m = fl.shape[0]
    return pl.pallas_call(
        _cum_kernel,
        grid=(m // seq,),
        in_specs=[pl.BlockSpec((seq, LANES), lambda i: (i, col_block)),
                  pl.BlockSpec((1, LANES), lambda i: (0, 0))],
        out_specs=pl.BlockSpec((seq, LANES), lambda i: (i, 0)),
        out_shape=jax.ShapeDtypeStruct((m, LANES), F32),
        compiler_params=_cparams("arbitrary"),
        name="cum_forget",
    )(fl, bias)


def _fox_kernel(q_ref, k_ref, v_ref, ck_ref, *rest, tq):
    n_cast = (len(rest) - 1) // 2
    o_ref = rest[n_cast]
    _cast_blocks(rest[:n_cast], rest[n_cast + 1:])
    seq = q_ref.shape[0]
    causal_b = _tri_bias(tq, True)
    ck = ck_ref[...] * LOG2E
    for qt in reversed(range(seq // tq)):
        lo, hi = qt * tq, (qt + 1) * tq
        t = _dot_nt(q_ref[lo:hi, :], k_ref[0:hi, :]) - ck[:, 0:hi]
        diag = t[:, lo:hi] + causal_b
        t = diag if qt == 0 else jnp.concatenate([t[:, 0:lo], diag], axis=1)
        o_ref[lo:hi, :] = _softmax_pv(t, v_ref[0:hi, :]).astype(o_ref.dtype)


def _fox_attention(q, k, v, ck, seq, casts=()):
    m = q.shape[0]
    b = m // seq
    blk = pl.BlockSpec((seq, HEAD_DIM), lambda i, h: (i, h))
    cast_specs, cast_shapes = _cast_specs(casts, b * FOX_HEADS, lambda i, h: i * FOX_HEADS + h)
    out = pl.pallas_call(
        functools.partial(_fox_kernel, tq=512),
        grid=(b, FOX_HEADS),
        in_specs=[blk, blk, blk,
                  pl.BlockSpec((None, 1, seq), lambda i, h: (i * FOX_HEADS + h, 0, 0))] + cast_specs,
        out_specs=[blk] + cast_specs,
        out_shape=[jax.ShapeDtypeStruct((m, FOX_HEADS * HEAD_DIM), BF16)] + cast_shapes,
        compiler_params=_cparams("arbitrary", "arbitrary"),
        name="fox_attn",
    )(q, k, v, ck, *casts)
    return out if casts else out[0]


def _merge_kernel(a1_ref, w1_ref, a2_ref, w2_ref, g1_ref, g2_ref, o_ref):
    u1 = _dot(a1_ref[...], w1_ref[...])
    u2 = _dot(a2_ref[...], w2_ref[...])
    o_ref[...] = (g1_ref[...].astype(F32) * u1 + g2_ref[...].astype(F32) * u2).astype(o_ref.dtype)


def _merge(a1, w1, a2, w2, gates):
    m, k = a1.shape
    n = w1.shape[1]
    tm, tn = 1024, 1024
    nb = n // tn
    aspec = pl.BlockSpec((tm, k), lambda j, i: (i, 0))
    wspec = pl.BlockSpec((k, tn), lambda j, i: (0, j))
    return pl.pallas_call(
        _merge_kernel,
        grid=(nb, m // tm),
        in_specs=[aspec, wspec, aspec, wspec,
                  pl.BlockSpec((tm, tn), lambda j, i: (i, j)),
                  pl.BlockSpec((tm, tn), lambda j, i: (i, nb + j))],
        out_specs=pl.BlockSpec((tm, tn), lambda j, i: (i, j)),
        out_shape=jax.ShapeDtypeStruct((m, n), BF16),
        compiler_params=_cparams("arbitrary", "arbitrary"),
        name="merge",
    )(a1, w1, a2, w2, gates, gates)


def _rope_tables(pos):
    inv = ROPE_THETA ** (-jnp.arange(ROPE_HALF, dtype=F32) / ROPE_HALF)
    ang = jnp.asarray(pos, dtype=F32)[:, None] * inv[None, :]
    cos, sin = jnp.cos(ang), jnp.sin(ang)
    n = ang.shape[0]
    gap = HEAD_DIM // 2 - ROPE_HALF
    c = jnp.concatenate([cos, jnp.ones((n, gap), F32), cos, jnp.ones((n, gap), F32)], axis=1)
    s = jnp.concatenate([-sin, jnp.zeros((n, gap), F32), sin, jnp.zeros((n, gap), F32)], axis=1)
    return c, s


def _layer(x2, c_pad, bsz, seq, w_ada, b_ada, norm1_g, norm2_g, w_in, b_forget, nsa_q_norm, nsa_k_norm,
           fox_q_norm, fox_k_norm, cmp_pos_k, cmp_pos_v, w_cmp_k1, w_cmp_k2, w_cmp_v1, w_cmp_v2,
           w_up_nsa, w_up_fox, w_out, w_ff1, w_ff2):
    d = x2.shape[1]
    hd = HEAD_DIM
    tm = 1024
    per_m = seq // tm

    mod = _ada(c_pad, w_ada, b_ada.reshape(1, -1))[:bsz]
    shift1, scale1, gate1, shift2, scale2, gate2 = [t.reshape(bsz, 1, d) for t in jnp.split(mod, 6, axis=-1)]

    h = _normmod(x2, norm1_g.reshape(1, d), scale1, shift1, seq)

    o_q = 0
    o_kv = o_q + NSA_HEADS * hd
    o_gate = o_kv + 3 * 2 * NSA_GROUPS * hd
    o_fox = o_gate + 3 * NSA_HEADS
    o_f = o_fox + 3 * FOX_HEADS * hd
    o_merge = o_f + FOX_HEADS
    gw = NSA_GROUPS * hd

    w_t = w_in.T.astype(BF16)
    fw = FOX_HEADS * hd

    pos = np.arange(seq)
    c_t, s_t = _rope_tables(pos)
    tab_specs = [pl.BlockSpec((tm, hd), lambda j, i: (i % per_m, 0))] * 2
    gspec = pl.BlockSpec((1, MXU_COLS), lambda j, i: (0, 0))
    hspec = pl.BlockSpec((1, hd), lambda j, i: (0, 0))
    ones_spec = pl.BlockSpec((MXU_COLS, MXU_COLS), lambda j, i: (0, 0))
    head_ones = jnp.asarray(np.kron(np.eye(MXU_COLS // hd), np.ones((hd, hd))), dtype=BF16)

    def chunk_gain(g):
        return jnp.tile(g.reshape(1, hd), (1, MXU_COLS // hd))

    q_scale = ATTN_SCALE * LOG2E
    def permuted_heads(lo, n_heads):
        return _head_perm(w_t[lo:lo + n_heads * hd].reshape(n_heads, hd, d), 1).reshape(n_heads * hd, d)
    w_qk = jnp.concatenate([permuted_heads(o_q, NSA_HEADS), permuted_heads(o_kv + 2 * gw, NSA_GROUPS),
                            permuted_heads(o_kv + 4 * gw, NSA_GROUPS)], axis=0)
    g_q = chunk_gain(q_scale * _head_perm(nsa_q_norm, 0))
    g_k = chunk_gain(_head_perm(nsa_k_norm, 0))
    rope_specs = [gspec, ones_spec] + tab_specs
    q_n = _matmul(h, w_qk, _ep_headnorm_rope, BF16, rows=(0, 1024, 2),
                  extras=(g_q, head_ones, c_t, s_t), extra_specs=rope_specs, name="proj_nsa_q")
    k_sw = _matmul(h, w_qk, _ep_headnorm_rope, BF16, rows=(NSA_HEADS * hd, 0, 1),
                   extras=(g_k, head_ones, c_t, s_t), extra_specs=rope_specs, name="proj_nsa_k")
    v_sw = _matmul(h, w_t, _ep_raw, BF16, tn=gw, rows=(o_kv + 3 * gw, 2 * gw, 2), name="proj_nsa_v")
    kv_c = _matmul(h, w_t, _ep_raw, F32, rows=(o_kv, 2 * gw, 1), name="proj_nsa_cmp")
    fq = _matmul(h, w_t, _ep_headnorm, BF16, rows=(o_fox, 1024, 2),
                 extras=(q_scale * fox_q_norm.reshape(1, hd),), extra_specs=[hspec], name="proj_fox_q")
    fk = _matmul(h, w_t, _ep_headnorm, BF16, rows=(o_fox + fw, 1024, 2),
                 extras=(fox_k_norm.reshape(1, hd),), extra_specs=[hspec], name="proj_fox_k")
    fv = _matmul(h, w_t, _ep_raw, BF16, rows=(o_fox + 2 * fw, 1024, 2), name="proj_fox_v")
    g_merge = _matmul(h, w_t, _ep_sigmoid, BF16, rows=(o_merge, 1024, 2 * d // 1024), name="proj_merge")

    per_group = NSA_REP * 3
    n_gate = NSA_HEADS * 3
    w_small = jnp.concatenate([w_t[o_gate:o_fox], w_t[o_f:o_merge]], axis=0)
    w_small = jnp.pad(w_small, ((0, LANES - w_small.shape[0]), (0, 0)))
    logits = _matmul(h, w_small, _ep_raw, F32, tn=LANES, rows=(0, 0, 1), name="proj_small")
    lane_blocks = [logits[:, g * per_group:(g + 1) * per_group] for g in range(NSA_GROUPS)]
    lane_blocks.append(logits[:, n_gate:n_gate + FOX_HEADS])
    small = jnp.concatenate([jnp.pad(blk, ((0, 0), (0, LANES - blk.shape[1]))) for blk in lane_blocks], axis=1)

    n_rows = seq // CMP_STRIDE
    end_pos = np.arange(n_rows) * CMP_STRIDE + CMP_BLOCK - 1
    ce, se = _rope_tables(end_pos)
    k_cmp, v_cmp = _compress(
        kv_c, seq, cmp_pos_k.reshape(2, CMP_STRIDE * hd), cmp_pos_v.reshape(2, CMP_STRIDE * hd),
        w_cmp_k1.astype(BF16), _head_perm(w_cmp_k2, 1).astype(BF16), w_cmp_v1.astype(BF16), w_cmp_v2.astype(BF16),
        _head_perm(nsa_k_norm, 0).reshape(1, hd), ce, se)

    n_slc = seq // SEL_BLOCK
    ci = np.arange(LANES)[:, None] * CMP_STRIDE
    sj = np.arange(LANES)[None, :] * SEL_BLOCK
    overlap = ((ci < sj + SEL_BLOCK) & (ci + CMP_BLOCK > sj) & (np.arange(LANES)[None, :] < n_slc)
               & (np.arange(LANES)[:, None] < n_rows - 1)).astype(np.float32)
    o_c, sel = _cmp_attention(q_n, k_cmp, v_cmp, jnp.asarray(overlap.T), seq)

    expand = (np.arange(LANES)[:, None] == (np.arange(seq)[None, :] >> SEL_SHIFT)).astype(np.float32)
    o_nsa = _nsa_attention(q_n, k_sw, v_sw, sel, jnp.asarray(expand, dtype=BF16), o_c, small, seq)

    f_bias = jnp.pad(b_forget, (0, LANES - FOX_HEADS)).reshape(1, LANES)
    cum = _cum_forget(small, f_bias, seq, NSA_GROUPS)
    ck = cum[:, :FOX_HEADS].reshape(bsz, seq, FOX_HEADS).transpose(0, 2, 1).reshape(bsz * FOX_HEADS, 1, seq)
    o_fox, up_nsa_b, up_fox_b, w_out_b, w_ff1_b = _fox_attention(
        fq, fk, fv, ck, seq, casts=(w_up_nsa, w_up_fox, w_out, w_ff1))

    y = _merge(o_nsa, up_nsa_b, o_fox, up_fox_b, g_merge)
    tn_out = 1024
    res_specs = [pl.BlockSpec((tm, tn_out), lambda j, i: (i, j)),
                 pl.BlockSpec((None, 1, tn_out), lambda j, i: (i // per_m, 0, j))]
    x_mid = _matmul(y, w_out_b, _ep_residual, F32, extras=(x2, gate1), extra_specs=res_specs,
                    tn=tn_out, name="out_proj")

    h2 = _normmod(x_mid, norm2_g.reshape(1, d), scale2, shift2, seq)
    hid, w_ff2_b = _matmul(h2, w_ff1_b, _ep_relu2, BF16, casts=(w_ff2,), name="ff1")
    return _matmul_k_residual(hid, w_ff2_b, x_mid, gate2, seq, name="ff2")


def kernel(x, c, w_ada, b_ada, norm1_g, norm2_g, w_in, b_forget, nsa_q_norm, nsa_k_norm, fox_q_norm, fox_k_norm, cmp_pos_k, cmp_pos_v, w_cmp_k1, w_cmp_k2, w_cmp_v1, w_cmp_v2, w_up_nsa, w_up_fox, w_out, w_ff1, w_ff2):
    bsz, seq, d = x.shape
    x2 = x.reshape(bsz * seq, d)
    c_pad = jnp.pad(c, ((0, 8 - bsz), (0, 0)))
    params = (w_ada, b_ada, norm1_g, norm2_g, w_in, b_forget, nsa_q_norm, nsa_k_norm, fox_q_norm, fox_k_norm,
              cmp_pos_k, cmp_pos_v, w_cmp_k1, w_cmp_k2, w_cmp_v1, w_cmp_v2, w_up_nsa, w_up_fox, w_out,
              w_ff1, w_ff2)
    for layer in range(w_ada.shape[0]):
        x2 = _layer(x2, c_pad, bsz, seq, *[p[layer] for p in params])
    return x2.reshape(bsz, seq, d)
```

```python
import functools

import numpy as np
import jax
import jax.numpy as jnp
from jax import lax
from jax.experimental import pallas as pl
from jax.experimental.pallas import tpu as pltpu

F32 = jnp.float32
BF16 = jnp.bfloat16

HEAD_DIM = 128
NSA_HEADS = 16
NSA_GROUPS = 4
NSA_REP = NSA_HEADS // NSA_GROUPS
FOX_HEADS = 16
CMP_BLOCK = 32
CMP_STRIDE = 16
SEL_BLOCK = 64
SEL_SHIFT = SEL_BLOCK.bit_length() - 1
SEL_TOPN = 16
WINDOW = 512
ROPE_THETA = 500000.0
ROT_DIM = HEAD_DIM // 4
RMS_EPS = 1e-6
ATTN_SCALE = HEAD_DIM ** -0.5

LOG2E = float(np.log2(np.e))
ROPE_HALF = ROT_DIM // 2
NEG = -1e30
BIG = 1e30
LANES = 128
MXU_COLS = 256
VMEM_LIMIT = 60 * 1024 * 1024
REGION_JOBS = 2


def _head_perm(x, axis):
    mid = HEAD_DIM // 2
    cuts = [(0, ROPE_HALF), (mid, mid + ROPE_HALF), (ROT_DIM, mid), (ROPE_HALF, ROT_DIM), (mid + ROPE_HALF, HEAD_DIM)]
    return jnp.concatenate([lax.slice_in_dim(x, a, b, axis=axis) for a, b in cuts], axis=axis)


def _cparams(*sem):
    return pltpu.CompilerParams(dimension_semantics=sem, vmem_limit_bytes=VMEM_LIMIT)


def _dot(a, b):
    return jnp.dot(a, b, preferred_element_type=F32)


def _dot_nt(a, b):
    return lax.dot_general(a, b, (((1,), (1,)), ((), ())), preferred_element_type=F32)


def _ada_kernel(c_ref, w_ref, b_ref, o_ref):
    c = c_ref[...]
    s = (c * jax.nn.sigmoid(c)).astype(BF16)
    o_ref[...] = _dot(s, w_ref[...].astype(BF16)) + b_ref[...]


def _ada(c_pad, w, b):
    rows, d = c_pad.shape
    n = w.shape[1]
    tn = 1024
    return pl.pallas_call(
        _ada_kernel,
        grid=(n // tn,),
        in_specs=[pl.BlockSpec((rows, d), lambda j: (0, 0)),
                  pl.BlockSpec((d, tn), lambda j: (0, j)),
                  pl.BlockSpec((1, tn), lambda j: (0, j))],
        out_specs=pl.BlockSpec((rows, tn), lambda j: (0, j)),
        out_shape=jax.ShapeDtypeStruct((rows, n), F32),
        compiler_params=_cparams("arbitrary"),
        name="ada",
    )(c_pad, w, b)


def _normmod_kernel(x_ref, g_ref, sc_ref, sh_ref, o_ref):
    x = x_ref[...]
    ms = jnp.mean(x * x, axis=-1, keepdims=True)
    y = x * lax.rsqrt(ms + RMS_EPS) * g_ref[...]
    o_ref[...] = (y * (1.0 + sc_ref[...]) + sh_ref[...]).astype(o_ref.dtype)


def _normmod(x2, g, scale, shift, seq):
    m, d = x2.shape
    tr = 512
    per = seq // tr
    return pl.pallas_call(
        _normmod_kernel,
        grid=(m // tr,),
        in_specs=[pl.BlockSpec((tr, d), lambda i: (i, 0)),
                  pl.BlockSpec((1, d), lambda i: (0, 0)),
                  pl.BlockSpec((None, 1, d), lambda i: (i // per, 0, 0)),
                  pl.BlockSpec((None, 1, d), lambda i: (i // per, 0, 0))],
        out_specs=pl.BlockSpec((tr, d), lambda i: (i, 0)),
        out_shape=jax.ShapeDtypeStruct((m, d), BF16),
        compiler_params=_cparams("arbitrary"),
        name="normmod",
    )(x2, g, scale, shift)


def _ep_raw(acc, o_ref):
    o_ref[...] = acc.astype(o_ref.dtype)


def _ep_sigmoid(acc, o_ref):
    o_ref[...] = jax.nn.sigmoid(acc).astype(o_ref.dtype)


def _ep_relu2(acc, o_ref):
    r = jnp.maximum(acc, 0.0)
    o_ref[...] = (r * r).astype(o_ref.dtype)


def _ep_residual(acc, o_ref, x_ref, g_ref):
    o_ref[...] = x_ref[...] + g_ref[...] * acc


def _head_norm(a, g):
    ms = jnp.mean(a * a, axis=-1, keepdims=True)
    return a * lax.rsqrt(ms + RMS_EPS) * g


def _rope(a, c, s):
    return a * c + pltpu.roll(a, HEAD_DIM // 2, 1) * s


def _chunk_head_norm(acc, g_ref, ones_ref):
    ss = _dot((acc * acc).astype(BF16), ones_ref[...])
    return acc * lax.rsqrt(ss * (1.0 / HEAD_DIM) + RMS_EPS) * g_ref[...]


def _ep_headnorm(acc, o_ref, g_ref):
    g = g_ref[...]
    for h0 in range(0, acc.shape[1], HEAD_DIM):
        o_ref[:, h0:h0 + HEAD_DIM] = _head_norm(acc[:, h0:h0 + HEAD_DIM], g).astype(o_ref.dtype)


def _ep_headnorm_rope(acc, o_ref, g_ref, ones_ref, c_ref, s_ref):
    c, s = c_ref[...], s_ref[...]
    for c0 in range(0, acc.shape[1], MXU_COLS):
        y = _chunk_head_norm(acc[:, c0:c0 + MXU_COLS], g_ref, ones_ref)
        for h0 in range(0, MXU_COLS, HEAD_DIM):
            dst = slice(c0 + h0, c0 + h0 + HEAD_DIM)
            o_ref[:, dst] = _rope(y[:, h0:h0 + HEAD_DIM], c, s).astype(o_ref.dtype)


def _cast_blocks(refs_in, refs_out):
    for src, dst in zip(refs_in, refs_out):
        dst[...] = src[...].astype(dst.dtype)


def _cast_specs(casts, steps, index_of_step):
    specs, shapes = [], []
    for w in casts:
        rows, cols = w.shape
        assert rows % (steps * 16) == 0
        specs.append(pl.BlockSpec((rows // steps, cols), lambda *g: (index_of_step(*g), 0)))
        shapes.append(jax.ShapeDtypeStruct((rows, cols), BF16))
    return specs, shapes


def _mm_kernel(a_ref, b_ref, *rest, epilogue, n_extra, n_cast, transposed, n_steps):
    extras = rest[:n_extra]
    cast_in = rest[n_extra:n_extra + n_cast]
    o_ref = rest[n_extra + n_cast]
    cast_out = rest[n_extra + n_cast + 1:n_extra + 2 * n_cast + 1]
    acc_sc = rest[-1]
    s = pl.program_id(0)

    def product():
        return _dot_nt(a_ref[...], b_ref[...]) if transposed else _dot(a_ref[...], b_ref[...])

    @pl.when(s == 0)
    def _():
        acc_sc[...] = product()
        _cast_blocks(cast_in, cast_out)

    @pl.when(jnp.logical_and(s > 0, s < n_steps))
    def _():
        epilogue(acc_sc[...], o_ref, *extras)
        acc_sc[...] = product()
        _cast_blocks(cast_in, cast_out)

    @pl.when(s == n_steps)
    def _():
        epilogue(acc_sc[...], o_ref, *extras)


def _matmul(a, b, epilogue, out_dtype, extras=(), extra_specs=(), tm=1024, tn=1024, name="mm",
            casts=(), rows=None):
    m, k = a.shape
    if rows is None:
        n = b.shape[1]
        tn = min(tn, n)
    else:
        first, stride, count = rows
        assert first % 16 == 0 and stride % 16 == 0
        n = count * tn
    n_m = m // tm
    n_steps = (n // tn) * n_m

    def tile(s):
        s = jnp.minimum(s, n_steps - 1)
        return s // n_m, s % n_m

    def done(s):
        return tile(jnp.maximum(s - 1, 0))

    if rows is None:
        b_spec = pl.BlockSpec((k, tn), lambda s: (0, tile(s)[0]))
    else:
        b_spec = pl.BlockSpec((pl.Element(tn), pl.Element(k)),
                              lambda s: (pl.multiple_of(first + tile(s)[0] * stride, 16), 0))
    late_specs = [pl.BlockSpec(sp.block_shape, functools.partial(lambda s, f: f(*done(s)), f=sp.index_map))
                  for sp in extra_specs]
    cast_specs, cast_shapes = _cast_specs(casts, n_steps, lambda s: jnp.minimum(s, n_steps - 1))
    out = pl.pallas_call(
        functools.partial(_mm_kernel, epilogue=epilogue, n_extra=len(extras), n_cast=len(casts),
                          transposed=rows is not None, n_steps=n_steps),
        grid=(n_steps + 1,),
        in_specs=[pl.BlockSpec((tm, k), lambda s: (tile(s)[1], 0)), b_spec] + late_specs + cast_specs,
        out_specs=[pl.BlockSpec((tm, tn), lambda s: done(s)[::-1])] + cast_specs,
        out_shape=[jax.ShapeDtypeStruct((m, n), out_dtype)] + cast_shapes,
        scratch_shapes=[pltpu.VMEM((tm, tn), F32)],
        compiler_params=_cparams("arbitrary"),
        name=name,
    )(a, b, *extras, *casts)
    return out if casts else out[0]


def _mmk_kernel(a_ref, b_ref, x_ref, g_ref, o_ref):
    kk = pl.program_id(2)

    @pl.when(kk == 0)
    def _():
        o_ref[...] = x_ref[...] + g_ref[...] * _dot(a_ref[...], b_ref[...])

    @pl.when(kk > 0)
    def _():
        o_ref[...] += g_ref[...] * _dot(a_ref[...], b_ref[...])


def _matmul_k_residual(a, b, x, gate, seq, tm=1024, tn=1024, tk=4096, name="mmk"):
    m, k = a.shape
    n = b.shape[1]
    per = seq // tm
    nk = k // tk
    return pl.pallas_call(
        _mmk_kernel,
        grid=(m // tm, n // tn, nk),
        in_specs=[pl.BlockSpec((tm, tk), lambda i, j, kk: (i, kk)),
                  pl.BlockSpec((tk, tn), lambda i, j, kk: (kk, j)),
                  pl.BlockSpec((tm, tn), lambda i, j, kk: (i, j)),
                  pl.BlockSpec((None, 1, tn), lambda i, j, kk: (i // per, 0, j))],
        out_specs=pl.BlockSpec((tm, tn), lambda i, j, kk: (i, j)),
        out_shape=jax.ShapeDtypeStruct((m, n), F32),
        compiler_params=_cparams("arbitrary", "arbitrary", "arbitrary"),
        name=name,
    )(a, b, x, gate)


def _compress_kernel(xk_ref, xv_ref, pek_ref, pev_ref, w1k_ref, w2k_ref, w1v_ref, w2v_ref,
                     g_ref, c_ref, s_ref, ko_ref, vo_ref):
    half = CMP_STRIDE * HEAD_DIM
    n_rows = xk_ref.shape[0] // CMP_STRIDE

    def comp(x_ref, pe_ref, w1_ref, w2_ref):
        x = jnp.concatenate([x_ref[pl.ds(l, n_rows, stride=CMP_STRIDE), :] for l in range(CMP_STRIDE)], axis=1)
        lo = (x + pe_ref[0:1, :]).astype(BF16)
        hi = (x + pe_ref[1:2, :]).astype(BF16)
        p = _dot(lo, w1_ref[0:half, :])
        q = _dot(hi, w1_ref[half:2 * half, :])
        h = p + pltpu.roll(q, q.shape[0] - 1, 0)
        h = jax.nn.gelu(h, approximate=True)
        return _dot(h.astype(BF16), w2_ref[...])

    kc = comp(xk_ref, pek_ref, w1k_ref, w2k_ref)
    kc = _rope(_head_norm(kc, g_ref[...]), c_ref[...], s_ref[...])
    ko_ref[...] = kc.astype(ko_ref.dtype)
    vo_ref[...] = comp(xv_ref, pev_ref, w1v_ref, w2v_ref).astype(vo_ref.dtype)


def _compress(kv_c, seq, pek, pev, w1k, w2k, w1v, w2v, g, c, s):
    b = kv_c.shape[0] // seq
    nr = seq // CMP_STRIDE
    kspec = pl.BlockSpec((seq, HEAD_DIM), lambda i, j: (i, j))
    vspec = pl.BlockSpec((seq, HEAD_DIM), lambda i, j: (i, NSA_GROUPS + j))
    full = lambda arr: pl.BlockSpec(arr.shape, lambda i, j: (0,) * arr.ndim)
    ospec = pl.BlockSpec((None, None, nr, HEAD_DIM), lambda i, j: (i, j, 0, 0))
    oshape = jax.ShapeDtypeStruct((b, NSA_GROUPS, nr, HEAD_DIM), BF16)
    return pl.pallas_call(
        _compress_kernel,
        grid=(b, NSA_GROUPS),
        in_specs=[kspec, vspec, full(pek), full(pev), full(w1k), full(w2k), full(w1v), full(w2v),
                  full(g), full(c), full(s)],
        out_specs=[ospec, ospec],
        out_shape=[oshape, oshape],
        compiler_params=_cparams("arbitrary", "arbitrary"),
        name="compress",
    )(kv_c, kv_c, pek, pev, w1k, w2k, w1v, w2v, g, c, s)


def _cmp_kernel(q_ref, k_ref, v_ref, ovt_ref, oc_ref, sel_ref, *, tt, n_slc):
    t0 = pl.program_id(2) * tt
    t = t0 + lax.broadcasted_iota(jnp.int32, (tt, LANES), 0)
    n = lax.broadcasted_iota(jnp.int32, (tt, LANES), 1)
    valid = n * CMP_STRIDE + (CMP_BLOCK - 1) <= t
    k = k_ref[...]
    v = v_ref[...]
    psum = jnp.zeros((tt, LANES), F32)
    for r in range(NSA_REP):
        sl = slice(r * HEAD_DIM, (r + 1) * HEAD_DIM)
        s = jnp.where(valid, _dot_nt(q_ref[:, sl], k), NEG)
        m = jnp.max(s, axis=-1, keepdims=True)
        p = jnp.where(valid, jnp.exp2(s - m), 0.0)
        d = jnp.sum(p, axis=-1, keepdims=True)
        p = p / jnp.where(d > 0, d, 1.0)
        psum = psum + p
        oc_ref[:, sl] = _dot(p.astype(BF16), v)
    imp = lax.dot_general(ovt_ref[...], psum, (((1,), (1,)), ((), ())), preferred_element_type=F32,
                          precision=lax.Precision.HIGHEST)[0:n_slc, :]
    j = lax.broadcasted_iota(jnp.int32, (n_slc, tt), 0)
    cur = (t0 + lax.broadcasted_iota(jnp.int32, (n_slc, tt), 1)) >> SEL_SHIFT
    score = jnp.where(j <= cur, imp, -BIG)
    score = jnp.where(j == 0, BIG, jnp.where(j == cur, BIG, jnp.where(j == cur - 1, BIG, score)))
    cnt = jnp.zeros((n_slc, tt), F32)
    for jp in range(n_slc):
        row = score[jp:jp + 1, :]
        ge = jnp.where(row >= score, 1.0, 0.0)
        gt = jnp.where(row > score, 1.0, 0.0)
        cnt = cnt + jnp.where(j > jp, ge, gt)
    sel = jnp.where(cnt < SEL_TOPN, jnp.where(score > -0.5 * BIG, 1.0, 0.0), 0.0)
    sel = jnp.concatenate([sel, jnp.zeros((LANES - n_slc, tt), F32)], axis=0)
    sel_ref[...] = sel.T.astype(sel_ref.dtype)


def _cmp_attention(q, kc, vc, overlap, seq):
    m, _ = q.shape
    b = m // seq
    tt = 2048
    per = seq // tt
    n_cmp =(seq - CMP_BLOCK) // CMP_STRIDE + 1
    assert n_cmp <= LANES and n_cmp * CMP_STRIDE + CMP_BLOCK - 1 >= seq
    kspec = pl.BlockSpec((None, None, LANES, HEAD_DIM), lambda i, g, j: (i, g, 0, 0))
    return pl.pallas_call(
        functools.partial(_cmp_kernel, tt=tt, n_slc=seq // SEL_BLOCK),
        grid=(b, NSA_GROUPS, per),
        in_specs=[pl.BlockSpec((tt, NSA_REP * HEAD_DIM), lambda i, g, j: (i * per + j, g)),
                  kspec, kspec,
                  pl.BlockSpec((LANES, LANES), lambda i, g, j: (0, 0))],
        out_specs=[pl.BlockSpec((tt, NSA_REP * HEAD_DIM), lambda i, g, j: (i * per + j, g)),
                   pl.BlockSpec((None, None, tt, LANES), lambda i, g, j: (i, g, j, 0))],
        out_shape=[jax.ShapeDtypeStruct((m, NSA_HEADS * HEAD_DIM), F32),
                   jax.ShapeDtypeStruct((b, NSA_GROUPS, seq, LANES), BF16)],
        compiler_params=_cparams("arbitrary", "arbitrary", "arbitrary"),
        name="cmp_attn",
    )(q, kc, vc, overlap)


def _softmax_pv(t, v):
    m = jnp.max(t, axis=-1, keepdims=True)
    p = jnp.exp2(t - m)
    l = jnp.sum(p, axis=-1, keepdims=True)
    return _dot(p.astype(BF16), v) / l


def _tri_bias(n, lower):
    row = lax.broadcasted_iota(jnp.int32, (n, n), 0)
    col = lax.broadcasted_iota(jnp.int32, (n, n), 1)
    return jnp.where(col <= row, 0.0, NEG) if lower else jnp.where(col > row, 0.0, NEG)


def _nsa_kernel(q_ref, ks_ref, vs_ref, kw_ref, vw_ref, sel_ref, ex_ref, oc_ref, gl_ref, o_ref,
                t_sc, p_sc, l_sc, bs_sc, bw_sc, os_sc, *, tq):
    seq = q_ref.shape[0]
    live = pl.program_id(0) >= 0
    n_back = WINDOW // tq
    w_cols = (n_back + 1) * tq
    bw_sc[:, 0:tq] = _tri_bias(tq, False)
    bw_sc[:, tq:n_back * tq] = jnp.zeros((tq, (n_back - 1) * tq), F32)
    bw_sc[:, n_back * tq:w_cols] = _tri_bias(tq, True)

    jobs = []
    for qt in reversed(range(seq // tq)):
        lo, hi = qt * tq, (qt + 1) * tq
        jobs.append(("sel", lo, hi, 0))
        jobs.append(("win", lo, hi, max(lo - WINDOW, 0)))

    assert REGION_JOBS % 2 == 0 and len(jobs) % REGION_JOBS == 0
    n_slots = 2 * REGION_JOBS

    def bias_slot(j):
        return (j // 2) % REGION_JOBS

    def head_rows(r):
        return slice(r * tq, (r + 1) * tq)

    def stage_scores(j):
        kind, lo, hi, klo = jobs[j]
        slot = j % n_slots
        n = hi - klo
        q4 = jnp.concatenate([q_ref[lo:hi, r * HEAD_DIM:(r + 1) * HEAD_DIM] for r in range(NSA_REP)], axis=0)
        k_ref = ks_ref if kind == "sel" else kw_ref
        t_sc[slot, :, 0:n] = _dot_nt(q4, k_ref[klo:hi, :])
        if kind == "sel":
            picked = _dot(sel_ref[lo:hi, :], ex_ref[:, 0:hi])
            bias = jnp.where(picked > 0.5, 0.0, NEG)
            if lo > 0:
                bs_sc[bias_slot(j), :, 0:lo] = bias[:, 0:lo]
            bs_sc[bias_slot(j), :, lo:hi] = bias[:, lo:hi] + _tri_bias(tq, True)

    def stage_softmax(j):
        kind, lo, hi, klo = jobs[j]
        slot = j % n_slots
        n = hi - klo
        for r in range(NSA_REP):
            bias = bs_sc[bias_slot(j), :, 0:n] if kind == "sel" else bw_sc[:, w_cols - n:w_cols]
            t = t_sc[slot, head_rows(r), 0:n] + bias
            p = jnp.exp2(t - jnp.max(t, axis=-1, keepdims=True))
            l_sc[slot, head_rows(r), :] = jnp.sum(p, axis=-1, keepdims=True)
            p_sc[slot, head_rows(r), 0:n] = p.astype(BF16)

    def stage_output(j):
        kind, lo, hi, klo = jobs[j]
        slot = j % n_slots
        n = hi - klo
        v_ref = vs_ref if kind == "sel" else vw_ref
        o4 = _dot(p_sc[slot, :, 0:n], v_ref[klo:hi, :]) / l_sc[slot]
        if kind == "sel":
            os_sc[...] = o4
            return
        gate = jax.nn.sigmoid(gl_ref[lo:hi, :])
        for r in range(NSA_REP):
            sl = slice(r * HEAD_DIM, (r + 1) * HEAD_DIM)
            o = (gate[:, 3 * r:3 * r + 1] * oc_ref[lo:hi, sl]
                 + gate[:, 3 * r + 1:3 * r + 2] * os_sc[head_rows(r), :]
                 + gate[:, 3 * r + 2:3 * r + 3] * o4[head_rows(r), :])
            o_ref[lo:hi, sl] = o.astype(o_ref.dtype)

    for r0 in range(0, len(jobs) + 2 * REGION_JOBS, REGION_JOBS):
        @pl.when(live)
        def _region():
            for j in range(r0, r0 + REGION_JOBS):
                if j < len(jobs):
                    stage_scores(j)
                if 0 <= j - REGION_JOBS < len(jobs):
                    stage_softmax(j - REGION_JOBS)
                if 0 <= j - 2 * REGION_JOBS < len(jobs):
                    stage_output(j - 2 * REGION_JOBS)


def _nsa_attention(q, kk, vv, sel, expand, oc, gl, seq):
    m = q.shape[0]
    b = m // seq
    width = NSA_REP * HEAD_DIM
    qspec = pl.BlockSpec((seq, width), lambda i, g: (i, g))
    ks = pl.BlockSpec((seq, HEAD_DIM), lambda i, g: (i, g))
    kw = pl.BlockSpec((seq, HEAD_DIM), lambda i, g: (i, NSA_GROUPS + g))
    tq = 128
    rows = NSA_REP * tq
    return pl.pallas_call(
        functools.partial(_nsa_kernel, tq=tq),
        grid=(b, NSA_GROUPS),
        scratch_shapes=[pltpu.VMEM((2 * REGION_JOBS, rows, seq), F32), pltpu.VMEM((2 * REGION_JOBS, rows, seq), BF16),
                        pltpu.VMEM((2 * REGION_JOBS, rows, 1), F32), pltpu.VMEM((REGION_JOBS, tq, seq), F32),
                        pltpu.VMEM((tq, WINDOW + tq), F32), pltpu.VMEM((rows, HEAD_DIM), F32)],
        in_specs=[qspec, ks, ks, kw, kw,
                  pl.BlockSpec((None, None, seq, LANES), lambda i, g: (i, g, 0, 0)),
                  pl.BlockSpec((LANES, seq), lambda i, g: (0, 0)),
                  qspec,
                  pl.BlockSpec((seq, LANES), lambda i, g: (i, g))],
        out_specs=qspec,
        out_shape=jax.ShapeDtypeStruct((m, NSA_HEADS * HEAD_DIM), BF16),
        compiler_params=_cparams("arbitrary", "arbitrary"),
        name="nsa_attn",
    )(q, kk, vv, kk, vv, sel, expand, oc, gl)


def _cum_kernel(f_ref, b_ref, o_ref):
    x = f_ref[...] + b_ref[...]
    ls = jnp.minimum(x, 0.0) - jnp.log1p(jnp.exp(-jnp.abs(x)))
    row = lax.broadcasted_iota(jnp.int32, ls.shape, 0)
    d = 1
    while d < ls.shape[0]:
        ls = ls + jnp.where(row >= d, pltpu.roll(ls, d, 0), 0.0)
        d *= 2
    o_ref[...] = ls


def _cum_forget(fl, bias, seq, col_block):
    m = fl.shape[0]
    return pl.pallas_call(
        _cum_kernel,
        grid=(m // seq,),
        in_specs=[pl.BlockSpec((seq, LANES), lambda i: (i, col_block)),
                  pl.BlockSpec((1, LANES), lambda i: (0, 0))],
        out_specs=pl.BlockSpec((seq, LANES), lambda i: (i, 0)),
        out_shape=jax.ShapeDtypeStruct((m, LANES), F32),
        compiler_params=_cparams("arbitrary"),
        name="cum_forget",
    )(fl, bias)


def _fox_kernel(q_ref, k_ref, v_ref, ck_ref, *rest, tq):
    n_cast = (len(rest) - 1) // 2
    o_ref = rest[n_cast]
    _cast_blocks(rest[:n_cast], rest[n_cast + 1:])
    seq = q_ref.shape[0]
    causal_b = _tri_bias(tq, True)
    ck = ck_ref[...] * LOG2E
    for qt in reversed(range(seq // tq)):
        lo, hi = qt * tq, (qt + 1) * tq
        t = _dot_nt(q_ref[lo:hi, :], k_ref[0:hi, :]) - ck[:, 0:hi]
        diag = t[:, lo:hi] + causal_b
        t = diag if qt == 0 else jnp.concatenate([t[:, 0:lo], diag], axis=1)
        o_ref[lo:hi, :] = _softmax_pv(t, v_ref[0:hi, :]).astype(o_ref.dtype)


def _fox_attention(q, k, v, ck, seq, casts=()):
    m = q.shape[0]
    b = m // seq
    blk = pl.BlockSpec((seq, HEAD_DIM), lambda i, h: (i, h))
    cast_specs, cast_shapes = _cast_specs(casts, b * FOX_HEADS, lambda i, h: i * FOX_HEADS + h)
    out = pl.pallas_call(
        functools.partial(_fox_kernel, tq=512),
        grid=(b, FOX_HEADS),
        in_specs=[blk, blk, blk,
                  pl.BlockSpec((None, 1, seq), lambda i, h: (i * FOX_HEADS + h, 0, 0))] + cast_specs,
        out_specs=[blk] + cast_specs,
        out_shape=[jax.ShapeDtypeStruct((m, FOX_HEADS * HEAD_DIM), BF16)] + cast_shapes,
        compiler_params=_cparams("arbitrary", "arbitrary"),
        name="fox_attn",
    )(q, k, v, ck, *casts)
    return out if casts else out[0]


def _merge_kernel(a1_ref, w1_ref, a2_ref, w2_ref, g1_ref, g2_ref, o_ref):
    u1 = _dot(a1_ref[...], w1_ref[...])
    u2 = _dot(a2_ref[...], w2_ref[...])
    o_ref[...] = (g1_ref[...].astype(F32) * u1 + g2_ref[...].astype(F32) * u2).astype(o_ref.dtype)


def _merge(a1, w1, a2, w2, gates):
    m, k = a1.shape
    n = w1.shape[1]
    tm, tn = 1024, 1024
    nb = n // tn
    aspec = pl.BlockSpec((tm, k), lambda j, i: (i, 0))
    wspec = pl.BlockSpec((k, tn), lambda j, i: (0, j))
    return pl.pallas_call(
        _merge_kernel,
        grid=(nb, m // tm),
        in_specs=[aspec, wspec, aspec, wspec,
                  pl.BlockSpec((tm, tn), lambda j, i: (i, j)),
                  pl.BlockSpec((tm, tn), lambda j, i: (i, nb + j))],
        out_specs=pl.BlockSpec((tm, tn), lambda j, i: (i, j)),
        out_shape=jax.ShapeDtypeStruct((m, n), BF16),
        compiler_params=_cparams("arbitrary", "arbitrary"),
        name="merge",
    )(a1, w1, a2, w2, gates, gates)


def _rope_tables(pos):
    inv = ROPE_THETA ** (-jnp.arange(ROPE_HALF, dtype=F32) / ROPE_HALF)
    ang = jnp.asarray(pos, dtype=F32)[:, None] * inv[None, :]
    cos, sin = jnp.cos(ang), jnp.sin(ang)
    n = ang.shape[0]
    gap = HEAD_DIM // 2 - ROPE_HALF
    c = jnp.concatenate([cos, jnp.ones((n, gap), F32), cos, jnp.ones((n, gap), F32)], axis=1)
    s = jnp.concatenate([-sin, jnp.zeros((n, gap), F32), sin, jnp.zeros((n, gap), F32)], axis=1)
    return c, s


def _layer(x2, c_pad, bsz, seq, w_ada, b_ada, norm1_g, norm2_g, w_in, b_forget, nsa_q_norm, nsa_k_norm,
           fox_q_norm, fox_k_norm, cmp_pos_k, cmp_pos_v, w_cmp_k1, w_cmp_k2, w_cmp_v1, w_cmp_v2,
           w_up_nsa, w_up_fox, w_out, w_ff1, w_ff2):
    d = x2.shape[1]
    hd = HEAD_DIM
    tm = 1024
    per_m = seq // tm

    mod = _ada(c_pad, w_ada, b_ada.reshape(1, -1))[:bsz]
    shift1, scale1, gate1, shift2, scale2, gate2 = [t.reshape(bsz, 1, d) for t in jnp.split(mod, 6, axis=-1)]

    h = _normmod(x2, norm1_g.reshape(1, d), scale1, shift1, seq)

    o_q = 0
    o_kv = o_q + NSA_HEADS * hd
    o_gate = o_kv + 3 * 2 * NSA_GROUPS * hd
    o_fox = o_gate + 3 * NSA_HEADS
    o_f = o_fox + 3 * FOX_HEADS * hd
    o_merge = o_f + FOX_HEADS
    gw = NSA_GROUPS * hd

    w_t = w_in.T.astype(BF16)
    fw = FOX_HEADS * hd

    pos = np.arange(seq)
    c_t, s_t = _rope_tables(pos)
    tab_specs = [pl.BlockSpec((tm, hd), lambda j, i: (i % per_m, 0))] * 2
    gspec = pl.BlockSpec((1, MXU_COLS), lambda j, i: (0, 0))
    hspec = pl.BlockSpec((1, hd), lambda j, i: (0, 0))
    ones_spec = pl.BlockSpec((MXU_COLS, MXU_COLS), lambda j, i: (0, 0))
    head_ones = jnp.asarray(np.kron(np.eye(MXU_COLS // hd), np.ones((hd, hd))), dtype=BF16)

    def chunk_gain(g):
        return jnp.tile(g.reshape(1, hd), (1, MXU_COLS // hd))

    q_scale = ATTN_SCALE * LOG2E
    def permuted_heads(lo, n_heads):
        return _head_perm(w_t[lo:lo + n_heads * hd].reshape(n_heads, hd, d), 1).reshape(n_heads * hd, d)
    w_qk = jnp.concatenate([permuted_heads(o_q, NSA_HEADS), permuted_heads(o_kv + 2 * gw, NSA_GROUPS),
                            permuted_heads(o_kv + 4 * gw, NSA_GROUPS)], axis=0)
    g_q = chunk_gain(q_scale * _head_perm(nsa_q_norm, 0))
    g_k = chunk_gain(_head_perm(nsa_k_norm, 0))
    rope_specs = [gspec, ones_spec] + tab_specs
    q_n = _matmul(h, w_qk, _ep_headnorm_rope, BF16, rows=(0, 1024, 2),
                  extras=(g_q, head_ones, c_t, s_t), extra_specs=rope_specs, name="proj_nsa_q")
    k_sw = _matmul(h, w_qk, _ep_headnorm_rope, BF16, rows=(NSA_HEADS * hd, 0, 1),
                   extras=(g_k, head_ones, c_t, s_t), extra_specs=rope_specs, name="proj_nsa_k")
    v_sw = _matmul(h, w_t, _ep_raw, BF16, tn=gw, rows=(o_kv + 3 * gw, 2 * gw, 2), name="proj_nsa_v")
    kv_c = _matmul(h, w_t, _ep_raw, F32, rows=(o_kv, 2 * gw, 1), name="proj_nsa_cmp")
    fq = _matmul(h, w_t, _ep_headnorm, BF16, rows=(o_fox, 1024, 2),
                 extras=(q_scale * fox_q_norm.reshape(1, hd),), extra_specs=[hspec], name="proj_fox_q")
    fk = _matmul(h, w_t, _ep_headnorm, BF16, rows=(o_fox + fw, 1024, 2),
                 extras=(fox_k_norm.reshape(1, hd),), extra_specs=[hspec], name="proj_fox_k")
    fv = _matmul(h, w_t, _ep_raw, BF16, rows=(o_fox + 2 * fw, 1024, 2), name="proj_fox_v")
    g_merge = _matmul(h, w_t, _ep_sigmoid, BF16, rows=(o_merge, 1024, 2 * d // 1024), name="proj_merge")

    per_group = NSA_REP * 3
    n_gate = NSA_HEADS * 3
    w_small = jnp.concatenate([w_t[o_gate:o_fox], w_t[o_f:o_merge]], axis=0)
    w_small = jnp.pad(w_small, ((0, LANES - w_small.shape[0]), (0, 0)))
    logits = _matmul(h, w_small, _ep_raw, F32, tn=LANES, rows=(0, 0, 1), name="proj_small")
    lane_blocks = [logits[:, g * per_group:(g + 1) * per_group] for g in range(NSA_GROUPS)]
    lane_blocks.append(logits[:, n_gate:n_gate + FOX_HEADS])
    small = jnp.concatenate([jnp.pad(blk, ((0, 0), (0, LANES - blk.shape[1]))) for blk in lane_blocks], axis=1)

    n_rows = seq // CMP_STRIDE
    end_pos = np.arange(n_rows) * CMP_STRIDE + CMP_BLOCK - 1
    ce, se = _rope_tables(end_pos)
    k_cmp, v_cmp = _compress(
        kv_c, seq, cmp_pos_k.reshape(2, CMP_STRIDE * hd), cmp_pos_v.reshape(2, CMP_STRIDE * hd),
        w_cmp_k1.astype(BF16), _head_perm(w_cmp_k2, 1).astype(BF16), w_cmp_v1.astype(BF16), w_cmp_v2.astype(BF16),
        _head_perm(nsa_k_norm, 0).reshape(1, hd), ce, se)

    n_slc = seq // SEL_BLOCK
    ci = np.arange(LANES)[:, None] * CMP_STRIDE
    sj = np.arange(LANES)[None, :] * SEL_BLOCK
    overlap = ((ci < sj + SEL_BLOCK) & (ci + CMP_BLOCK > sj) & (np.arange(LANES)[None, :] < n_slc)
               & (np.arange(LANES)[:, None] < n_rows - 1)).astype(np.float32)
    o_c, sel = _cmp_attention(q_n, k_cmp, v_cmp, jnp.asarray(overlap.T), seq)

    expand = (np.arange(LANES)[:, None] == (np.arange(seq)[None, :] >> SEL_SHIFT)).astype(np.float32)
    o_nsa = _nsa_attention(q_n, k_sw, v_sw, sel, jnp.asarray(expand, dtype=BF16), o_c, small, seq)

    f_bias = jnp.pad(b_forget, (0, LANES - FOX_HEADS)).reshape(1, LANES)
    cum = _cum_forget(small, f_bias, seq, NSA_GROUPS)
    ck = cum[:, :FOX_HEADS].reshape(bsz, seq, FOX_HEADS).transpose(0, 2, 1).reshape(bsz * FOX_HEADS, 1, seq)
    o_fox, up_nsa_b, up_fox_b, w_out_b, w_ff1_b = _fox_attention(
        fq, fk, fv, ck, seq, casts=(w_up_nsa, w_up_fox, w_out, w_ff1))

    y = _merge(o_nsa, up_nsa_b, o_fox, up_fox_b, g_merge)
    tn_out = 1024
    res_specs = [pl.BlockSpec((tm, tn_out), lambda j, i: (i, j)),
                 pl.BlockSpec((None, 1, tn_out), lambda j, i: (i // per_m, 0, j))]
    x_mid = _matmul(y, w_out_b, _ep_residual, F32, extras=(x2, gate1), extra_specs=res_specs,
                    tn=tn_out, name="out_proj")

    h2 = _normmod(x_mid, norm2_g.reshape(1, d), scale2, shift2, seq)
    hid, w_ff2_b = _matmul(h2, w_ff1_b, _ep_relu2, BF16, casts=(w_ff2,), name="ff1")
    return _matmul_k_residual(hid, w_ff2_b, x_mid, gate2, seq, name="ff2")


def kernel(x, c, w_ada, b_ada, norm1_g, norm2_g, w_in, b_forget, nsa_q_norm, nsa_k_norm, fox_q_norm, fox_k_norm, cmp_pos_k, cmp_pos_v, w_cmp_k1, w_cmp_k2, w_cmp_v1, w_cmp_v2, w_up_nsa, w_up_fox, w_out, w_ff1, w_ff2):
    bsz, seq, d = x.shape
    x2 = x.reshape(bsz * seq, d)
    c_pad = jnp.pad(c, ((0, 8 - bsz), (0, 0)))
    params = (w_ada, b_ada, norm1_g, norm2_g, w_in, b_forget, nsa_q_norm, nsa_k_norm, fox_q_norm, fox_k_norm,
              cmp_pos_k, cmp_pos_v, w_cmp_k1, w_cmp_k2, w_cmp_v1, w_cmp_v2, w_up_nsa, w_up_fox, w_out,
              w_ff1, w_ff2)
    for layer in range(w_ada.shape[0]):
        x2 = _layer(x2, c_pad, bsz, seq, *[p[layer] for p in params])
    return x2.reshape(bsz, seq, d)
```

```python
import functools

import numpy as np
import jax
import jax.numpy as jnp
from jax import lax
from jax.experimental import pallas as pl
from jax.experimental.pallas import tpu as pltpu

F32 = jnp.float32
BF16 = jnp.bfloat16

HEAD_DIM = 128
NSA_HEADS = 16
NSA_GROUPS = 4
NSA_REP = NSA_HEADS // NSA_GROUPS
FOX_HEADS = 16
CMP_BLOCK = 32
CMP_STRIDE = 16
SEL_BLOCK = 64
SEL_SHIFT = SEL_BLOCK.bit_length() - 1
SEL_TOPN = 16
WINDOW = 512
ROPE_THETA = 500000.0
ROT_DIM = HEAD_DIM // 4
RMS_EPS = 1e-6
ATTN_SCALE = HEAD_DIM ** -0.5

LOG2E = float(np.log2(np.e))
ROPE_HALF = ROT_DIM // 2
NEG = -1e30
BIG = 1e30
LANES = 128
MXU_COLS = 256
VMEM_LIMIT = 60 * 1024 * 1024
REGION_JOBS = 2


def _head_perm(x, axis):
    mid = HEAD_DIM // 2
    cuts = [(0, ROPE_HALF), (mid, mid + ROPE_HALF), (ROT_DIM, mid), (ROPE_HALF, ROT_DIM), (mid + ROPE_HALF, HEAD_DIM)]
    return jnp.concatenate([lax.slice_in_dim(x, a, b, axis=axis) for a, b in cuts], axis=axis)


def _cparams(*sem):
    return pltpu.CompilerParams(dimension_semantics=sem, vmem_limit_bytes=VMEM_LIMIT)


def _dot(a, b):
    return jnp.dot(a, b, preferred_element_type=F32)


def _dot_nt(a, b):
    return lax.dot_general(a, b, (((1,), (1,)), ((), ())), preferred_element_type=F32)


def _ada_kernel(c_ref, w_ref, b_ref, o_ref):
    c = c_ref[...]
    s = (c * jax.nn.sigmoid(c)).astype(BF16)
    o_ref[...] = _dot(s, w_ref[...].astype(BF16)) + b_ref[...]


def _ada(c_pad, w, b):
    rows, d = c_pad.shape
    n = w.shape[1]
    tn = 1024
    return pl.pallas_call(
        _ada_kernel,
        grid=(n // tn,),
        in_specs=[pl.BlockSpec((rows, d), lambda j: (0, 0)),
                  pl.BlockSpec((d, tn), lambda j: (0, j)),
                  pl.BlockSpec((1, tn), lambda j: (0, j))],
        out_specs=pl.BlockSpec((rows, tn), lambda j: (0, j)),
        out_shape=jax.ShapeDtypeStruct((rows, n), F32),
        compiler_params=_cparams("arbitrary"),
        name="ada",
    )(c_pad, w, b)


def _normmod_kernel(x_ref, g_ref, sc_ref, sh_ref, o_ref):
    x = x_ref[...]
    ms = jnp.mean(x * x, axis=-1, keepdims=True)
    y = x * lax.rsqrt(ms + RMS_EPS) * g_ref[...]
    o_ref[...] = (y * (1.0 + sc_ref[...]) + sh_ref[...]).astype(o_ref.dtype)


def _normmod(x2, g, scale, shift, seq):
    m, d = x2.shape
    tr = 512
    per = seq // tr
    return pl.pallas_call(
        _normmod_kernel,
        grid=(m // tr,),
        in_specs=[pl.BlockSpec((tr, d), lambda i: (i, 0)),
                  pl.BlockSpec((1, d), lambda i: (0, 0)),
                  pl.BlockSpec((None, 1, d), lambda i: (i // per, 0, 0)),
                  pl.BlockSpec((None, 1, d), lambda i: (i // per, 0, 0))],
        out_specs=pl.BlockSpec((tr, d), lambda i: (i, 0)),
        out_shape=jax.ShapeDtypeStruct((m, d), BF16),
        compiler_params=_cparams("arbitrary"),
        name="normmod",
    )(x2, g, scale, shift)


def _ep_raw(acc, o_ref):
    o_ref[...] = acc.astype(o_ref.dtype)


def _ep_sigmoid(acc, o_ref):
    o_ref[...] = jax.nn.sigmoid(acc).astype(o_ref.dtype)


def _ep_relu2(acc, o_ref):
    r = jnp.maximum(acc, 0.0)
    o_ref[...] = (r * r).astype(o_ref.dtype)


def _ep_residual(acc, o_ref, x_ref, g_ref):
    o_ref[...] = x_ref[...] + g_ref[...] * acc


def _head_norm(a, g):
    ms = jnp.mean(a * a, axis=-1, keepdims=True)
    return a * lax.rsqrt(ms + RMS_EPS) * g


def _rope(a, c, s):
    return a * c + pltpu.roll(a, HEAD_DIM // 2, 1) * s


def _chunk_head_norm(acc, g_ref, ones_ref):
    ss = _dot((acc * acc).astype(BF16), ones_ref[...])
    return acc * lax.rsqrt(ss * (1.0 / HEAD_DIM) + RMS_EPS) * g_ref[...]


def _ep_headnorm(acc, o_ref, g_ref):
    g = g_ref[...]
    for h0 in range(0, acc.shape[1], HEAD_DIM):
        o_ref[:, h0:h0 + HEAD_DIM] = _head_norm(acc[:, h0:h0 + HEAD_DIM], g).astype(o_ref.dtype)


def _ep_headnorm_rope(acc, o_ref, g_ref, ones_ref, c_ref, s_ref):
    c, s = c_ref[...], s_ref[...]
    for c0 in range(0, acc.shape[1], MXU_COLS):
        y = _chunk_head_norm(acc[:, c0:c0 + MXU_COLS], g_ref, ones_ref)
        for h0 in range(0, MXU_COLS, HEAD_DIM):
            dst = slice(c0 + h0, c0 + h0 + HEAD_DIM)
            o_ref[:, dst] = _rope(y[:, h0:h0 + HEAD_DIM], c, s).astype(o_ref.dtype)


def _cast_blocks(refs_in, refs_out):
    for src, dst in zip(refs_in, refs_out):
        dst[...] = src[...].astype(dst.dtype)


def _cast_specs(casts, steps, index_of_step):
    specs, shapes = [], []
    for w in casts:
        rows, cols = w.shape
        assert rows % (steps * 16) == 0
        specs.append(pl.BlockSpec((rows // steps, cols), lambda *g: (index_of_step(*g), 0)))
        shapes.append(jax.ShapeDtypeStruct((rows, cols), BF16))
    return specs, shapes


def _mm_kernel(a_ref, b_ref, *rest, epilogue, n_extra, n_cast, transposed, n_steps, late):
    extras = rest[:n_extra]
    cast_in = rest[n_extra:n_extra + n_cast]
    o_ref = rest[n_extra + n_cast]
    cast_out = rest[n_extra + n_cast + 1:n_extra + 2 * n_cast + 1]

    def product():
        return _dot_nt(a_ref[...], b_ref[...]) if transposed else _dot(a_ref[...], b_ref[...])

    if not late:
        epilogue(product(), o_ref, *extras)
        _cast_blocks(cast_in, cast_out)
        return
    acc_sc = rest[-1]
    s = pl.program_id(0)

    @pl.when(s == 0)
    def _():
        acc_sc[...] = product()
        _cast_blocks(cast_in, cast_out)

    @pl.when(jnp.logical_and(s > 0, s < n_steps))
    def _():
        epilogue(acc_sc[...], o_ref, *extras)
        acc_sc[...] = product()
        _cast_blocks(cast_in, cast_out)

    @pl.when(s == n_steps)
    def _():
        epilogue(acc_sc[...], o_ref, *extras)


def _matmul(a, b, epilogue, out_dtype, extras=(), extra_specs=(), tm=1024, tn=1024, name="mm",
            casts=(), rows=None, late=False):
    lag = 1 if late else 0
    m, k = a.shape
    if rows is None:
        n = b.shape[1]
        tn = min(tn, n)
    else:
        first, stride, count = rows
        assert first % 16 == 0 and stride % 16 == 0
        n = count * tn
    n_m = m // tm
    n_steps = (n // tn) * n_m

    def tile(s):
        s = jnp.minimum(s, n_steps - 1)
        return s // n_m, s % n_m

    def done(s):
        return tile(jnp.maximum(s - lag, 0))

    if rows is None:
        b_spec = pl.BlockSpec((k, tn), lambda s: (0, tile(s)[0]))
    else:
        b_spec = pl.BlockSpec((pl.Element(tn), pl.Element(k)),
                              lambda s: (pl.multiple_of(first + tile(s)[0] * stride, 16), 0))
    late_specs = [pl.BlockSpec(sp.block_shape, functools.partial(lambda s, f: f(*done(s)), f=sp.index_map))
                  for sp in extra_specs]
    cast_specs, cast_shapes = _cast_specs(casts, n_steps, lambda s: jnp.minimum(s, n_steps - 1))
    out = pl.pallas_call(
        functools.partial(_mm_kernel, epilogue=epilogue, n_extra=len(extras), n_cast=len(casts),
                          transposed=rows is not None, n_steps=n_steps, late=late),
        grid=(n_steps + lag,),
        in_specs=[pl.BlockSpec((tm, k), lambda s: (tile(s)[1], 0)), b_spec] + late_specs + cast_specs,
        out_specs=[pl.BlockSpec((tm, tn), lambda s: done(s)[::-1])] + cast_specs,
        out_shape=[jax.ShapeDtypeStruct((m, n), out_dtype)] + cast_shapes,
        scratch_shapes=[pltpu.VMEM((tm, tn), F32)] if late else [],
        compiler_params=_cparams("arbitrary"),
        name=name,
    )(a, b, *extras, *casts)
    return out if casts else out[0]


def _mmk_kernel(a_ref, b_ref, x_ref, g_ref, o_ref):
    kk = pl.program_id(2)

    @pl.when(kk == 0)
    def _():
        o_ref[...] = x_ref[...] + g_ref[...] * _dot(a_ref[...], b_ref[...])

    @pl.when(kk > 0)
    def _():
        o_ref[...] += g_ref[...] * _dot(a_ref[...], b_ref[...])


def _matmul_k_residual(a, b, x, gate, seq, tm=1024, tn=1024, tk=4096, name="mmk"):
    m, k = a.shape
    n = b.shape[1]
    per = seq // tm
    nk = k // tk
    return pl.pallas_call(
        _mmk_kernel,
        grid=(m // tm, n // tn, nk),
        in_specs=[pl.BlockSpec((tm, tk), lambda i, j, kk: (i, kk)),
                  pl.BlockSpec((tk, tn), lambda i, j, kk: (kk, j)),
                  pl.BlockSpec((tm, tn), lambda i, j, kk: (i, j)),
                  pl.BlockSpec((None, 1, tn), lambda i, j, kk: (i // per, 0, j))],
        out_specs=pl.BlockSpec((tm, tn), lambda i, j, kk: (i, j)),
        out_shape=jax.ShapeDtypeStruct((m, n), F32),
        compiler_params=_cparams("arbitrary", "arbitrary", "arbitrary"),
        name=name,
    )(a, b, x, gate)


def _compress_kernel(xk_ref, xv_ref, pek_ref, pev_ref, w1k_ref, w2k_ref, w1v_ref, w2v_ref,
                     g_ref, c_ref, s_ref, ko_ref, vo_ref):
    half = CMP_STRIDE * HEAD_DIM
    n_rows = xk_ref.shape[0] // CMP_STRIDE

    def comp(x_ref, pe_ref, w1_ref, w2_ref):
        x = jnp.concatenate([x_ref[pl.ds(l, n_rows, stride=CMP_STRIDE), :] for l in range(CMP_STRIDE)], axis=1)
        lo = (x + pe_ref[0:1, :]).astype(BF16)
        hi = (x + pe_ref[1:2, :]).astype(BF16)
        p = _dot(lo, w1_ref[0:half, :])
        q = _dot(hi, w1_ref[half:2 * half, :])
        h = p + pltpu.roll(q, q.shape[0] - 1, 0)
        h = jax.nn.gelu(h, approximate=True)
        return _dot(h.astype(BF16), w2_ref[...])

    kc = comp(xk_ref, pek_ref, w1k_ref, w2k_ref)
    kc = _rope(_head_norm(kc, g_ref[...]), c_ref[...], s_ref[...])
    ko_ref[...] = kc.astype(ko_ref.dtype)
    vo_ref[...] = comp(xv_ref, pev_ref, w1v_ref, w2v_ref).astype(vo_ref.dtype)


def _compress(kv_c, seq, pek, pev, w1k, w2k, w1v, w2v, g, c, s):
    b = kv_c.shape[0] // seq
    nr = seq // CMP_STRIDE
    kspec = pl.BlockSpec((seq, HEAD_DIM), lambda i, j: (i, j))
    vspec = pl.BlockSpec((seq, HEAD_DIM), lambda i, j: (i, NSA_GROUPS + j))
    full = lambda arr: pl.BlockSpec(arr.shape, lambda i, j: (0,) * arr.ndim)
    ospec = pl.BlockSpec((None, None, nr, HEAD_DIM), lambda i, j: (i, j, 0, 0))
    oshape = jax.ShapeDtypeStruct((b, NSA_GROUPS, nr, HEAD_DIM), BF16)
    return pl.pallas_call(
        _compress_kernel,
        grid=(b, NSA_GROUPS),
        in_specs=[kspec, vspec, full(pek), full(pev), full(w1k), full(w2k), full(w1v), full(w2v),
                  full(g), full(c), full(s)],
        out_specs=[ospec, ospec],
        out_shape=[oshape, oshape],
        compiler_params=_cparams("arbitrary", "arbitrary"),
        name="compress",
    )(kv_c, kv_c, pek, pev, w1k, w2k, w1v, w2v, g, c, s)


def _cmp_kernel(q_ref, k_ref, v_ref, ovt_ref, oc_ref, sel_ref, *, tt, n_slc):
    t0 = pl.program_id(2) * tt
    t = t0 + lax.broadcasted_iota(jnp.int32, (tt, LANES), 0)
    n = lax.broadcasted_iota(jnp.int32, (tt, LANES), 1)
    valid = n * CMP_STRIDE + (CMP_BLOCK - 1) <= t
    k = k_ref[...]
    v = v_ref[...]
    psum = jnp.zeros((tt, LANES), F32)
    for r in range(NSA_REP):
        sl = slice(r * HEAD_DIM, (r + 1) * HEAD_DIM)
        s = jnp.where(valid, _dot_nt(q_ref[:, sl], k), NEG)
        m = jnp.max(s, axis=-1, keepdims=True)
        p = jnp.where(valid, jnp.exp2(s - m), 0.0)
        d = jnp.sum(p, axis=-1, keepdims=True)
        p = p / jnp.where(d > 0, d, 1.0)
        psum = psum + p
        oc_ref[:, sl] = _dot(p.astype(BF16), v)
    imp = lax.dot_general(ovt_ref[...], psum, (((1,), (1,)), ((), ())), preferred_element_type=F32,
                          precision=lax.Precision.HIGHEST)[0:n_slc, :]
    j = lax.broadcasted_iota(jnp.int32, (n_slc, tt), 0)
    cur = (t0 + lax.broadcasted_iota(jnp.int32, (n_slc, tt), 1)) >> SEL_SHIFT
    score = jnp.where(j <= cur, imp, -BIG)
    score = jnp.where(j == 0, BIG, jnp.where(j == cur, BIG, jnp.where(j == cur - 1, BIG, score)))
    cnt = jnp.zeros((n_slc, tt), F32)
    for jp in range(n_slc):
        row = score[jp:jp + 1, :]
        ge = jnp.where(row >= score, 1.0, 0.0)
        gt = jnp.where(row > score, 1.0, 0.0)
        cnt = cnt + jnp.where(j > jp, ge, gt)
    sel = jnp.where(cnt < SEL_TOPN, jnp.where(score > -0.5 * BIG, 1.0, 0.0), 0.0)
    sel = jnp.concatenate([sel, jnp.zeros((LANES - n_slc, tt), F32)], axis=0)
    sel_ref[...] = sel.T.astype(sel_ref.dtype)


def _cmp_attention(q, kc, vc, overlap, seq):
    m, _ = q.shape
    b = m // seq
    tt = 2048
    per = seq // tt
    n_cmp =(seq - CMP_BLOCK) // CMP_STRIDE + 1
    assert n_cmp <= LANES and n_cmp * CMP_STRIDE + CMP_BLOCK - 1 >= seq
    kspec = pl.BlockSpec((None, None, LANES, HEAD_DIM), lambda i, g, j: (i, g, 0, 0))
    return pl.pallas_call(
        functools.partial(_cmp_kernel, tt=tt, n_slc=seq // SEL_BLOCK),
        grid=(b, NSA_GROUPS, per),
        in_specs=[pl.BlockSpec((tt, NSA_REP * HEAD_DIM), lambda i, g, j: (i * per + j, g)),
                  kspec, kspec,
                  pl.BlockSpec((LANES, LANES), lambda i, g, j: (0, 0))],
        out_specs=[pl.BlockSpec((tt, NSA_REP * HEAD_DIM), lambda i, g, j: (i * per + j, g)),
                   pl.BlockSpec((None, None, tt, LANES), lambda i, g, j: (i, g, j, 0))],
        out_shape=[jax.ShapeDtypeStruct((m, NSA_HEADS * HEAD_DIM), F32),
                   jax.ShapeDtypeStruct((b, NSA_GROUPS, seq, LANES), BF16)],
        compiler_params=_cparams("arbitrary", "arbitrary", "arbitrary"),
        name="cmp_attn",
    )(q, kc, vc, overlap)


def _softmax_pv(t, v):
    m = jnp.max(t, axis=-1, keepdims=True)
    p = jnp.exp2(t - m)
    l = jnp.sum(p, axis=-1, keepdims=True)
    return _dot(p.astype(BF16), v) / l


def _tri_bias(n, lower):
    row = lax.broadcasted_iota(jnp.int32, (n, n), 0)
    col = lax.broadcasted_iota(jnp.int32, (n, n), 1)
    return jnp.where(col <= row, 0.0, NEG) if lower else jnp.where(col > row, 0.0, NEG)


def _nsa_kernel(q_ref, ks_ref, vs_ref, kw_ref, vw_ref, sel_ref, ex_ref, oc_ref, gl_ref, o_ref,
                t_sc, p_sc, l_sc, bs_sc, bw_sc, os_sc, *, tq):
    seq = q_ref.shape[0]
    live = pl.program_id(0) >= 0
    n_back = WINDOW // tq
    w_cols = (n_back + 1) * tq
    bw_sc[:, 0:tq] = _tri_bias(tq, False)
    bw_sc[:, tq:n_back * tq] = jnp.zeros((tq, (n_back - 1) * tq), F32)
    bw_sc[:, n_back * tq:w_cols] = _tri_bias(tq, True)

    jobs = []
    for qt in reversed(range(seq // tq)):
        lo, hi = qt * tq, (qt + 1) * tq
        jobs.append(("sel", lo, hi, 0))
        jobs.append(("win", lo, hi, max(lo - WINDOW, 0)))

    assert REGION_JOBS % 2 == 0 and len(jobs) % REGION_JOBS == 0
    n_slots = 2 * REGION_JOBS

    def bias_slot(j):
        return (j // 2) % REGION_JOBS

    def head_rows(r):
        return slice(r * tq, (r + 1) * tq)

    def stage_scores(j):
        kind, lo, hi, klo = jobs[j]
        slot = j % n_slots
        n = hi - klo
        q4 = jnp.concatenate([q_ref[lo:hi, r * HEAD_DIM:(r + 1) * HEAD_DIM] for r in range(NSA_REP)], axis=0)
        k_ref = ks_ref if kind == "sel" else kw_ref
        t_sc[slot, :, 0:n] = _dot_nt(q4, k_ref[klo:hi, :])
        if kind == "sel":
            picked = _dot(sel_ref[lo:hi, :], ex_ref[:, 0:hi])
            bias = jnp.where(picked > 0.5, 0.0, NEG)
            if lo > 0:
                bs_sc[bias_slot(j), :, 0:lo] = bias[:, 0:lo]
            bs_sc[bias_slot(j), :, lo:hi] = bias[:, lo:hi] + _tri_bias(tq, True)

    def stage_softmax(j):
        kind, lo, hi, klo = jobs[j]
        slot = j % n_slots
        n = hi - klo
        for r in range(NSA_REP):
            bias = bs_sc[bias_slot(j), :, 0:n] if kind == "sel" else bw_sc[:, w_cols - n:w_cols]
            t = t_sc[slot, head_rows(r), 0:n] + bias
            p = jnp.exp2(t - jnp.max(t, axis=-1, keepdims=True))
            l_sc[slot, head_rows(r), :] = jnp.sum(p, axis=-1, keepdims=True)
            p_sc[slot, head_rows(r), 0:n] = p.astype(BF16)

    def stage_output(j):
        kind, lo, hi, klo = jobs[j]
        slot = j % n_slots
        n = hi - klo
        v_ref = vs_ref if kind == "sel" else vw_ref
        o4 = _dot(p_sc[slot, :, 0:n], v_ref[klo:hi, :]) / l_sc[slot]
        if kind == "sel":
            os_sc[...] = o4
            return
        gate = jax.nn.sigmoid(gl_ref[lo:hi, :])
        for r in range(NSA_REP):
            sl = slice(r * HEAD_DIM, (r + 1) * HEAD_DIM)
            o = (gate[:, 3 * r:3 * r + 1] * oc_ref[lo:hi, sl]
                 + gate[:, 3 * r + 1:3 * r + 2] * os_sc[head_rows(r), :]
                 + gate[:, 3 * r + 2:3 * r + 3] * o4[head_rows(r), :])
            o_ref[lo:hi, sl] = o.astype(o_ref.dtype)

    for r0 in range(0, len(jobs) + 2 * REGION_JOBS, REGION_JOBS):
        @pl.when(live)
        def _region():
            for j in range(r0, r0 + REGION_JOBS):
                if j < len(jobs):
                    stage_scores(j)
                if 0 <= j - REGION_JOBS < len(jobs):
                    stage_softmax(j - REGION_JOBS)
                if 0 <= j - 2 * REGION_JOBS < len(jobs):
                    stage_output(j - 2 * REGION_JOBS)


def _nsa_attention(q, kk, vv, sel, expand, oc, gl, seq):
    m = q.shape[0]
    b = m // seq
    width = NSA_REP * HEAD_DIM
    qspec = pl.BlockSpec((seq, width), lambda i, g: (i, g))
    ks = pl.BlockSpec((seq, HEAD_DIM), lambda i, g: (i, g))
    kw = pl.BlockSpec((seq, HEAD_DIM), lambda i, g: (i, NSA_GROUPS + g))
    tq = 128
    rows = NSA_REP * tq
    return pl.pallas_call(
        functools.partial(_nsa_kernel, tq=tq),
        grid=(b, NSA_GROUPS),
        scratch_shapes=[pltpu.VMEM((2 * REGION_JOBS, rows, seq), F32), pltpu.VMEM((2 * REGION_JOBS, rows, seq), BF16),
                        pltpu.VMEM((2 * REGION_JOBS, rows, 1), F32), pltpu.VMEM((REGION_JOBS, tq, seq), F32),
                        pltpu.VMEM((tq, WINDOW + tq), F32), pltpu.VMEM((rows, HEAD_DIM), F32)],
        in_specs=[qspec, ks, ks, kw, kw,
                  pl.BlockSpec((None, None, seq, LANES), lambda i, g: (i, g, 0, 0)),
                  pl.BlockSpec((LANES, seq), lambda i, g: (0, 0)),
                  qspec,
                  pl.BlockSpec((seq, LANES), lambda i, g: (i, g))],
        out_specs=qspec,
        out_shape=jax.ShapeDtypeStruct((m, NSA_HEADS * HEAD_DIM), BF16),
        compiler_params=_cparams("arbitrary", "arbitrary"),
        name="nsa_attn",
    )(q, kk, vv, kk, vv, sel, expand, oc, gl)


def _cum_kernel(f_ref, b_ref, o_ref):
    x = f_ref[...] + b_ref[...]
    ls = jnp.minimum(x, 0.0) - jnp.log1p(jnp.exp(-jnp.abs(x)))
    row = lax.broadcasted_iota(jnp.int32, ls.shape, 0)
    d = 1
    while d < ls.shape[0]:
        ls = ls + jnp.where(row >= d, pltpu.roll(ls, d, 0), 0.0)
        d *= 2
    o_ref[...] = ls


def _cum_forget(fl, bias, seq, col_block):
    m = fl.shape[0]
    return pl.pallas_call(
        _cum_kernel,
        grid=(m // seq,),
        in_specs=[pl.BlockSpec((seq, LANES), lambda i: (i, col_block)),
                  pl.BlockSpec((1, LANES), lambda i: (0, 0))],
        out_specs=pl.BlockSpec((seq, LANES), lambda i: (i, 0)),
        out_shape=jax.ShapeDtypeStruct((m, LANES), F32),
        compiler_params=_cparams("arbitrary"),
        name="cum_forget",
    )(fl, bias)


def _fox_kernel(q_ref, k_ref, v_ref, ck_ref, *rest, tq):
    n_cast = (len(rest) - 1) // 2
    o_ref = rest[n_cast]
    _cast_blocks(rest[:n_cast], rest[n_cast + 1:])
    seq = q_ref.shape[0]
    causal_b = _tri_bias(tq, True)
    ck = ck_ref[...] * LOG2E
    for qt in reversed(range(seq // tq)):
        lo, hi = qt * tq, (qt + 1) * tq
        t = _dot_nt(q_ref[lo:hi, :], k_ref[0:hi, :]) - ck[:, 0:hi]
        diag = t[:, lo:hi] + causal_b
        t = diag if qt == 0 else jnp.concatenate([t[:, 0:lo], diag], axis=1)
        o_ref[lo:hi, :] = _softmax_pv(t, v_ref[0:hi, :]).astype(o_ref.dtype)


def _fox_attention(q, k, v, ck, seq, casts=()):
    m = q.shape[0]
    b = m // seq
    blk = pl.BlockSpec((seq, HEAD_DIM), lambda i, h: (i, h))
    cast_specs, cast_shapes = _cast_specs(casts, b * FOX_HEADS, lambda i, h: i * FOX_HEADS + h)
    out = pl.pallas_call(
        functools.partial(_fox_kernel, tq=512),
        grid=(b, FOX_HEADS),
        in_specs=[blk, blk, blk,
                  pl.BlockSpec((None, 1, seq), lambda i, h: (i * FOX_HEADS + h, 0, 0))] + cast_specs,
        out_specs=[blk] + cast_specs,
        out_shape=[jax.ShapeDtypeStruct((m, FOX_HEADS * HEAD_DIM), BF16)] + cast_shapes,
        compiler_params=_cparams("arbitrary", "arbitrary"),
        name="fox_attn",
    )(q, k, v, ck, *casts)
    return out if casts else out[0]


def _merge_kernel(a1_ref, w1_ref, a2_ref, w2_ref, g1_ref, g2_ref, o_ref):
    u1 = _dot(a1_ref[...], w1_ref[...])
    u2 = _dot(a2_ref[...], w2_ref[...])
    o_ref[...] = (g1_ref[...].astype(F32) * u1 + g2_ref[...].astype(F32) * u2).astype(o_ref.dtype)


def _merge(a1, w1, a2, w2, gates):
    m, k = a1.shape
    n = w1.shape[1]
    tm, tn = 1024, 1024
    nb = n // tn
    aspec = pl.BlockSpec((tm, k), lambda j, i: (i, 0))
    wspec = pl.BlockSpec((k, tn), lambda j, i: (0, j))
    return pl.pallas_call(
        _merge_kernel,
        grid=(nb, m // tm),
        in_specs=[aspec, wspec, aspec, wspec,
                  pl.BlockSpec((tm, tn), lambda j, i: (i, j)),
                  pl.BlockSpec((tm, tn), lambda j, i: (i, nb + j))],
        out_specs=pl.BlockSpec((tm, tn), lambda j, i: (i, j)),
        out_shape=jax.ShapeDtypeStruct((m, n), BF16),
        compiler_params=_cparams("arbitrary", "arbitrary"),
        name="merge",
    )(a1, w1, a2, w2, gates, gates)


def _rope_tables(pos):
    inv = ROPE_THETA ** (-jnp.arange(ROPE_HALF, dtype=F32) / ROPE_HALF)
    ang = jnp.asarray(pos, dtype=F32)[:, None] * inv[None, :]
    cos, sin = jnp.cos(ang), jnp.sin(ang)
    n = ang.shape[0]
    gap = HEAD_DIM // 2 - ROPE_HALF
    c = jnp.concatenate([cos, jnp.ones((n, gap), F32), cos, jnp.ones((n, gap), F32)], axis=1)
    s = jnp.concatenate([-sin, jnp.zeros((n, gap), F32), sin, jnp.zeros((n, gap), F32)], axis=1)
    return c, s


def _layer(x2, c_pad, bsz, seq, w_ada, b_ada, norm1_g, norm2_g, w_in, b_forget, nsa_q_norm, nsa_k_norm,
           fox_q_norm, fox_k_norm, cmp_pos_k, cmp_pos_v, w_cmp_k1, w_cmp_k2, w_cmp_v1, w_cmp_v2,
           w_up_nsa, w_up_fox, w_out, w_ff1, w_ff2):
    d = x2.shape[1]
    hd = HEAD_DIM
    tm = 1024
    per_m = seq // tm

    mod = _ada(c_pad, w_ada, b_ada.reshape(1, -1))[:bsz]
    shift1, scale1, gate1, shift2, scale2, gate2 = [t.reshape(bsz, 1, d) for t in jnp.split(mod, 6, axis=-1)]

    h = _normmod(x2, norm1_g.reshape(1, d), scale1, shift1, seq)

    o_q = 0
    o_kv = o_q + NSA_HEADS * hd
    o_gate = o_kv + 3 * 2 * NSA_GROUPS * hd
    o_fox = o_gate + 3 * NSA_HEADS
    o_f = o_fox + 3 * FOX_HEADS * hd
    o_merge = o_f + FOX_HEADS
    gw = NSA_GROUPS * hd

    w_t = w_in.T.astype(BF16)
    fw = FOX_HEADS * hd

    pos = np.arange(seq)
    c_t, s_t = _rope_tables(pos)
    tab_specs = [pl.BlockSpec((tm, hd), lambda j, i: (i % per_m, 0))] * 2
    gspec = pl.BlockSpec((1, MXU_COLS), lambda j, i: (0, 0))
    hspec = pl.BlockSpec((1, hd), lambda j, i: (0, 0))
    ones_spec = pl.BlockSpec((MXU_COLS, MXU_COLS), lambda j, i: (0, 0))
    head_ones = jnp.asarray(np.kron(np.eye(MXU_COLS // hd), np.ones((hd, hd))), dtype=BF16)

    def chunk_gain(g):
        return jnp.tile(g.reshape(1, hd), (1, MXU_COLS // hd))

    q_scale = ATTN_SCALE * LOG2E
    def permuted_heads(lo, n_heads):
        return _head_perm(w_t[lo:lo + n_heads * hd].reshape(n_heads, hd, d), 1).reshape(n_heads * hd, d)
    w_qk = jnp.concatenate([permuted_heads(o_q, NSA_HEADS), permuted_heads(o_kv + 2 * gw, NSA_GROUPS),
                            permuted_heads(o_kv + 4 * gw, NSA_GROUPS)], axis=0)
    g_q = chunk_gain(q_scale * _head_perm(nsa_q_norm, 0))
    g_k = chunk_gain(_head_perm(nsa_k_norm, 0))
    rope_specs = [gspec, ones_spec] + tab_specs
    q_n = _matmul(h, w_qk, _ep_headnorm_rope, BF16, rows=(0, 1024, 2),
                  extras=(g_q, head_ones, c_t, s_t), extra_specs=rope_specs, late=True, name="proj_nsa_q")
    k_sw = _matmul(h, w_qk, _ep_headnorm_rope, BF16, rows=(NSA_HEADS * hd, 0, 1),
                   extras=(g_k, head_ones, c_t, s_t), extra_specs=rope_specs, late=True, name="proj_nsa_k")
    v_sw = _matmul(h, w_t, _ep_raw, BF16, tn=gw, rows=(o_kv + 3 * gw, 2 * gw, 2), name="proj_nsa_v")
    kv_c = _matmul(h, w_t, _ep_raw, F32, rows=(o_kv, 2 * gw, 1), name="proj_nsa_cmp")
    fq = _matmul(h, w_t, _ep_headnorm, BF16, rows=(o_fox, 1024, 2),
                 extras=(q_scale * fox_q_norm.reshape(1, hd),), extra_specs=[hspec], late=True, name="proj_fox_q")
    fk = _matmul(h, w_t, _ep_headnorm, BF16, rows=(o_fox + fw, 1024, 2),
                 extras=(fox_k_norm.reshape(1, hd),), extra_specs=[hspec], late=True, name="proj_fox_k")
    fv = _matmul(h, w_t, _ep_raw, BF16, rows=(o_fox + 2 * fw, 1024, 2), name="proj_fox_v")
    g_merge = _matmul(h, w_t, _ep_sigmoid, BF16, rows=(o_merge, 1024, 2 * d // 1024), late=True, name="proj_merge")

    per_group = NSA_REP * 3
    n_gate = NSA_HEADS * 3
    w_small = jnp.concatenate([w_t[o_gate:o_fox], w_t[o_f:o_merge]], axis=0)
    w_small = jnp.pad(w_small, ((0, LANES - w_small.shape[0]), (0, 0)))
    logits = _matmul(h, w_small, _ep_raw, F32, tn=LANES, rows=(0, 0, 1), name="proj_small")
    lane_blocks = [logits[:, g * per_group:(g + 1) * per_group] for g in range(NSA_GROUPS)]
    lane_blocks.append(logits[:, n_gate:n_gate + FOX_HEADS])
    small = jnp.concatenate([jnp.pad(blk, ((0, 0), (0, LANES - blk.shape[1]))) for blk in lane_blocks], axis=1)

    n_rows = seq // CMP_STRIDE
    end_pos = np.arange(n_rows) * CMP_STRIDE + CMP_BLOCK - 1
    ce, se = _rope_tables(end_pos)
    k_cmp, v_cmp = _compress(
        kv_c, seq, cmp_pos_k.reshape(2, CMP_STRIDE * hd), cmp_pos_v.reshape(2, CMP_STRIDE * hd),
        w_cmp_k1.astype(BF16), _head_perm(w_cmp_k2, 1).astype(BF16), w_cmp_v1.astype(BF16), w_cmp_v2.astype(BF16),
        _head_perm(nsa_k_norm, 0).reshape(1, hd), ce, se)

    n_slc = seq // SEL_BLOCK
    ci = np.arange(LANES)[:, None] * CMP_STRIDE
    sj = np.arange(LANES)[None, :] * SEL_BLOCK
    overlap = ((ci < sj + SEL_BLOCK) & (ci + CMP_BLOCK > sj) & (np.arange(LANES)[None, :] < n_slc)
               & (np.arange(LANES)[:, None] < n_rows - 1)).astype(np.float32)
    o_c, sel = _cmp_attention(q_n, k_cmp, v_cmp, jnp.asarray(overlap.T), seq)

    expand = (np.arange(LANES)[:, None] == (np.arange(seq)[None, :] >> SEL_SHIFT)).astype(np.float32)
    o_nsa = _nsa_attention(q_n, k_sw, v_sw, sel, jnp.asarray(expand, dtype=BF16), o_c, small, seq)

    f_bias = jnp.pad(b_forget, (0, LANES - FOX_HEADS)).reshape(1, LANES)
    cum = _cum_forget(small, f_bias, seq, NSA_GROUPS)
    ck = cum[:, :FOX_HEADS].reshape(bsz, seq, FOX_HEADS).transpose(0, 2, 1).reshape(bsz * FOX_HEADS, 1, seq)
    o_fox, up_nsa_b, up_fox_b, w_out_b, w_ff1_b = _fox_attention(
        fq, fk, fv, ck, seq, casts=(w_up_nsa, w_up_fox, w_out, w_ff1))

    y = _merge(o_nsa, up_nsa_b, o_fox, up_fox_b, g_merge)
    tn_out = 1024
    res_specs = [pl.BlockSpec((tm, tn_out), lambda j, i: (i, j)),
                 pl.BlockSpec((None, 1, tn_out), lambda j, i: (i // per_m, 0, j))]
    x_mid = _matmul(y, w_out_b, _ep_residual, F32, extras=(x2, gate1), extra_specs=res_specs,
                    tn=tn_out, name="out_proj")

    h2 = _normmod(x_mid, norm2_g.reshape(1, d), scale2, shift2, seq)
    hid, w_ff2_b = _matmul(h2, w_ff1_b, _ep_relu2, BF16, casts=(w_ff2,), name="ff1")
    return _matmul_k_residual(hid, w_ff2_b, x_mid, gate2, seq, name="ff2")


def kernel(x, c, w_ada, b_ada, norm1_g, norm2_g, w_in, b_forget, nsa_q_norm, nsa_k_norm, fox_q_norm, fox_k_norm, cmp_pos_k, cmp_pos_v, w_cmp_k1, w_cmp_k2, w_cmp_v1, w_cmp_v2, w_up_nsa, w_up_fox, w_out, w_ff1, w_ff2):
    bsz, seq, d = x.shape
    x2 = x.reshape(bsz * seq, d)
    c_pad = jnp.pad(c, ((0, 8 - bsz), (0, 0)))
    params = (w_ada, b_ada, norm1_g, norm2_g, w_in, b_forget, nsa_q_norm, nsa_k_norm, fox_q_norm, fox_k_norm,
              cmp_pos_k, cmp_pos_v, w_cmp_k1, w_cmp_k2, w_cmp_v1, w_cmp_v2, w_up_nsa, w_up_fox, w_out,
              w_ff1, w_ff2)
    for layer in range(w_ada.shape[0]):
        x2 = _layer(x2, c_pad, bsz, seq, *[p[layer] for p in params])
    return x2.reshape(bsz, seq, d)
```
